```python
import math
import jax, jax.numpy as jnp
from jax import lax
import numpy as np

D_MODEL = 1024
BATCH = 4
SEQ = 4096
DEPTH = 1
DEC_BATCH = 8
DEC_SEQ = 64
PAST_LEN = 1024

CHUNK = 64
MIX_WIDTH = D_MODEL
S5_WIDTH = MIX_WIDTH // 2
S5_GROUP = 16
S5_GROUPS = S5_WIDTH // S5_GROUP
S5_STATE = 64
GDN_WIDTH = MIX_WIDTH - S5_WIDTH
GDN_HEAD_DIM = 128
GDN_HEADS = GDN_WIDTH // GDN_HEAD_DIM
CONV_WIDTH = 4
CONV_CH = 3 * GDN_WIDTH
D_FF = -(-8 * D_MODEL // (3 * 256)) * 256
OFF_Q = S5_WIDTH
OFF_K = OFF_Q + GDN_WIDTH
OFF_V = OFF_K + GDN_WIDTH
OFF_Z = OFF_V + GDN_WIDTH
OFF_B = OFF_Z + GDN_WIDTH
OFF_A = OFF_B + GDN_HEADS
IN_COLS = OFF_A + GDN_HEADS
SPLITS = (OFF_Q, OFF_K, OFF_V, OFF_Z, OFF_B, OFF_A)
ALPHA = (2 * DEPTH) ** 0.25
BETA_INIT = (8 * DEPTH) ** -0.25
LN_EPS = 1e-5
RMS_EPS = 1e-6

kernel_name = 'hymba_s5_gdn_deepnorm_adaln_stream_step'

F32 = jnp.float32


def layer_norm(x, g=None, b=None):
    xf = x.astype(F32)
    mu = jnp.mean(xf, -1, keepdims=True)
    var = jnp.mean(jnp.square(xf - mu), -1, keepdims=True)
    y = (xf - mu) * lax.rsqrt(var + LN_EPS)
    if g is not None:
        y = y * g.astype(F32) + b.astype(F32)
    return y.astype(x.dtype)


def l2norm(t):
    return t * lax.rsqrt(jnp.sum(t * t, -1, keepdims=True) + RMS_EPS)


def causal_conv(x, buf, w):
    L = x.shape[1]
    xp = jnp.concatenate([buf.astype(x.dtype), x], axis=1)
    y = sum(xp[:, j:j + L] * w[j] for j in range(CONV_WIDTH))
    return y, xp[:, -(CONV_WIDTH - 1):]


def s5_mixer(u, h0_re, h0_im, a_re, a_im, log_dt, b_re, b_im, c_re, c_im, d_skip, w_glu):
    Bsz, L, _ = u.shape
    uf = u.astype(F32).reshape(Bsz, L, S5_GROUPS, S5_GROUP)
    a_re = a_re.astype(F32)
    a_im = a_im.astype(F32)
    dt = jnp.exp(log_dt.astype(F32))[:, None]
    mag = jnp.exp(a_re * dt)
    ab_re, ab_im = mag * jnp.cos(a_im * dt), mag * jnp.sin(a_im * dt)
    den = a_re * a_re + a_im * a_im
    num_re, num_im = ab_re - 1.0, ab_im
    f_re = (num_re * a_re + num_im * a_im) / den
    f_im = (num_im * a_re - num_re * a_im) / den
    bu_re = jnp.einsum('blgj,gpj->blgp', uf, b_re.astype(F32))
    bu_im = jnp.einsum('blgj,gpj->blgp', uf, b_im.astype(F32))
    x_re = f_re * bu_re - f_im * bu_im
    x_im = f_re * bu_im + f_im * bu_re
    h0_re = h0_re.astype(F32)
    h0_im = h0_im.astype(F32)
    x_re = x_re.at[:, 0].add(ab_re * h0_re - ab_im * h0_im)
    x_im = x_im.at[:, 0].add(ab_re * h0_im + ab_im * h0_re)
    A_re = jnp.broadcast_to(ab_re, x_re.shape)
    A_im = jnp.broadcast_to(ab_im, x_re.shape)

    def combine(e1, e2):
        a1r, a1i, b1r, b1i = e1
        a2r, a2i, b2r, b2i = e2
        return (a1r * a2r - a1i * a2i,
                a1r * a2i + a1i * a2r,
                a2r * b1r - a2i * b1i + b2r,
                a2r * b1i + a2i * b1r + b2i)

    _, _, h_re, h_im = lax.associative_scan(combine, (A_re, A_im, x_re, x_im), axis=1)
    y = (jnp.einsum('blgp,gjp->blgj', h_re, c_re.astype(F32))
         - jnp.einsum('blgp,gjp->blgj', h_im, c_im.astype(F32))
         + d_skip.astype(F32).reshape(S5_GROUPS, S5_GROUP) * uf)
    y = jax.nn.gelu(y.reshape(Bsz, L, S5_WIDTH))
    za, zb = jnp.split(y @ w_glu.astype(F32), 2, axis=-1)
    return za * jax.nn.sigmoid(zb), h_re[:, -1], h_im[:, -1]


def chunk_gated_delta(q, k, v, g, beta, s0):
    Bsz, H, L, Dh = q.shape
    C = min(CHUNK, L)
    N = L // C
    q, k, v = (t.reshape(Bsz, H, N, C, Dh) for t in (q, k, v))
    g, beta = (t.reshape(Bsz, H, N, C) for t in (g, beta))
    decay = jnp.cumsum(g, axis=-1)
    causal = jnp.tril(jnp.ones((C, C), bool))
    strict = jnp.tril(jnp.ones((C, C), bool), -1)
    diff = decay[..., :, None] - decay[..., None, :]
    dmask = jnp.exp(jnp.where(causal, diff, -jnp.inf))
    k_beta = k * beta[..., None]
    v_beta = v * beta[..., None]
    A = jnp.where(strict, jnp.einsum('bhnid,bhnjd->bhnij', k_beta, k) * dmask, 0.0)
    eye = jnp.eye(C, dtype=F32)
    T = lax.linalg.triangular_solve(eye + A, jnp.broadcast_to(eye, A.shape),
                                    left_side=True, lower=True)
    u = T @ v_beta
    w = T @ (k_beta * jnp.exp(decay)[..., None])
    qk = jnp.einsum('bhnid,bhnjd->bhnij', q, k) * dmask
    q_dec = q * jnp.exp(decay)[..., None]
    k_dec = k * jnp.exp(decay[..., -1:] - decay)[..., None]
    g_last = jnp.exp(decay[..., -1])

    def step(S, inp):
        u_n, w_n, qk_n, qd_n, kd_n, gl_n = inp
        v_new = u_n - w_n @ S
        o_n = qd_n @ S + qk_n @ v_new
        S = S * gl_n[..., None, None] + jnp.einsum('bhck,bhcv->bhkv', kd_n, v_new)
        return S, o_n

    xs = tuple(jnp.moveaxis(t, 2, 0) for t in (u, w, qk, q_dec, k_dec, g_last))
    S, o = lax.scan(step, s0, xs)
    return jnp.moveaxis(o, 0, 2).reshape(Bsz, H, L, Dh), S


def gdn_mixer(q, k, v, z, b_raw, a_raw, s0, conv_buf, conv_w, a_log, dt_bias, norm_g):
    Bsz, L, _ = q.shape
    qkv, new_buf = causal_conv(jnp.concatenate([q, k, v], -1), conv_buf, conv_w)
    qkv = jax.nn.silu(qkv).astype(F32)
    heads = lambda t: t.reshape(Bsz, L, GDN_HEADS, GDN_HEAD_DIM).transpose(0, 2, 1, 3)
    qh, kh, vh = (heads(t) for t in jnp.split(qkv, 3, axis=-1))
    qh = l2norm(qh) * GDN_HEAD_DIM ** -0.5
    kh = l2norm(kh)
    beta = jax.nn.sigmoid(b_raw.astype(F32)).transpose(0, 2, 1)
    g = -jnp.exp(a_log.astype(F32)) * jax.nn.softplus(a_raw.astype(F32) + dt_bias.astype(F32))
    o, s_new = chunk_gated_delta(qh, kh, vh, g.transpose(0, 2, 1), beta, s0.astype(F32))
    o = o.transpose(0, 2, 1, 3)
    o = o * lax.rsqrt(jnp.mean(o * o, -1, keepdims=True) + RMS_EPS) * norm_g.astype(F32)
    o = o * jax.nn.silu(z.astype(F32)).reshape(Bsz, L, GDN_HEADS, GDN_HEAD_DIM)
    return o.reshape(Bsz, L, GDN_WIDTH), s_new, new_buf


def trunk_layer(x, c, h_re, h_im, s_gdn, conv_buf,
                w_ada, b_ada, w_in, s5_a_re, s5_a_im, s5_log_dt, s5_b_re, s5_b_im,
                s5_c_re, s5_c_im, s5_d, w_s5_glu, gdn_conv_w, gdn_a_log, gdn_dt_bias,
                gdn_norm_g, w_out, ln1_g, ln1_b, w_ffn_up, w_ffn_down, ln2_g, ln2_b):
    mod = (jax.nn.silu(c) @ w_ada + b_ada)[:, None, :]
    sh1, sc1, g1, sh2, sc2, g2 = jnp.split(mod, 6, axis=-1)
    h = layer_norm(x) * (1 + sc1) + sh1
    u, q, k, v, z, b_raw, a_raw = jnp.split(h @ w_in, SPLITS, axis=-1)
    y_a, h_re, h_im = s5_mixer(u, h_re, h_im, s5_a_re, s5_a_im, s5_log_dt, s5_b_re, s5_b_im,
                               s5_c_re, s5_c_im, s5_d, w_s5_glu)
    y_b, s_gdn, conv_buf = gdn_mixer(q, k, v, z, b_raw, a_raw, s_gdn, conv_buf, gdn_conv_w,
                                     gdn_a_log, gdn_dt_bias, gdn_norm_g)
    mix = jnp.concatenate([y_a.astype(x.dtype), y_b.astype(x.dtype)], -1) @ w_out
    x = layer_norm(ALPHA * x + (1 + g1) * mix, ln1_g, ln1_b)
    h = layer_norm(x) * (1 + sc2) + sh2
    gate, up = jnp.split(h @ w_ffn_up, 2, axis=-1)
    ffn = (jax.nn.silu(gate) * up) @ w_ffn_down
    x = layer_norm(ALPHA * x + (1 + g2) * ffn, ln2_g, ln2_b)
    return x, h_re, h_im, s_gdn, conv_buf


def setup_inputs(seed: int = 0) -> dict:
    key = jax.random.key(seed)
    ks = jax.random.split(key, 32)
    nrm = lambda k, shape, s: s * jax.random.normal(k, shape, F32)
    G, P, J, H, Dh = S5_GROUPS, S5_STATE, S5_GROUP, GDN_HEADS, GDN_HEAD_DIM
    n_idx = jnp.arange(P, dtype=F32)
    dt0 = jnp.exp(jax.random.uniform(ks[22], (DEPTH, H), F32, math.log(1e-3), math.log(1e-1)))
    return {
        'x_prompt': nrm(ks[0], (BATCH, SEQ, D_MODEL), 1.0),
        'x_sample': nrm(ks[1], (DEC_BATCH, DEC_SEQ, D_MODEL), 1.0),
        'state_s5_re': nrm(ks[2], (DEPTH, DEC_BATCH, G, P), 0.3),
        'state_s5_im': nrm(ks[3], (DEPTH, DEC_BATCH, G, P), 0.3),
        'state_gdn': nrm(ks[4], (DEPTH, DEC_BATCH, H, Dh, Dh), 0.3),
        'cache_gdn_conv': nrm(ks[5], (DEPTH, DEC_BATCH, CONV_WIDTH - 1, CONV_CH), 1.0),
        'c_prompt': nrm(ks[6], (BATCH, D_MODEL), 1.0),
        'c_sample': nrm(ks[7], (DEC_BATCH, D_MODEL), 1.0),
        'w_ada': nrm(ks[8], (DEPTH, D_MODEL, 6 * D_MODEL), 0.1 * D_MODEL ** -0.5),
        'b_ada': nrm(ks[9], (DEPTH, 6 * D_MODEL), 0.01),
        'w_in': nrm(ks[10], (DEPTH, D_MODEL, IN_COLS), D_MODEL ** -0.5),
        's5_a_re': -0.5 + nrm(ks[11], (DEPTH, G, P), 0.01),
        's5_a_im': jnp.pi * n_idx + nrm(ks[12], (DEPTH, G, P), 0.01),
        's5_log_dt': jax.random.uniform(ks[13], (DEPTH, G), F32, math.log(1e-3), math.log(1e-1)),
        's5_b_re': nrm(ks[14], (DEPTH, G, P, J), (2 * J) ** -0.5),
        's5_b_im': nrm(ks[15], (DEPTH, G, P, J), (2 * J) ** -0.5),
        's5_c_re': nrm(ks[16], (DEPTH, G, J, P), P ** -0.5),
        's5_c_im': nrm(ks[17], (DEPTH, G, J, P), P ** -0.5),
        's5_d': nrm(ks[18], (DEPTH, S5_WIDTH), 1.0),
        'w_s5_glu': nrm(ks[19], (DEPTH, S5_WIDTH, 2 * S5_WIDTH), S5_WIDTH ** -0.5),
        'gdn_conv_w': nrm(ks[20], (DEPTH, CONV_WIDTH, CONV_CH), CONV_WIDTH ** -0.5),
        'gdn_a_log': jnp.log(jax.random.uniform(ks[21], (DEPTH, H), F32, 1.0, 16.0)),
        'gdn_dt_bias': dt0 + jnp.log(-jnp.expm1(-dt0)),
        'gdn_norm_g': 1.0 + nrm(ks[23], (DEPTH, Dh), 0.1),
        'w_out': nrm(ks[24], (DEPTH, MIX_WIDTH, D_MODEL), BETA_INIT * MIX_WIDTH ** -0.5),
        'ln1_g': 1.0 + nrm(ks[25], (DEPTH, D_MODEL), 0.1),
        'ln1_b': nrm(ks[26], (DEPTH, D_MODEL), 0.01),
        'w_ffn_up': nrm(ks[27], (DEPTH, D_MODEL, 2 * D_FF), D_MODEL ** -0.5),
        'w_ffn_down': nrm(ks[28], (DEPTH, D_FF, D_MODEL), BETA_INIT * D_FF ** -0.5),
        'ln2_g': 1.0 + nrm(ks[29], (DEPTH, D_MODEL), 0.1),
        'ln2_b': nrm(ks[30], (DEPTH, D_MODEL), 0.01),
    }


def reference(x_prompt, x_sample, state_s5_re, state_s5_im, state_gdn, cache_gdn_conv,
              c_prompt, c_sample, w_ada, b_ada, w_in, s5_a_re, s5_a_im, s5_log_dt,
              s5_b_re, s5_b_im, s5_c_re, s5_c_im, s5_d, w_s5_glu, gdn_conv_w, gdn_a_log,
              gdn_dt_bias, gdn_norm_g, w_out, ln1_g, ln1_b, w_ffn_up, w_ffn_down, ln2_g, ln2_b):
    bp = x_prompt.shape[0]
    xp, xs = x_prompt, x_sample
    p_re, p_im, p_gdn, p_conv = [], [], [], []
    s_re, s_im, s_gdn, s_conv = [], [], [], []
    for l in range(DEPTH):
        lw = (w_ada[l], b_ada[l], w_in[l], s5_a_re[l], s5_a_im[l], s5_log_dt[l], s5_b_re[l],
              s5_b_im[l], s5_c_re[l], s5_c_im[l], s5_d[l], w_s5_glu[l], gdn_conv_w[l],
              gdn_a_log[l], gdn_dt_bias[l], gdn_norm_g[l], w_out[l], ln1_g[l], ln1_b[l],
              w_ffn_up[l], w_ffn_down[l], ln2_g[l], ln2_b[l])
        xp, hr, hi, sg, cb = trunk_layer(
            xp, c_prompt,
            jnp.zeros((bp, S5_GROUPS, S5_STATE), F32),
            jnp.zeros((bp, S5_GROUPS, S5_STATE), F32),
            jnp.zeros((bp, GDN_HEADS, GDN_HEAD_DIM, GDN_HEAD_DIM), F32),
            jnp.zeros((bp, CONV_WIDTH - 1, CONV_CH), x_prompt.dtype),
            *lw)
        p_re.append(hr); p_im.append(hi); p_gdn.append(sg); p_conv.append(cb)
        xs, hr, hi, sg, cb = trunk_layer(
            xs, c_sample, state_s5_re[l], state_s5_im[l], state_gdn[l], cache_gdn_conv[l], *lw)
        s_re.append(hr); s_im.append(hi); s_gdn.append(sg); s_conv.append(cb)
    return (xp, xs,
            jnp.stack(p_re), jnp.stack(p_im), jnp.stack(p_gdn), jnp.stack(p_conv),
            jnp.stack(s_re), jnp.stack(s_im), jnp.stack(s_gdn), jnp.stack(s_conv))
```

```python
import functools

import jax
import jax.numpy as jnp
from jax import lax
from jax.experimental import pallas as pl
from jax.experimental.pallas import tpu as pltpu

F32 = jnp.float32
BF16 = jnp.bfloat16

D_MODEL = 1024
S5_WIDTH = 512
S5_GROUP = 16
S5_GROUPS = 32
S5_STATE = 64
GDN_WIDTH = 512
GDN_HEAD_DIM = 128
GDN_HEADS = 4
CONV_WIDTH = 4
CONV_CH = 3 * GDN_WIDTH
D_FF = 2816
GDN_CHUNK = 64
LN_EPS = 1e-5
RMS_EPS = 1e-6
DEPTH = 1
ALPHA = (2 * DEPTH) ** 0.25

LANES = 128
S5_T = 8
S5_BLK_GROUPS = LANES // S5_GROUP
S5_NBLK = S5_GROUPS // S5_BLK_GROUPS
S5_BLK_STATE = S5_BLK_GROUPS * S5_STATE
IN_PAD = S5_WIDTH + 4 * GDN_WIDTH + LANES
VMEM_LIMIT = 56 * 1024 * 1024

HI = lax.Precision.HIGHEST


def _cparams(*sem):
    return pltpu.CompilerParams(dimension_semantics=sem, vmem_limit_bytes=VMEM_LIMIT)


def _ln(x):
    mu = jnp.mean(x, -1, keepdims=True)
    xc = x - mu
    var = jnp.mean(xc * xc, -1, keepdims=True)
    return xc * lax.rsqrt(var + LN_EPS)


def _sigmoid(x):
    return 1.0 / (1.0 + jnp.exp(-x))


def _silu(x):
    return x * _sigmoid(x)


def _ada_kernel(c_ref, w_ref, b_ref, o_ref):
    c = c_ref[...]
    o_ref[...] = jnp.dot(_silu(c), w_ref[...], preferred_element_type=F32,
                         precision=HI) + b_ref[...]


def _ada(c, w_ada, b_ada):
    rows = c.shape[0]
    ncol = w_ada.shape[1]
    blk = D_MODEL
    return pl.pallas_call(
        _ada_kernel,
        grid=(ncol // blk,),
        in_specs=[pl.BlockSpec((rows, D_MODEL), lambda j: (0, 0)),
                  pl.BlockSpec((D_MODEL, blk), lambda j: (0, j)),
                  pl.BlockSpec((1, blk), lambda j: (0, j))],
        out_specs=pl.BlockSpec((rows, blk), lambda j: (0, j)),
        out_shape=jax.ShapeDtypeStruct((rows, ncol), F32),
        compiler_params=_cparams("arbitrary"),
        name="ada",
    )(c, w_ada, b_ada.reshape(1, ncol))


def _s5prep_kernel(are_ref, aim_ref, ldt_ref, bre_ref, bim_ref, btre_ref, btim_ref,
                   cre_ref, cim_ref, k_ref, p_ref, q_ref, at_ref):
    a_re = are_ref[...]
    a_im = aim_ref[...]
    dt = jnp.exp(ldt_ref[...])
    den = a_re * a_re + a_im * a_im

    def lam_pow(tau):
        mag = jnp.exp(a_re * dt * tau)
        return mag * jnp.cos(a_im * dt * tau), mag * jnp.sin(a_im * dt * tau)

    l_re, l_im = lam_pow(1.0)
    n_re, n_im = l_re - 1.0, l_im
    f_re = (n_re * a_re + n_im * a_im) / den
    f_im = (n_im * a_re - n_re * a_im) / den
    c_re = cre_ref[...]
    c_im = cim_ref[...]
    bt_re = btre_ref[...]
    bt_im = btim_ref[...]
    b_re = bre_ref[...]
    b_im = bim_ref[...]
    bdot = functools.partial(jnp.einsum, 'gjp,gpi->gji', precision=HI,
                             preferred_element_type=F32)
    for tau in range(S5_T + 1):
        p_re, p_im = (jnp.ones_like(a_re), jnp.zeros_like(a_re)) if tau == 0 else lam_pow(float(tau))
        if tau == S5_T:
            at_ref[0] = p_re
            at_ref[1] = p_im
        if tau < S5_T:
            g_re = p_re * f_re - p_im * f_im
            g_im = p_re * f_im + p_im * f_re
            cg_re = c_re * g_re[:, None, :] - c_im * g_im[:, None, :]
            cg_im = c_re * g_im[:, None, :] + c_im * g_re[:, None, :]
            k_ref[tau] = bdot(cg_re, b_re) - bdot(cg_im, b_im)
            s = S5_T - 1 - tau
            p_ref[s, 0] = g_re[:, None, :] * bt_re - g_im[:, None, :] * bt_im
            p_ref[s, 1] = g_re[:, None, :] * bt_im + g_im[:, None, :] * bt_re
        if tau >= 1:
            q_ref[tau - 1, 0] = c_re * p_re[:, None, :] - c_im * p_im[:, None, :]
            q_ref[tau - 1, 1] = -(c_re * p_im[:, None, :] + c_im * p_re[:, None, :])


def _s5_prep(a_re, a_im, log_dt, b_re, b_im, c_re, c_im):
    G, P, J, T = S5_GROUPS, S5_STATE, S5_GROUP, S5_T
    bt_re = jnp.swapaxes(b_re, 1, 2)
    bt_im = jnp.swapaxes(b_im, 1, 2)
    kt, pm, qm, at = pl.pallas_call(
        _s5prep_kernel,
        out_shape=[jax.ShapeDtypeStruct((T, G, J, J), F32),
                   jax.ShapeDtypeStruct((T, 2, G, J, P), F32),
                   jax.ShapeDtypeStruct((T, 2, G, J, P), F32),
                   jax.ShapeDtypeStruct((2, G, P), F32)],
        compiler_params=pltpu.CompilerParams(vmem_limit_bytes=VMEM_LIMIT),
        name="s5prep",
    )(a_re, a_im, log_dt.reshape(G, 1), b_re, b_im, bt_re, bt_im, c_re, c_im)

    NB, BG = S5_NBLK, S5_BLK_GROUPS
    eye_g = jnp.eye(BG, dtype=F32)
    tau = jnp.arange(T)[None, :] - jnp.arange(T)[:, None]
    kst = jnp.where((tau >= 0)[:, :, None, None, None], kt[jnp.clip(tau, 0, T - 1)], 0.0)
    kst = kst.reshape(T, T, NB, BG, J, J)
    w = jnp.einsum('stkgji,gh->ksgithj', kst, eye_g).reshape(NB, T * LANES, T * LANES)
    pmr = pm.reshape(T, 2, NB, BG, J, P)
    pmat = jnp.einsum('sckgip,gh->ksgichp', pmr, eye_g).reshape(NB, T * LANES, 2 * S5_BLK_STATE)
    qmr = qm.reshape(T, 2, NB, BG, J, P)
    qmat = jnp.einsum('tckgjp,gh->kcgpthj', qmr, eye_g).reshape(NB, 2 * S5_BLK_STATE, T * LANES)
    a_t = at.reshape(2, NB, 1, S5_BLK_STATE).transpose(1, 2, 0, 3).reshape(NB, 1, 2 * S5_BLK_STATE)
    return w.astype(BF16), pmat.astype(BF16), qmat.astype(BF16), a_t


def _s5_kernel(u_ref, h0_ref, w_ref, p_ref, q_ref, at_ref, d_ref, y_ref, hT_ref,
               lhs_ref, x_ref, hp_ref, *, bt, n):
    T, half = S5_T, S5_BLK_STATE
    for b in range(bt):
        for t in range(T):
            lhs_ref[b * n:(b + 1) * n, t * LANES:(t + 1) * LANES] = (
                u_ref[b, pl.ds(t, n, stride=T), :].astype(BF16))
    lhs = lhs_ref[...]
    x_ref[...] = jnp.dot(lhs, p_ref[0], preferred_element_type=F32)
    a_re = at_ref[0, :, :half]
    a_im = at_ref[0, :, half:]
    for b in range(bt):
        def step(i, h):
            h_re, h_im = h
            row = b * n + i
            hp_ref[pl.ds(row, 1), :half] = h_re
            hp_ref[pl.ds(row, 1), half:] = h_im
            x_re = x_ref[pl.ds(row, 1), :half]
            x_im = x_ref[pl.ds(row, 1), half:]
            return (a_re * h_re - a_im * h_im + x_re, a_re * h_im + a_im * h_re + x_im)
        h_re, h_im = lax.fori_loop(0, n, step, (h0_ref[b, 0, :, :half], h0_ref[b, 0, :, half:]))
        hT_ref[b, 0, :, :half] = h_re
        hT_ref[b, 0, :, half:] = h_im
    y = (jnp.dot(lhs, w_ref[0], preferred_element_type=F32)
         + jnp.dot(hp_ref[...].astype(BF16), q_ref[0], preferred_element_type=F32))
    d = d_ref[...]
    for b in range(bt):
        for t in range(T):
            y_ref[b, pl.ds(t, n, stride=T), :] = (
                y[b * n:(b + 1) * n, t * LANES:(t + 1) * LANES]
                + d * u_ref[b, pl.ds(t, n, stride=T), :])


def _s5(u, h0, w, pmat, qmat, a_t, d_skip, bt):
    B, L, _ = u.shape
    n = L // S5_T
    tl, st = S5_T * LANES, 2 * S5_BLK_STATE
    return pl.pallas_call(
        functools.partial(_s5_kernel, bt=bt, n=n),
        grid=(S5_NBLK, B // bt),
        in_specs=[pl.BlockSpec((bt, L, LANES), lambda k, b: (b, 0, k)),
                  pl.BlockSpec((bt, 1, 1, st), lambda k, b: (b, k, 0, 0)),
                  pl.BlockSpec((1, tl, tl), lambda k, b: (k, 0, 0)),
                  pl.BlockSpec((1, tl, st), lambda k, b: (k, 0, 0)),
                  pl.BlockSpec((1, st, tl), lambda k, b: (k, 0, 0)),
                  pl.BlockSpec((1, 1, st), lambda k, b: (k, 0, 0)),
                  pl.BlockSpec((1, LANES), lambda k, b: (0, k))],
        out_specs=[pl.BlockSpec((bt, L, LANES), lambda k, b: (b, 0, k)),
                   pl.BlockSpec((bt, 1, 1, st), lambda k, b: (b, k, 0, 0))],
        out_shape=[jax.ShapeDtypeStruct((B, L, S5_WIDTH), F32),
                   jax.ShapeDtypeStruct((B, S5_NBLK, 1, st), F32)],
        scratch_shapes=[pltpu.VMEM((bt * n, tl), BF16),
                        pltpu.VMEM((bt * n, st), F32),
                        pltpu.VMEM((bt * n, st), F32)],
        compiler_params=_cparams("arbitrary", "arbitrary"),
        name="s5",
    )(u, h0, w, pmat, qmat, a_t, d_skip.reshape(1, S5_WIDTH))


def _s5_state_in(h_re, h_im):
    B = h_re.shape[0]
    r = h_re.reshape(B, S5_NBLK, 1, S5_BLK_STATE)
    i = h_im.reshape(B, S5_NBLK, 1, S5_BLK_STATE)
    return jnp.concatenate([r, i], -1)


def _s5_state_out(h):
    B = h.shape[0]
    h = h.reshape(B, S5_NBLK, 2, S5_BLK_GROUPS, S5_STATE)
    return (h[:, :, 0].reshape(B, S5_GROUPS, S5_STATE), h[:, :, 1].reshape(B, S5_GROUPS, S5_STATE))


def _inproj_kernel(x_ref, sh_ref, sc_ref, w_ref, u_ref, qkv_ref, z_ref, ba_ref):
    bt, lt, d = x_ref.shape
    h = _ln(x_ref[...]) * (1.0 + sc_ref[:, 0]) + sh_ref[:, 0]
    out = jnp.dot(h.reshape(bt * lt, d).astype(BF16), w_ref[...], preferred_element_type=F32)
    o_q, o_z, o_b = S5_WIDTH, S5_WIDTH + CONV_CH, S5_WIDTH + CONV_CH + GDN_WIDTH
    u_ref[...] = out[:, :o_q].reshape(bt, lt, S5_WIDTH)
    qkv_ref[...] = out[:, o_q:o_z].reshape(bt, lt, CONV_CH)
    z_ref[...] = out[:, o_z:o_b].reshape(bt, lt, GDN_WIDTH)
    ba_ref[...] = out[:, o_b:].reshape(bt, lt, LANES)


def _inproj(x, mod, w_in_p, bt, lt):
    B, L, _ = x.shape
    xs = lambda w: pl.BlockSpec((bt, lt, w), lambda b, i: (b, i, 0))
    ms = lambda idx: pl.BlockSpec((bt, 1, 1, D_MODEL), lambda b, i: (b, idx, 0, 0))
    return pl.pallas_call(
        _inproj_kernel,
        grid=(B // bt, L // lt),
        in_specs=[xs(D_MODEL), ms(0), ms(1),
                  pl.BlockSpec((D_MODEL, IN_PAD), lambda b, i: (0, 0))],
        out_specs=[xs(S5_WIDTH), xs(CONV_CH), xs(GDN_WIDTH), xs(LANES)],
        out_shape=[jax.ShapeDtypeStruct((B, L, w), F32) for w in (S5_WIDTH, CONV_CH, GDN_WIDTH, LANES)],
        compiler_params=_cparams("arbitrary", "arbitrary"),
        name="inproj",
    )(x, mod, mod, w_in_p)


def _mm(a, b):
    return jnp.dot(a.astype(BF16), b.astype(BF16), preferred_element_type=F32)


def _mm_nt(a, b):
    return lax.dot_general(a.astype(BF16), b.astype(BF16), (((1,), (1,)), ((), ())),
                           preferred_element_type=F32)


def _mm_tn(a, b):
    return lax.dot_general(a.astype(BF16), b.astype(BF16), (((0,), (0,)), ((), ())),
                           preferred_element_type=F32)


def _mm_hi(a, b):
    return jnp.dot(a, b, preferred_element_type=F32, precision=HI)


def _gdn_kernel(qkv_ref, z_ref, ba_ref, s0_ref, cb_ref, cw_ref, alog_ref, dtb_ref, ng_ref,
                y_ref, sT_ref, cbo_ref, xbuf_ref, s_ref, *, nchunk):
    C, Dh, H = GDN_CHUNK, GDN_HEAD_DIM, GDN_HEADS
    lt = nchunk * C
    tail = 8

    @pl.when(pl.program_id(1) == 0)
    def _():
        s_ref[...] = s0_ref[0]
        xbuf_ref[0:tail, :] = jnp.zeros((tail, CONV_CH), F32)
        xbuf_ref[tail - (CONV_WIDTH - 1):tail, :] = cb_ref[0]

    xbuf_ref[tail:tail + lt, :] = qkv_ref[0]
    conv = None
    for j in range(CONV_WIDTH):
        off = tail - (CONV_WIDTH - 1) + j
        term = xbuf_ref[off:off + lt, :] * cw_ref[j:j + 1, :]
        conv = term if conv is None else conv + term
    act = _silu(conv)
    cbo_ref[0] = xbuf_ref[lt + tail - (CONV_WIDTH - 1):lt + tail, :]
    xbuf_ref[0:tail, :] = xbuf_ref[lt:lt + tail, :]

    ba = ba_ref[0]
    beta = _sigmoid(ba)
    xa = ba + dtb_ref[...]
    softplus = jnp.maximum(xa, 0.0) + jnp.log1p(jnp.exp(-jnp.abs(xa)))
    g = -jnp.exp(alog_ref[...]) * softplus

    ri = lax.broadcasted_iota(jnp.int32, (C, C), 0)
    ci = lax.broadcasted_iota(jnp.int32, (C, C), 1)
    causal = ri >= ci
    strict = ri > ci
    eye = ri == ci
    tril = causal.astype(F32)
    eye_f = eye.astype(F32)
    ng = ng_ref[...]

    for c in range(nchunk):
        r0 = c * C
        decay = _mm_hi(tril, g[r0:r0 + C, :])
        for h in range(H):
            dcol = decay[:, H + h:H + h + 1]
            dmat = jnp.broadcast_to(dcol, (C, C))
            drow = jnp.sum(jnp.where(eye, dmat, 0.0), axis=0, keepdims=True)
            dmask = jnp.where(causal, jnp.exp(dmat - drow), 0.0)
            bcol = beta[r0:r0 + C, h:h + 1]
            q = act[r0:r0 + C, h * Dh:(h + 1) * Dh]
            k = act[r0:r0 + C, GDN_WIDTH + h * Dh:GDN_WIDTH + (h + 1) * Dh]
            v = act[r0:r0 + C, 2 * GDN_WIDTH + h * Dh:2 * GDN_WIDTH + (h + 1) * Dh]
            q = q * lax.rsqrt(jnp.sum(q * q, -1, keepdims=True) + RMS_EPS) * (Dh ** -0.5)
            k = k * lax.rsqrt(jnp.sum(k * k, -1, keepdims=True) + RMS_EPS)
            kb = k * bcol
            vb = v * bcol
            a = jnp.where(strict, _mm_nt(kb, k) * dmask, 0.0)
            nm = -a
            tinv = eye_f + nm
            for _ in range(5):
                nm = _mm_hi(nm, nm)
                tinv = tinv + _mm_hi(tinv, nm)
            edec = jnp.exp(dcol)
            u = _mm(tinv, vb)
            w = _mm(tinv, kb * edec)
            qk = _mm_nt(q, k) * dmask
            dlast = decay[C - 1:C, H + h:H + h + 1]
            qd = q * edec
            kd = k * jnp.exp(dlast - dcol)
            s = s_ref[h]
            v_new = u - _mm(w, s)
            o = _mm(qd, s) + _mm(qk, v_new)
            s_ref[h] = s * jnp.exp(dlast) + _mm_tn(kd, v_new)
            o = o * lax.rsqrt(jnp.mean(o * o, -1, keepdims=True) + RMS_EPS) * ng
            y_ref[0, r0:r0 + C, h * Dh:(h + 1) * Dh] = o * _silu(z_ref[0, r0:r0 + C, h * Dh:(h + 1) * Dh])

    sT_ref[0] = s_ref[...]


def _gdn(qkv, z, ba, s0, conv_buf, conv_w, a_log, dt_bias, norm_g, nchunk):
    B, L, _ = qkv.shape
    H, Dh = GDN_HEADS, GDN_HEAD_DIM
    lt = nchunk * GDN_CHUNK
    place = lambda t: jnp.zeros((1, LANES), F32).at[0, H:2 * H].set(t)
    xs = lambda w: pl.BlockSpec((1, lt, w), lambda b, i: (b, i, 0))
    full = lambda *shape: pl.BlockSpec(shape, lambda b, i: (0,) * len(shape))
    return pl.pallas_call(
        functools.partial(_gdn_kernel, nchunk=nchunk),
        grid=(B, L // lt),
        in_specs=[xs(CONV_CH), xs(GDN_WIDTH), xs(LANES),
                  pl.BlockSpec((1, H, Dh, Dh), lambda b, i: (b, 0, 0, 0)),
                  pl.BlockSpec((1, CONV_WIDTH - 1, CONV_CH), lambda b, i: (b, 0, 0)),
                  full(CONV_WIDTH, CONV_CH), full(1, LANES), full(1, LANES), full(1, Dh)],
        out_specs=[xs(GDN_WIDTH),
                   pl.BlockSpec((1, H, Dh, Dh), lambda b, i: (b, 0, 0, 0)),
                   pl.BlockSpec((1, CONV_WIDTH - 1, CONV_CH), lambda b, i: (b, 0, 0))],
        out_shape=[jax.ShapeDtypeStruct((B, L, GDN_WIDTH), F32),
                   jax.ShapeDtypeStruct((B, H, Dh, Dh), F32),
                   jax.ShapeDtypeStruct((B, CONV_WIDTH - 1, CONV_CH), F32)],
        scratch_shapes=[pltpu.VMEM((lt + 8, CONV_CH), F32),
                        pltpu.VMEM((H, Dh, Dh), F32)],
        compiler_params=_cparams("arbitrary", "arbitrary"),
        name="gdn",
    )(qkv, z, ba, s0, conv_buf, conv_w, place(a_log), place(dt_bias), norm_g.reshape(1, Dh))


FF_BLK = 1408


def _gelu_tanh(x):
    return 0.5 * x * (1.0 + jnp.tanh(0.7978845608028654 * (x + 0.044715 * x * x * x)))


def _outffn_kernel(x_ref, ya_ref, yb_ref, g1_ref, sh2_ref, sc2_ref, g2_ref,
                   wglu_ref, wout_ref, l1g_ref, l1b_ref, wup_ref, wdn_ref, l2g_ref, l2b_ref,
                   o_ref):
    bt, lt, d = x_ref.shape
    m = bt * lt
    flat = lambda t: t.reshape(m, t.shape[-1])
    unflat = lambda t: t.reshape(bt, lt, t.shape[-1])
    zz = _mm(_gelu_tanh(flat(ya_ref[...])), wglu_ref[...])
    y_a = zz[:, :S5_WIDTH] * _sigmoid(zz[:, S5_WIDTH:])
    mix = _mm(y_a, wout_ref[:S5_WIDTH, :]) + _mm(flat(yb_ref[...]), wout_ref[S5_WIDTH:, :])
    x1 = _ln(ALPHA * x_ref[...] + (1.0 + g1_ref[:, 0]) * unflat(mix)) * l1g_ref[...] + l1b_ref[...]
    h = flat(_ln(x1) * (1.0 + sc2_ref[:, 0]) + sh2_ref[:, 0]).astype(BF16)
    acc = None
    for j in range(D_FF // FF_BLK):
        gate = jnp.dot(h, wup_ref[:, j * FF_BLK:(j + 1) * FF_BLK], preferred_element_type=F32)
        up = jnp.dot(h, wup_ref[:, D_FF + j * FF_BLK:D_FF + (j + 1) * FF_BLK], preferred_element_type=F32)
        part = _mm(_silu(gate) * up, wdn_ref[j * FF_BLK:(j + 1) * FF_BLK, :])
        acc = part if acc is None else acc + part
    o_ref[...] = _ln(ALPHA * x1 + (1.0 + g2_ref[:, 0]) * unflat(acc)) * l2g_ref[...] + l2b_ref[...]


def _outffn(x, ya, yb, mod, w_glu, w_out, ln1_g, ln1_b, w_up, w_dn, ln2_g, ln2_b, bt, lt):
    B, L, _ = x.shape
    xs = lambda w: pl.BlockSpec((bt, lt, w), lambda b, i: (b, i, 0))
    ms = lambda idx: pl.BlockSpec((bt, 1, 1, D_MODEL), lambda b, i: (b, idx, 0, 0))
    const = lambda a: pl.BlockSpec(a.shape, lambda b, i: (0,) * a.ndim, pipeline_mode=pl.Buffered(1))
    vec = lambda t: t.reshape(1, D_MODEL)
    weights = (w_glu, w_out, vec(ln1_g), vec(ln1_b), w_up, w_dn, vec(ln2_g), vec(ln2_b))
    return pl.pallas_call(
        _outffn_kernel,
        grid=(B // bt, L // lt),
        in_specs=[xs(D_MODEL), xs(S5_WIDTH), xs(GDN_WIDTH), ms(2), ms(3), ms(4), ms(5)]
                 + [const(a) for a in weights],
        out_specs=xs(D_MODEL),
        out_shape=jax.ShapeDtypeStruct((B, L, D_MODEL), F32),
        compiler_params=_cparams("arbitrary", "arbitrary"),
        name="outffn",
    )(x, ya, yb, mod, mod, mod, mod, *weights)


def _layer(x, mod, h_re, h_im, s_gdn, conv_buf, wts, bt, lt, gdn_nchunk):
    (w_in_p, s5_mats, s5_d, w_glu, conv_w, a_log, dt_bias, norm_g, w_out,
     ln1_g, ln1_b, w_up, w_dn, ln2_g, ln2_b) = wts
    u, qkv, z, ba = _inproj(x, mod, w_in_p, bt, lt)
    w, pmat, qmat, a_t = s5_mats
    ys5, hT = _s5(u, _s5_state_in(h_re, h_im), w, pmat, qmat, a_t, s5_d, bt)
    yb, sT, cbo = _gdn(qkv, z, ba, s_gdn, conv_buf, conv_w, a_log, dt_bias, norm_g, gdn_nchunk)
    y = _outffn(x, ys5, yb, mod, w_glu, w_out, ln1_g, ln1_b, w_up, w_dn, ln2_g, ln2_b, bt, lt)
    o_re, o_im = _s5_state_out(hT)
    return y, o_re, o_im, sT, cbo


def kernel(x_prompt, x_sample, state_s5_re, state_s5_im, state_gdn, cache_gdn_conv, c_prompt, c_sample, w_ada, b_ada, w_in, s5_a_re, s5_a_im, s5_log_dt, s5_b_re, s5_b_im, s5_c_re, s5_c_im, s5_d, w_s5_glu, gdn_conv_w, gdn_a_log, gdn_dt_bias, gdn_norm_g, w_out, ln1_g, ln1_b, w_ffn_up, w_ffn_down, ln2_g, ln2_b):
    bp, bs = x_prompt.shape[0], x_sample.shape[0]
    assert w_ada.shape[0] == DEPTH == 1
    l = 0
    mod = _ada(jnp.concatenate([c_prompt, c_sample], 0), w_ada[l], b_ada[l])
    mod = mod.reshape(bp + bs, 6, 1, D_MODEL)
    w_in_p = jnp.pad(w_in[l], ((0, 0), (0, IN_PAD - w_in.shape[-1]))).astype(BF16)
    s5_mats = _s5_prep(s5_a_re[l], s5_a_im[l], s5_log_dt[l], s5_b_re[l], s5_b_im[l],
                       s5_c_re[l], s5_c_im[l])
    wts = (w_in_p, s5_mats, s5_d[l], w_s5_glu[l].astype(BF16), gdn_conv_w[l], gdn_a_log[l],
           gdn_dt_bias[l], gdn_norm_g[l], w_out[l].astype(BF16), ln1_g[l], ln1_b[l],
           w_ffn_up[l].astype(BF16), w_ffn_down[l].astype(BF16), ln2_g[l], ln2_b[l])
    zeros = lambda *s: jnp.zeros(s, F32)
    yp, p_re, p_im, p_gdn, p_conv = _layer(
        x_prompt, mod[:bp],
        zeros(bp, S5_GROUPS, S5_STATE), zeros(bp, S5_GROUPS, S5_STATE),
        zeros(bp, GDN_HEADS, GDN_HEAD_DIM, GDN_HEAD_DIM), zeros(bp, CONV_WIDTH - 1, CONV_CH),
        wts, bt=1, lt=512, gdn_nchunk=2)
    ys, s_re, s_im, s_gdn, s_conv = _layer(
        x_sample, mod[bp:], state_s5_re[l], state_s5_im[l], state_gdn[l], cache_gdn_conv[l],
        wts, bt=bs, lt=x_sample.shape[1], gdn_nchunk=1)
    st = lambda t: t[None]
    return (yp, ys, st(p_re), st(p_im), st(p_gdn), st(p_conv),
            st(s_re), st(s_im), st(s_gdn), st(s_conv))
```

```python
import functools

import jax
import jax.numpy as jnp
from jax import lax
from jax.experimental import pallas as pl
from jax.experimental.pallas import tpu as pltpu

F32 = jnp.float32
BF16 = jnp.bfloat16

D_MODEL = 1024
S5_WIDTH = 512
S5_GROUP = 16
S5_GROUPS = 32
S5_STATE = 64
GDN_WIDTH = 512
GDN_HEAD_DIM = 128
GDN_HEADS = 4
CONV_WIDTH = 4
CONV_CH = 3 * GDN_WIDTH
D_FF = 2816
GDN_CHUNK = 64
LN_EPS = 1e-5
RMS_EPS = 1e-6
DEPTH = 1
ALPHA = (2 * DEPTH) ** 0.25

LANES = 128
S5_T = 8
S5_BLK_GROUPS = LANES // S5_GROUP
S5_NBLK = S5_GROUPS // S5_BLK_GROUPS
S5_BLK_STATE = S5_BLK_GROUPS * S5_STATE
IN_PAD = S5_WIDTH + 4 * GDN_WIDTH + LANES
VMEM_LIMIT = 56 * 1024 * 1024

HI = lax.Precision.HIGHEST


def _cparams(*sem):
    return pltpu.CompilerParams(dimension_semantics=sem, vmem_limit_bytes=VMEM_LIMIT)


def _ln(x):
    mu = jnp.mean(x, -1, keepdims=True)
    xc = x - mu
    var = jnp.mean(xc * xc, -1, keepdims=True)
    return xc * lax.rsqrt(var + LN_EPS)


def _sigmoid(x):
    return 1.0 / (1.0 + jnp.exp(-x))


def _silu(x):
    return x * _sigmoid(x)


def _ada_kernel(c_ref, w_ref, b_ref, o_ref):
    c = c_ref[...]
    o_ref[...] = jnp.dot(_silu(c), w_ref[...], preferred_element_type=F32,
                         precision=HI) + b_ref[...]


def _ada(c, w_ada, b_ada):
    rows = c.shape[0]
    ncol = w_ada.shape[1]
    blk = D_MODEL
    return pl.pallas_call(
        _ada_kernel,
        grid=(ncol // blk,),
        in_specs=[pl.BlockSpec((rows, D_MODEL), lambda j: (0, 0)),
                  pl.BlockSpec((D_MODEL, blk), lambda j: (0, j)),
                  pl.BlockSpec((1, blk), lambda j: (0, j))],
        out_specs=pl.BlockSpec((rows, blk), lambda j: (0, j)),
        out_shape=jax.ShapeDtypeStruct((rows, ncol), F32),
        compiler_params=_cparams("arbitrary"),
        name="ada",
    )(c, w_ada, b_ada.reshape(1, ncol))


def _s5prep_kernel(are_ref, aim_ref, ldt_ref, bre_ref, bim_ref, btre_ref, btim_ref,
                   cre_ref, cim_ref, k_ref, p_ref, q_ref, at_ref):
    a_re = are_ref[...]
    a_im = aim_ref[...]
    dt = jnp.exp(ldt_ref[...])
    den = a_re * a_re + a_im * a_im

    def lam_pow(tau):
        mag = jnp.exp(a_re * dt * tau)
        return mag * jnp.cos(a_im * dt * tau), mag * jnp.sin(a_im * dt * tau)

    l_re, l_im = lam_pow(1.0)
    n_re, n_im = l_re - 1.0, l_im
    f_re = (n_re * a_re + n_im * a_im) / den
    f_im = (n_im * a_re - n_re * a_im) / den
    c_re = cre_ref[...]
    c_im = cim_ref[...]
    bt_re = btre_ref[...]
    bt_im = btim_ref[...]
    b_re = bre_ref[...]
    b_im = bim_ref[...]
    bdot = functools.partial(jnp.einsum, 'gjp,gpi->gji', precision=HI,
                             preferred_element_type=F32)
    for tau in range(S5_T + 1):
        p_re, p_im = (jnp.ones_like(a_re), jnp.zeros_like(a_re)) if tau == 0 else lam_pow(float(tau))
        if tau == S5_T:
            at_ref[0] = p_re
            at_ref[1] = p_im
        if tau < S5_T:
            g_re = p_re * f_re - p_im * f_im
            g_im = p_re * f_im + p_im * f_re
            cg_re = c_re * g_re[:, None, :] - c_im * g_im[:, None, :]
            cg_im = c_re * g_im[:, None, :] + c_im * g_re[:, None, :]
            k_ref[tau] = bdot(cg_re, b_re) - bdot(cg_im, b_im)
            s = S5_T - 1 - tau
            p_ref[s, 0] = g_re[:, None, :] * bt_re - g_im[:, None, :] * bt_im
            p_ref[s, 1] = g_re[:, None, :] * bt_im + g_im[:, None, :] * bt_re
        if tau >= 1:
            q_ref[tau - 1, 0] = c_re * p_re[:, None, :] - c_im * p_im[:, None, :]
            q_ref[tau - 1, 1] = -(c_re * p_im[:, None, :] + c_im * p_re[:, None, :])


def _s5_prep(a_re, a_im, log_dt, b_re, b_im, c_re, c_im):
    G, P, J, T = S5_GROUPS, S5_STATE, S5_GROUP, S5_T
    bt_re = jnp.swapaxes(b_re, 1, 2)
    bt_im = jnp.swapaxes(b_im, 1, 2)
    kt, pm, qm, at = pl.pallas_call(
        _s5prep_kernel,
        out_shape=[jax.ShapeDtypeStruct((T, G, J, J), F32),
                   jax.ShapeDtypeStruct((T, 2, G, J, P), F32),
                   jax.ShapeDtypeStruct((T, 2, G, J, P), F32),
                   jax.ShapeDtypeStruct((2, G, P), F32)],
        compiler_params=pltpu.CompilerParams(vmem_limit_bytes=VMEM_LIMIT),
        name="s5prep",
    )(a_re, a_im, log_dt.reshape(G, 1), b_re, b_im, bt_re, bt_im, c_re, c_im)

    NB, BG = S5_NBLK, S5_BLK_GROUPS
    eye_g = jnp.eye(BG, dtype=F32)
    tau = jnp.arange(T)[None, :] - jnp.arange(T)[:, None]
    kst = jnp.where((tau >= 0)[:, :, None, None, None], kt[jnp.clip(tau, 0, T - 1)], 0.0)
    kst = kst.reshape(T, T, NB, BG, J, J)
    w = jnp.einsum('stkgji,gh->ksgithj', kst, eye_g).reshape(NB, T * LANES, T * LANES)
    pmr = pm.reshape(T, 2, NB, BG, J, P)
    pmat = jnp.einsum('sckgip,gh->ksgichp', pmr, eye_g).reshape(NB, T * LANES, 2 * S5_BLK_STATE)
    qmr = qm.reshape(T, 2, NB, BG, J, P)
    qmat = jnp.einsum('tckgjp,gh->kcgpthj', qmr, eye_g).reshape(NB, 2 * S5_BLK_STATE, T * LANES)
    a_t = at.reshape(2, NB, 1, S5_BLK_STATE).transpose(1, 2, 0, 3).reshape(NB, 1, 2 * S5_BLK_STATE)
    return w.astype(BF16), pmat.astype(BF16), qmat.astype(BF16), a_t


def _s5_kernel(u_ref, h0_ref, w_ref, p_ref, q_ref, at_ref, d_ref, y_ref, hT_ref,
               lhs_ref, x_ref, hp_ref, *, bt, n):
    T, half = S5_T, S5_BLK_STATE
    for b in range(bt):
        for t in range(T):
            lhs_ref[b * n:(b + 1) * n, t * LANES:(t + 1) * LANES] = (
                u_ref[b, pl.ds(t, n, stride=T), :].astype(BF16))
    lhs = lhs_ref[...]
    x_ref[...] = jnp.dot(lhs, p_ref[0], preferred_element_type=F32)
    a_re = at_ref[0, :, :half]
    a_im = at_ref[0, :, half:]
    for b in range(bt):
        def step(i, h):
            h_re, h_im = h
            row = b * n + i
            hp_ref[pl.ds(row, 1), :half] = h_re
            hp_ref[pl.ds(row, 1), half:] = h_im
            x_re = x_ref[pl.ds(row, 1), :half]
            x_im = x_ref[pl.ds(row, 1), half:]
            return (a_re * h_re - a_im * h_im + x_re, a_re * h_im + a_im * h_re + x_im)
        h_re, h_im = lax.fori_loop(0, n, step, (h0_ref[b, 0, :, :half], h0_ref[b, 0, :, half:]))
        hT_ref[b, 0, :, :half] = h_re
        hT_ref[b, 0, :, half:] = h_im
    y = (jnp.dot(lhs, w_ref[0], preferred_element_type=F32)
         + jnp.dot(hp_ref[...].astype(BF16), q_ref[0], preferred_element_type=F32))
    d = d_ref[...]
    for b in range(bt):
        for t in range(T):
            y_ref[b, pl.ds(t, n, stride=T), :] = (
                y[b * n:(b + 1) * n, t * LANES:(t + 1) * LANES]
                + d * u_ref[b, pl.ds(t, n, stride=T), :])


def _s5(u, h0, w, pmat, qmat, a_t, d_skip, bt):
    B, L, _ = u.shape
    n = L // S5_T
    tl, st = S5_T * LANES, 2 * S5_BLK_STATE
    return pl.pallas_call(
        functools.partial(_s5_kernel, bt=bt, n=n),
        grid=(S5_NBLK, B // bt),
        in_specs=[pl.BlockSpec((bt, L, LANES), lambda k, b: (b, 0, k)),
                  pl.BlockSpec((bt, 1, 1, st), lambda k, b: (b, k, 0, 0)),
                  pl.BlockSpec((1, tl, tl), lambda k, b: (k, 0, 0)),
                  pl.BlockSpec((1, tl, st), lambda k, b: (k, 0, 0)),
                  pl.BlockSpec((1, st, tl), lambda k, b: (k, 0, 0)),
                  pl.BlockSpec((1, 1, st), lambda k, b: (k, 0, 0)),
                  pl.BlockSpec((1, LANES), lambda k, b: (0, k))],
        out_specs=[pl.BlockSpec((bt, L, LANES), lambda k, b: (b, 0, k)),
                   pl.BlockSpec((bt, 1, 1, st), lambda k, b: (b, k, 0, 0))],
        out_shape=[jax.ShapeDtypeStruct((B, L, S5_WIDTH), F32),
                   jax.ShapeDtypeStruct((B, S5_NBLK, 1, st), F32)],
        scratch_shapes=[pltpu.VMEM((bt * n, tl), BF16),
                        pltpu.VMEM((bt * n, st), F32),
                        pltpu.VMEM((bt * n, st), F32)],
        compiler_params=_cparams("arbitrary", "arbitrary"),
        name="s5",
    )(u, h0, w, pmat, qmat, a_t, d_skip.reshape(1, S5_WIDTH))


def _s5_state_in(h_re, h_im):
    B = h_re.shape[0]
    r = h_re.reshape(B, S5_NBLK, 1, S5_BLK_STATE)
    i = h_im.reshape(B, S5_NBLK, 1, S5_BLK_STATE)
    return jnp.concatenate([r, i], -1)


def _s5_state_out(h):
    B = h.shape[0]
    h = h.reshape(B, S5_NBLK, 2, S5_BLK_GROUPS, S5_STATE)
    return (h[:, :, 0].reshape(B, S5_GROUPS, S5_STATE), h[:, :, 1].reshape(B, S5_GROUPS, S5_STATE))


def _inproj_kernel(x_ref, sh_ref, sc_ref, w_ref, u_ref, qkv_ref, z_ref, ba_ref):
    bt, lt, d = x_ref.shape
    h = _ln(x_ref[...]) * (1.0 + sc_ref[:, 0]) + sh_ref[:, 0]
    out = jnp.dot(h.reshape(bt * lt, d).astype(BF16), w_ref[...], preferred_element_type=F32)
    o_q, o_z, o_b = S5_WIDTH, S5_WIDTH + CONV_CH, S5_WIDTH + CONV_CH + GDN_WIDTH
    u_ref[...] = out[:, :o_q].reshape(bt, lt, S5_WIDTH)
    qkv_ref[...] = out[:, o_q:o_z].reshape(bt, lt, CONV_CH)
    z_ref[...] = out[:, o_z:o_b].reshape(bt, lt, GDN_WIDTH)
    ba_ref[...] = out[:, o_b:].reshape(bt, lt, LANES)


def _inproj(x, mod, w_in_p, bt, lt):
    B, L, _ = x.shape
    xs = lambda w: pl.BlockSpec((bt, lt, w), lambda b, i: (b, i, 0))
    ms = lambda idx: pl.BlockSpec((bt, 1, 1, D_MODEL), lambda b, i: (b, idx, 0, 0))
    return pl.pallas_call(
        _inproj_kernel,
        grid=(B // bt, L // lt),
        in_specs=[xs(D_MODEL), ms(0), ms(1),
                  pl.BlockSpec((D_MODEL, IN_PAD), lambda b, i: (0, 0))],
        out_specs=[xs(S5_WIDTH), xs(CONV_CH), xs(GDN_WIDTH), xs(LANES)],
        out_shape=[jax.ShapeDtypeStruct((B, L, w), F32) for w in (S5_WIDTH, CONV_CH, GDN_WIDTH, LANES)],
        compiler_params=_cparams("arbitrary", "arbitrary"),
        name="inproj",
    )(x, mod, mod, w_in_p)


def _mm(a, b):
    return jnp.dot(a.astype(BF16), b.astype(BF16), preferred_element_type=F32)


def _mm_nt(a, b):
    return lax.dot_general(a.astype(BF16), b.astype(BF16), (((1,), (1,)), ((), ())),
                           preferred_element_type=F32)


def _mm_tn(a, b):
    return lax.dot_general(a.astype(BF16), b.astype(BF16), (((0,), (0,)), ((), ())),
                           preferred_element_type=F32)


def _mm_hi(a, b):
    return jnp.dot(a, b, preferred_element_type=F32, precision=HI)


GDN_INV_BASE = 16
GDN_INV_BASE_PASSES = 3
GDN_INV_UP_PASSES = 1


def _mm_split(a, b, passes):
    if passes == 1:
        return _mm(a, b)
    a_hi = a.astype(BF16)
    b_hi = b.astype(BF16)
    a_lo = (a - a_hi.astype(F32)).astype(BF16)
    b_lo = (b - b_hi.astype(F32)).astype(BF16)
    dot = functools.partial(jnp.dot, preferred_element_type=F32)
    return dot(a_hi, b_hi) + (dot(a_hi, b_lo) + dot(a_lo, b_hi))


def _unit_lower_inverse(a, ri, ci, size):
    eye_f = (ri == ci).astype(F32)
    blk = lambda s: (ri // s) == (ci // s)
    base = min(GDN_INV_BASE, size)
    n = -jnp.where(blk(base), a, 0.0)
    d = eye_f + n
    span = 2
    while span < base:
        n = _mm_split(n, n, GDN_INV_BASE_PASSES)
        d = d + _mm_split(d, n, GDN_INV_BASE_PASSES)
        span *= 2
    prev = base
    while prev < size:
        cur = min(prev * 4, size)
        off = jnp.where(blk(cur) & jnp.logical_not(blk(prev)), a, 0.0)
        m = _mm_split(d, off, GDN_INV_UP_PASSES)
        m2 = _mm_split(m, m, GDN_INV_UP_PASSES)
        r = eye_f - m
        r = r + _mm_split(r, m2, GDN_INV_UP_PASSES)
        d = _mm_split(r, d, GDN_INV_UP_PASSES)
        prev = cur
    return d


def _gdn_kernel(qkv_ref, z_ref, ba_ref, s0_ref, cb_ref, cw_ref, alog_ref, dtb_ref, ng_ref,
                y_ref, sT_ref, cbo_ref, xbuf_ref, s_ref, *, chunk, nchunk):
    C, Dh, H = chunk, GDN_HEAD_DIM, GDN_HEADS
    lt = nchunk * C
    tail = 8

    @pl.when(pl.program_id(1) == 0)
    def _():
        s_ref[...] = s0_ref[0]
        xbuf_ref[0:tail, :] = jnp.zeros((tail, CONV_CH), F32)
        xbuf_ref[tail - (CONV_WIDTH - 1):tail, :] = cb_ref[0]

    xbuf_ref[tail:tail + lt, :] = qkv_ref[0]
    conv = None
    for j in range(CONV_WIDTH):
        off = tail - (CONV_WIDTH - 1) + j
        term = xbuf_ref[off:off + lt, :] * cw_ref[j:j + 1, :]
        conv = term if conv is None else conv + term
    act = _silu(conv)
    cbo_ref[0] = xbuf_ref[lt + tail - (CONV_WIDTH - 1):lt + tail, :]
    xbuf_ref[0:tail, :] = xbuf_ref[lt:lt + tail, :]

    ba = ba_ref[0]
    beta = _sigmoid(ba)
    xa = ba + dtb_ref[...]
    softplus = jnp.maximum(xa, 0.0) + jnp.log1p(jnp.exp(-jnp.abs(xa)))
    g = -jnp.exp(alog_ref[...]) * softplus

    ri = lax.broadcasted_iota(jnp.int32, (C, C), 0)
    ci = lax.broadcasted_iota(jnp.int32, (C, C), 1)
    causal = ri >= ci
    strict = ri > ci
    eye = ri == ci
    tril = causal.astype(F32)
    ng = ng_ref[...]

    for c in range(nchunk):
        r0 = c * C
        decay = _mm_hi(tril, g[r0:r0 + C, :])
        for h in range(H):
            dcol = decay[:, H + h:H + h + 1]
            dmat = jnp.broadcast_to(dcol, (C, C))
            drow = jnp.sum(jnp.where(eye, dmat, 0.0), axis=0, keepdims=True)
            dmask = jnp.where(causal, jnp.exp(dmat - drow), 0.0)
            bcol = beta[r0:r0 + C, h:h + 1]
            q = act[r0:r0 + C, h * Dh:(h + 1) * Dh]
            k = act[r0:r0 + C, GDN_WIDTH + h * Dh:GDN_WIDTH + (h + 1) * Dh]
            v = act[r0:r0 + C, 2 * GDN_WIDTH + h * Dh:2 * GDN_WIDTH + (h + 1) * Dh]
            q = q * lax.rsqrt(jnp.sum(q * q, -1, keepdims=True) + RMS_EPS) * (Dh ** -0.5)
            k = k * lax.rsqrt(jnp.sum(k * k, -1, keepdims=True) + RMS_EPS)
            kb = k * bcol
            vb = v * bcol
            a = jnp.where(strict, _mm_nt(kb, k) * dmask, 0.0)
            tinv = _unit_lower_inverse(a, ri, ci, C)
            edec = jnp.exp(dcol)
            uw = _mm(tinv, jnp.concatenate([vb, kb * edec], axis=1))
            u, w = uw[:, :Dh], uw[:, Dh:]
            qk = _mm_nt(q, k) * dmask
            dlast = decay[C - 1:C, H + h:H + h + 1]
            kd = k * jnp.exp(dlast - dcol)
            s = s_ref[h]
            ws = _mm(jnp.concatenate([w, q * edec], axis=0), s)
            v_new = u - ws[:C]
            o = ws[C:] + _mm(qk, v_new)
            s_ref[h] = s * jnp.exp(dlast) + _mm_tn(kd, v_new)
            o = o * lax.rsqrt(jnp.mean(o * o, -1, keepdims=True) + RMS_EPS) * ng
            y_ref[0, r0:r0 + C, h * Dh:(h + 1) * Dh] = o * _silu(z_ref[0, r0:r0 + C, h * Dh:(h + 1) * Dh])

    sT_ref[0] = s_ref[...]


def _gdn(qkv, z, ba, s0, conv_buf, conv_w, a_log, dt_bias, norm_g, chunk, nchunk=1):
    B, L, _ = qkv.shape
    H, Dh = GDN_HEADS, GDN_HEAD_DIM
    lt = nchunk * chunk
    place = lambda t: jnp.zeros((1, LANES), F32).at[0, H:2 * H].set(t)
    xs = lambda w: pl.BlockSpec((1, lt, w), lambda b, i: (b, i, 0))
    full = lambda *shape: pl.BlockSpec(shape, lambda b, i: (0,) * len(shape))
    return pl.pallas_call(
        functools.partial(_gdn_kernel, chunk=chunk, nchunk=nchunk),
        grid=(B, L // lt),
        in_specs=[xs(CONV_CH), xs(GDN_WIDTH), xs(LANES),
                  pl.BlockSpec((1, H, Dh, Dh), lambda b, i: (b, 0, 0, 0)),
                  pl.BlockSpec((1, CONV_WIDTH - 1, CONV_CH), lambda b, i: (b, 0, 0)),
                  full(CONV_WIDTH, CONV_CH), full(1, LANES), full(1, LANES), full(1, Dh)],
        out_specs=[xs(GDN_WIDTH),
                   pl.BlockSpec((1, H, Dh, Dh), lambda b, i: (b, 0, 0, 0)),
                   pl.BlockSpec((1, CONV_WIDTH - 1, CONV_CH), lambda b, i: (b, 0, 0))],
        out_shape=[jax.ShapeDtypeStruct((B, L, GDN_WIDTH), F32),
                   jax.ShapeDtypeStruct((B, H, Dh, Dh), F32),
                   jax.ShapeDtypeStruct((B, CONV_WIDTH - 1, CONV_CH), F32)],
        scratch_shapes=[pltpu.VMEM((lt + 8, CONV_CH), F32),
                        pltpu.VMEM((H, Dh, Dh), F32)],
        compiler_params=_cparams("arbitrary", "arbitrary"),
        name="gdn",
    )(qkv, z, ba, s0, conv_buf, conv_w, place(a_log), place(dt_bias), norm_g.reshape(1, Dh))


FF_BLK = 1408


def _gelu_tanh(x):
    return 0.5 * x * (1.0 + jnp.tanh(0.7978845608028654 * (x + 0.044715 * x * x * x)))


def _outffn_kernel(x_ref, ya_ref, yb_ref, g1_ref, sh2_ref, sc2_ref, g2_ref,
                   wglu_ref, wout_ref, l1g_ref, l1b_ref, wup_ref, wdn_ref, l2g_ref, l2b_ref,
                   o_ref):
    bt, lt, d = x_ref.shape
    m = bt * lt
    flat = lambda t: t.reshape(m, t.shape[-1])
    unflat = lambda t: t.reshape(bt, lt, t.shape[-1])
    zz = _mm(_gelu_tanh(flat(ya_ref[...])), wglu_ref[...])
    y_a = zz[:, :S5_WIDTH] * _sigmoid(zz[:, S5_WIDTH:])
    mix = _mm(y_a, wout_ref[:S5_WIDTH, :]) + _mm(flat(yb_ref[...]), wout_ref[S5_WIDTH:, :])
    x1 = _ln(ALPHA * x_ref[...] + (1.0 + g1_ref[:, 0]) * unflat(mix)) * l1g_ref[...] + l1b_ref[...]
    h = flat(_ln(x1) * (1.0 + sc2_ref[:, 0]) + sh2_ref[:, 0]).astype(BF16)
    acc = None
    for j in range(D_FF // FF_BLK):
        gate = jnp.dot(h, wup_ref[:, j * FF_BLK:(j + 1) * FF_BLK], preferred_element_type=F32)
        up = jnp.dot(h, wup_ref[:, D_FF + j * FF_BLK:D_FF + (j + 1) * FF_BLK], preferred_element_type=F32)
        part = _mm(_silu(gate) * up, wdn_ref[j * FF_BLK:(j + 1) * FF_BLK, :])
        acc = part if acc is None else acc + part
    o_ref[...] = _ln(ALPHA * x1 + (1.0 + g2_ref[:, 0]) * unflat(acc)) * l2g_ref[...] + l2b_ref[...]


def _outffn(x, ya, yb, mod, w_glu, w_out, ln1_g, ln1_b, w_up, w_dn, ln2_g, ln2_b, bt, lt):
    B, L, _ = x.shape
    xs = lambda w: pl.BlockSpec((bt, lt, w), lambda b, i: (b, i, 0))
    ms = lambda idx: pl.BlockSpec((bt, 1, 1, D_MODEL), lambda b, i: (b, idx, 0, 0))
    const = lambda a: pl.BlockSpec(a.shape, lambda b, i: (0,) * a.ndim, pipeline_mode=pl.Buffered(1))
    vec = lambda t: t.reshape(1, D_MODEL)
    weights = (w_glu, w_out, vec(ln1_g), vec(ln1_b), w_up, w_dn, vec(ln2_g), vec(ln2_b))
    return pl.pallas_call(
        _outffn_kernel,
        grid=(B // bt, L // lt),
        in_specs=[xs(D_MODEL), xs(S5_WIDTH), xs(GDN_WIDTH), ms(2), ms(3), ms(4), ms(5)]
                 + [const(a) for a in weights],
        out_specs=xs(D_MODEL),
        out_shape=jax.ShapeDtypeStruct((B, L, D_MODEL), F32),
        compiler_params=_cparams("arbitrary", "arbitrary"),
        name="outffn",
    )(x, ya, yb, mod, mod, mod, mod, *weights)


def _layer(x, mod, h_re, h_im, s_gdn, conv_buf, wts, bt, lt, gdn_chunk):
    (w_in_p, s5_mats, s5_d, w_glu, conv_w, a_log, dt_bias, norm_g, w_out,
     ln1_g, ln1_b, w_up, w_dn, ln2_g, ln2_b) = wts
    u, qkv, z, ba = _inproj(x, mod, w_in_p, bt, lt)
    w, pmat, qmat, a_t = s5_mats
    ys5, hT = _s5(u, _s5_state_in(h_re, h_im), w, pmat, qmat, a_t, s5_d, bt)
    yb, sT, cbo = _gdn(qkv, z, ba, s_gdn, conv_buf, conv_w, a_log, dt_bias, norm_g, gdn_chunk)
    y = _outffn(x, ys5, yb, mod, w_glu, w_out, ln1_g, ln1_b, w_up, w_dn, ln2_g, ln2_b, bt, lt)
    o_re, o_im = _s5_state_out(hT)
    return y, o_re, o_im, sT, cbo


def kernel(x_prompt, x_sample, state_s5_re, state_s5_im, state_gdn, cache_gdn_conv, c_prompt, c_sample, w_ada, b_ada, w_in, s5_a_re, s5_a_im, s5_log_dt, s5_b_re, s5_b_im, s5_c_re, s5_c_im, s5_d, w_s5_glu, gdn_conv_w, gdn_a_log, gdn_dt_bias, gdn_norm_g, w_out, ln1_g, ln1_b, w_ffn_up, w_ffn_down, ln2_g, ln2_b):
    bp, bs = x_prompt.shape[0], x_sample.shape[0]
    assert w_ada.shape[0] == DEPTH == 1
    l = 0
    mod = _ada(jnp.concatenate([c_prompt, c_sample], 0), w_ada[l], b_ada[l])
    mod = mod.reshape(bp + bs, 6, 1, D_MODEL)
    w_in_p = jnp.pad(w_in[l], ((0, 0), (0, IN_PAD - w_in.shape[-1]))).astype(BF16)
    s5_mats = _s5_prep(s5_a_re[l], s5_a_im[l], s5_log_dt[l], s5_b_re[l], s5_b_im[l],
                       s5_c_re[l], s5_c_im[l])
    wts = (w_in_p, s5_mats, s5_d[l], w_s5_glu[l].astype(BF16), gdn_conv_w[l], gdn_a_log[l],
           gdn_dt_bias[l], gdn_norm_g[l], w_out[l].astype(BF16), ln1_g[l], ln1_b[l],
           w_ffn_up[l].astype(BF16), w_ffn_down[l].astype(BF16), ln2_g[l], ln2_b[l])
    zeros = lambda *s: jnp.zeros(s, F32)
    yp, p_re, p_im, p_gdn, p_conv = _layer(
        x_prompt, mod[:bp],
        zeros(bp, S5_GROUPS, S5_STATE), zeros(bp, S5_GROUPS, S5_STATE),
        zeros(bp, GDN_HEADS, GDN_HEAD_DIM, GDN_HEAD_DIM), zeros(bp, CONV_WIDTH - 1, CONV_CH),
        wts, bt=1, lt=512, gdn_chunk=256)
    ys, s_re, s_im, s_gdn, s_conv = _layer(
        x_sample, mod[bp:], state_s5_re[l], state_s5_im[l], state_gdn[l], cache_gdn_conv[l],
        wts, bt=bs, lt=x_sample.shape[1], gdn_chunk=x_sample.shape[1])
    st = lambda t: t[None]
    return (yp, ys, st(p_re), st(p_im), st(p_gdn), st(p_conv),
            st(s_re), st(s_im), st(s_gdn), st(s_conv))
```

```python
import functools

import jax
import jax.numpy as jnp
from jax import lax
from jax.experimental import pallas as pl
from jax.experimental.pallas import tpu as pltpu

F32 = jnp.float32
BF16 = jnp.bfloat16

D_MODEL = 1024
S5_WIDTH = 512
S5_GROUP = 16
S5_GROUPS = 32
S5_STATE = 64
GDN_WIDTH = 512
GDN_HEAD_DIM = 128
GDN_HEADS = 4
CONV_WIDTH = 4
CONV_CH = 3 * GDN_WIDTH
D_FF = 2816
GDN_CHUNK = 64
LN_EPS = 1e-5
RMS_EPS = 1e-6
DEPTH = 1
ALPHA = (2 * DEPTH) ** 0.25

LANES = 128
S5_T = 8
S5_BLK_GROUPS = LANES // S5_GROUP
S5_NBLK = S5_GROUPS // S5_BLK_GROUPS
S5_BLK_STATE = S5_BLK_GROUPS * S5_STATE
IN_PAD = S5_WIDTH + 4 * GDN_WIDTH + LANES
VMEM_LIMIT = 56 * 1024 * 1024

HI = lax.Precision.HIGHEST


def _cparams(*sem):
    return pltpu.CompilerParams(dimension_semantics=sem, vmem_limit_bytes=VMEM_LIMIT)


def _ln(x):
    mu = jnp.mean(x, -1, keepdims=True)
    xc = x - mu
    var = jnp.mean(xc * xc, -1, keepdims=True)
    return xc * lax.rsqrt(var + LN_EPS)


def _sigmoid(x):
    return 1.0 / (1.0 + jnp.exp(-x))


def _silu(x):
    return x * _sigmoid(x)


def _ada_kernel(c_ref, w_ref, b_ref, o_ref):
    c = c_ref[...]
    o_ref[...] = jnp.dot(_silu(c), w_ref[...], preferred_element_type=F32,
                         precision=HI) + b_ref[...]


def _ada(c, w_ada, b_ada):
    rows = c.shape[0]
    ncol = w_ada.shape[1]
    blk = D_MODEL
    return pl.pallas_call(
        _ada_kernel,
        grid=(ncol // blk,),
        in_specs=[pl.BlockSpec((rows, D_MODEL), lambda j: (0, 0)),
                  pl.BlockSpec((D_MODEL, blk), lambda j: (0, j)),
                  pl.BlockSpec((1, blk), lambda j: (0, j))],
        out_specs=pl.BlockSpec((rows, blk), lambda j: (0, j)),
        out_shape=jax.ShapeDtypeStruct((rows, ncol), F32),
        compiler_params=_cparams("arbitrary"),
        name="ada",
    )(c, w_ada, b_ada.reshape(1, ncol))


def _s5prep_kernel(are_ref, aim_ref, ldt_ref, bre_ref, bim_ref, btre_ref, btim_ref,
                   cre_ref, cim_ref, k_ref, p_ref, q_ref, at_ref):
    a_re = are_ref[...]
    a_im = aim_ref[...]
    dt = jnp.exp(ldt_ref[...])
    den = a_re * a_re + a_im * a_im

    def lam_pow(tau):
        mag = jnp.exp(a_re * dt * tau)
        return mag * jnp.cos(a_im * dt * tau), mag * jnp.sin(a_im * dt * tau)

    l_re, l_im = lam_pow(1.0)
    n_re, n_im = l_re - 1.0, l_im
    f_re = (n_re * a_re + n_im * a_im) / den
    f_im = (n_im * a_re - n_re * a_im) / den
    c_re = cre_ref[...]
    c_im = cim_ref[...]
    bt_re = btre_ref[...]
    bt_im = btim_ref[...]
    b_re = bre_ref[...]
    b_im = bim_ref[...]
    bdot = functools.partial(jnp.einsum, 'gjp,gpi->gji', precision=HI,
                             preferred_element_type=F32)
    for tau in range(S5_T + 1):
        p_re, p_im = (jnp.ones_like(a_re), jnp.zeros_like(a_re)) if tau == 0 else lam_pow(float(tau))
        if tau == S5_T:
            at_ref[0] = p_re
            at_ref[1] = p_im
        if tau < S5_T:
            g_re = p_re * f_re - p_im * f_im
            g_im = p_re * f_im + p_im * f_re
            cg_re = c_re * g_re[:, None, :] - c_im * g_im[:, None, :]
            cg_im = c_re * g_im[:, None, :] + c_im * g_re[:, None, :]
            k_ref[tau] = bdot(cg_re, b_re) - bdot(cg_im, b_im)
            s = S5_T - 1 - tau
            p_ref[s, 0] = g_re[:, None, :] * bt_re - g_im[:, None, :] * bt_im
            p_ref[s, 1] = g_re[:, None, :] * bt_im + g_im[:, None, :] * bt_re
        if tau >= 1:
            q_ref[tau - 1, 0] = c_re * p_re[:, None, :] - c_im * p_im[:, None, :]
            q_ref[tau - 1, 1] = -(c_re * p_im[:, None, :] + c_im * p_re[:, None, :])


def _s5_prep(a_re, a_im, log_dt, b_re, b_im, c_re, c_im):
    G, P, J, T = S5_GROUPS, S5_STATE, S5_GROUP, S5_T
    bt_re = jnp.swapaxes(b_re, 1, 2)
    bt_im = jnp.swapaxes(b_im, 1, 2)
    kt, pm, qm, at = pl.pallas_call(
        _s5prep_kernel,
        out_shape=[jax.ShapeDtypeStruct((T, G, J, J), F32),
                   jax.ShapeDtypeStruct((T, 2, G, J, P), F32),
                   jax.ShapeDtypeStruct((T, 2, G, J, P), F32),
                   jax.ShapeDtypeStruct((2, G, P), F32)],
        compiler_params=pltpu.CompilerParams(vmem_limit_bytes=VMEM_LIMIT),
        name="s5prep",
    )(a_re, a_im, log_dt.reshape(G, 1), b_re, b_im, bt_re, bt_im, c_re, c_im)

    NB, BG = S5_NBLK, S5_BLK_GROUPS
    eye_g = jnp.eye(BG, dtype=F32)
    tau = jnp.arange(T)[None, :] - jnp.arange(T)[:, None]
    kst = jnp.where((tau >= 0)[:, :, None, None, None], kt[jnp.clip(tau, 0, T - 1)], 0.0)
    kst = kst.reshape(T, T, NB, BG, J, J)
    w = jnp.einsum('stkgji,gh->ksgithj', kst, eye_g).reshape(NB, T * LANES, T * LANES)
    pmr = pm.reshape(T, 2, NB, BG, J, P)
    pmat = jnp.einsum('sckgip,gh->ksgichp', pmr, eye_g).reshape(NB, T * LANES, 2 * S5_BLK_STATE)
    qmr = qm.reshape(T, 2, NB, BG, J, P)
    qmat = jnp.einsum('tckgjp,gh->kcgpthj', qmr, eye_g).reshape(NB, 2 * S5_BLK_STATE, T * LANES)
    a_t = at.reshape(2, NB, 1, S5_BLK_STATE).transpose(1, 2, 0, 3).reshape(NB, 1, 2 * S5_BLK_STATE)
    return w.astype(BF16), pmat.astype(BF16), qmat.astype(BF16), a_t


def _s5_kernel(u_ref, h0_ref, w_ref, p_ref, q_ref, at_ref, d_ref, y_ref, hT_ref,
               lhs_ref, x_ref, hp_ref, *, bt, n):
    T, half = S5_T, S5_BLK_STATE
    for b in range(bt):
        for t in range(T):
            lhs_ref[b * n:(b + 1) * n, t * LANES:(t + 1) * LANES] = (
                u_ref[b, pl.ds(t, n, stride=T), :].astype(BF16))
    lhs = lhs_ref[...]
    x_ref[...] = jnp.dot(lhs, p_ref[0], preferred_element_type=F32)
    a_re = at_ref[0, :, :half]
    a_im = at_ref[0, :, half:]
    for b in range(bt):
        def step(i, h):
            h_re, h_im = h
            row = b * n + i
            hp_ref[pl.ds(row, 1), :half] = h_re
            hp_ref[pl.ds(row, 1), half:] = h_im
            x_re = x_ref[pl.ds(row, 1), :half]
            x_im = x_ref[pl.ds(row, 1), half:]
            return (a_re * h_re - a_im * h_im + x_re, a_re * h_im + a_im * h_re + x_im)
        h_re, h_im = lax.fori_loop(0, n, step, (h0_ref[b, 0, :, :half], h0_ref[b, 0, :, half:]))
        hT_ref[b, 0, :, :half] = h_re
        hT_ref[b, 0, :, half:] = h_im
    y = (jnp.dot(lhs, w_ref[0], preferred_element_type=F32)
         + jnp.dot(hp_ref[...].astype(BF16), q_ref[0], preferred_element_type=F32))
    d = d_ref[...]
    for b in range(bt):
        for t in range(T):
            y_ref[b, pl.ds(t, n, stride=T), :] = (
                y[b * n:(b + 1) * n, t * LANES:(t + 1) * LANES]
                + d * u_ref[b, pl.ds(t, n, stride=T), :])


def _s5(u, h0, w, pmat, qmat, a_t, d_skip, bt):
    B, L, _ = u.shape
    n = L // S5_T
    tl, st = S5_T * LANES, 2 * S5_BLK_STATE
    return pl.pallas_call(
        functools.partial(_s5_kernel, bt=bt, n=n),
        grid=(S5_NBLK, B // bt),
        in_specs=[pl.BlockSpec((bt, L, LANES), lambda k, b: (b, 0, k)),
                  pl.BlockSpec((bt, 1, 1, st), lambda k, b: (b, k, 0, 0)),
                  pl.BlockSpec((1, tl, tl), lambda k, b: (k, 0, 0)),
                  pl.BlockSpec((1, tl, st), lambda k, b: (k, 0, 0)),
                  pl.BlockSpec((1, st, tl), lambda k, b: (k, 0, 0)),
                  pl.BlockSpec((1, 1, st), lambda k, b: (k, 0, 0)),
                  pl.BlockSpec((1, LANES), lambda k, b: (0, k))],
        out_specs=[pl.BlockSpec((bt, L, LANES), lambda k, b: (b, 0, k)),
                   pl.BlockSpec((bt, 1, 1, st), lambda k, b: (b, k, 0, 0))],
        out_shape=[jax.ShapeDtypeStruct((B, L, S5_WIDTH), F32),
                   jax.ShapeDtypeStruct((B, S5_NBLK, 1, st), F32)],
        scratch_shapes=[pltpu.VMEM((bt * n, tl), BF16),
                        pltpu.VMEM((bt * n, st), F32),
                        pltpu.VMEM((bt * n, st), F32)],
        compiler_params=_cparams("arbitrary", "arbitrary"),
        name="s5",
    )(u, h0, w, pmat, qmat, a_t, d_skip.reshape(1, S5_WIDTH))


def _s5_state_in(h_re, h_im):
    B = h_re.shape[0]
    r = h_re.reshape(B, S5_NBLK, 1, S5_BLK_STATE)
    i = h_im.reshape(B, S5_NBLK, 1, S5_BLK_STATE)
    return jnp.concatenate([r, i], -1)


def _s5_state_out(h):
    B = h.shape[0]
    h = h.reshape(B, S5_NBLK, 2, S5_BLK_GROUPS, S5_STATE)
    return (h[:, :, 0].reshape(B, S5_GROUPS, S5_STATE), h[:, :, 1].reshape(B, S5_GROUPS, S5_STATE))


def _inproj_kernel(x_ref, sh_ref, sc_ref, w_ref, u_ref, qkv_ref, z_ref, ba_ref):
    bt, lt, d = x_ref.shape
    h = _ln(x_ref[...]) * (1.0 + sc_ref[:, 0]) + sh_ref[:, 0]
    out = jnp.dot(h.reshape(bt * lt, d).astype(BF16), w_ref[...], preferred_element_type=F32)
    o_q, o_z, o_b = S5_WIDTH, S5_WIDTH + CONV_CH, S5_WIDTH + CONV_CH + GDN_WIDTH
    u_ref[...] = out[:, :o_q].reshape(bt, lt, S5_WIDTH)
    qkv_ref[...] = out[:, o_q:o_z].reshape(bt, lt, CONV_CH)
    z_ref[...] = out[:, o_z:o_b].reshape(bt, lt, GDN_WIDTH)
    ba_ref[...] = out[:, o_b:].reshape(bt, lt, LANES)


def _inproj(x, mod, w_in_p, bt, lt):
    B, L, _ = x.shape
    xs = lambda w: pl.BlockSpec((bt, lt, w), lambda b, i: (b, i, 0))
    ms = lambda idx: pl.BlockSpec((bt, 1, 1, D_MODEL), lambda b, i: (b, idx, 0, 0))
    return pl.pallas_call(
        _inproj_kernel,
        grid=(B // bt, L // lt),
        in_specs=[xs(D_MODEL), ms(0), ms(1),
                  pl.BlockSpec((D_MODEL, IN_PAD), lambda b, i: (0, 0))],
        out_specs=[xs(S5_WIDTH), xs(CONV_CH), xs(GDN_WIDTH), xs(LANES)],
        out_shape=[jax.ShapeDtypeStruct((B, L, w), F32) for w in (S5_WIDTH, CONV_CH, GDN_WIDTH, LANES)],
        compiler_params=_cparams("arbitrary", "arbitrary"),
        name="inproj",
    )(x, mod, mod, w_in_p)


def _mm(a, b):
    return jnp.dot(a.astype(BF16), b.astype(BF16), preferred_element_type=F32)


def _mm_nt(a, b):
    return lax.dot_general(a.astype(BF16), b.astype(BF16), (((1,), (1,)), ((), ())),
                           preferred_element_type=F32)


def _mm_tn(a, b):
    return lax.dot_general(a.astype(BF16), b.astype(BF16), (((0,), (0,)), ((), ())),
                           preferred_element_type=F32)


def _mm_hi(a, b):
    return jnp.dot(a, b, preferred_element_type=F32, precision=HI)


GDN_INV_BASE = 2
GDN_INV_JOIN = 4


def _bdot(a, b):
    return jnp.dot(a, b, preferred_element_type=F32)


def _unit_lower_inverse(a_list, ri, ci, size):
    eye_f = (ri == ci).astype(F32)
    blk = lambda s: (ri // s) == (ci // s)
    prev = min(GDN_INV_BASE, size)
    d_list = [(eye_f - jnp.where(blk(prev), a, 0.0)).astype(BF16) for a in a_list]
    while prev < size:
        cur = min(prev * GDN_INV_JOIN, size)
        off_mask = blk(cur) & jnp.logical_not(blk(prev))
        nxt = []
        for a, d in zip(a_list, d_list):
            m = _bdot(d, jnp.where(off_mask, a, 0.0).astype(BF16))
            r = eye_f - m
            span, mp = 2, m.astype(BF16)
            while span < cur // prev:
                mp = _bdot(mp, mp).astype(BF16)
                r = r + _bdot(r.astype(BF16), mp)
                span *= 2
            nxt.append(_bdot(r.astype(BF16), d).astype(BF16))
        d_list = nxt
        prev = cur
    return d_list


def _gdn_kernel(qkv_ref, z_ref, ba_ref, s0_ref, cb_ref, cw_ref, alog_ref, dtb_ref, ng_ref,
                y_ref, sT_ref, cbo_ref, xbuf_ref, s_ref, *, chunk):
    C, Dh, H = chunk, GDN_HEAD_DIM, GDN_HEADS
    lt = C
    tail = 8

    @pl.when(pl.program_id(1) == 0)
    def _():
        s_ref[...] = s0_ref[0]
        xbuf_ref[0:tail, :] = jnp.zeros((tail, CONV_CH), F32)
        xbuf_ref[tail - (CONV_WIDTH - 1):tail, :] = cb_ref[0]

    xbuf_ref[tail:tail + lt, :] = qkv_ref[0]
    conv = None
    for j in range(CONV_WIDTH):
        off = tail - (CONV_WIDTH - 1) + j
        term = xbuf_ref[off:off + lt, :] * cw_ref[j:j + 1, :]
        conv = term if conv is None else conv + term
    act = _silu(conv)
    cbo_ref[0] = xbuf_ref[lt + tail - (CONV_WIDTH - 1):lt + tail, :]
    xbuf_ref[0:tail, :] = xbuf_ref[lt:lt + tail, :]

    ba = ba_ref[0]
    beta = _sigmoid(ba)
    xa = ba + dtb_ref[...]
    softplus = jnp.maximum(xa, 0.0) + jnp.log1p(jnp.exp(-jnp.abs(xa)))
    g = -jnp.exp(alog_ref[...]) * softplus

    ri = lax.broadcasted_iota(jnp.int32, (C, C), 0)
    ci = lax.broadcasted_iota(jnp.int32, (C, C), 1)
    causal = ri >= ci
    strict = ri > ci
    eye = ri == ci
    ng = ng_ref[...]
    decay = _mm_hi(causal.astype(F32), g)
    heads = range(H)

    a_list, qk_list, rhs_list, qd_list, kdT_list, gl_list = [], [], [], [], [], []
    for h in heads:
        dcol = decay[:, H + h:H + h + 1]
        dmat = jnp.broadcast_to(dcol, (C, C))
        drow = jnp.sum(jnp.where(eye, dmat, 0.0), axis=0, keepdims=True)
        e = jnp.exp(dmat - drow)
        bcol = beta[:, h:h + 1]
        q = act[:, h * Dh:(h + 1) * Dh]
        k = act[:, GDN_WIDTH + h * Dh:GDN_WIDTH + (h + 1) * Dh]
        v = act[:, 2 * GDN_WIDTH + h * Dh:2 * GDN_WIDTH + (h + 1) * Dh]
        q = q * (lax.rsqrt(jnp.sum(q * q, -1, keepdims=True) + RMS_EPS) * (Dh ** -0.5))
        k = k * lax.rsqrt(jnp.sum(k * k, -1, keepdims=True) + RMS_EPS)
        kb = k * bcol
        kT = k.T
        kT_b = kT.astype(BF16)
        edec = jnp.exp(dcol)
        dlast = decay[C - 1:C, H + h:H + h + 1]
        a_list.append(jnp.where(strict, _bdot(kb.astype(BF16), kT_b) * e, 0.0))
        qk_list.append((jnp.where(causal, _bdot(q.astype(BF16), kT_b) * e, 0.0)).astype(BF16))
        rhs_list.append(jnp.concatenate([v * bcol, kb * edec], axis=1).astype(BF16))
        qd_list.append((q * edec).astype(BF16))
        kdT_list.append((kT * jnp.exp(dlast - drow)).astype(BF16))
        gl_list.append(jnp.exp(dlast))

    tinv_list = _unit_lower_inverse(a_list, ri, ci, C)

    for h in heads:
        uw = _bdot(tinv_list[h], rhs_list[h])
        u, w = uw[:, :Dh], uw[:, Dh:]
        s = s_ref[h]
        ws = _bdot(jnp.concatenate([w.astype(BF16), qd_list[h]], axis=0), s.astype(BF16))
        v_new = (u - ws[:C]).astype(BF16)
        o = ws[C:] + _bdot(qk_list[h], v_new)
        s_ref[h] = s * gl_list[h] + _bdot(kdT_list[h], v_new)
        o = o * lax.rsqrt(jnp.mean(o * o, -1, keepdims=True) + RMS_EPS) * ng
        y_ref[0, :, h * Dh:(h + 1) * Dh] = o * _silu(z_ref[0, :, h * Dh:(h + 1) * Dh])

    sT_ref[0] = s_ref[...]


def _gdn(qkv, z, ba, s0, conv_buf, conv_w, a_log, dt_bias, norm_g, chunk):
    B, L, _ = qkv.shape
    H, Dh = GDN_HEADS, GDN_HEAD_DIM
    lt = chunk
    place = lambda t: jnp.zeros((1, LANES), F32).at[0, H:2 * H].set(t)
    xs = lambda w: pl.BlockSpec((1, lt, w), lambda b, i: (b, i, 0))
    full = lambda *shape: pl.BlockSpec(shape, lambda b, i: (0,) * len(shape))
    return pl.pallas_call(
        functools.partial(_gdn_kernel, chunk=chunk),
        grid=(B, L // lt),
        in_specs=[xs(CONV_CH), xs(GDN_WIDTH), xs(LANES),
                  pl.BlockSpec((1, H, Dh, Dh), lambda b, i: (b, 0, 0, 0)),
                  pl.BlockSpec((1, CONV_WIDTH - 1, CONV_CH), lambda b, i: (b, 0, 0)),
                  full(CONV_WIDTH, CONV_CH), full(1, LANES), full(1, LANES), full(1, Dh)],
        out_specs=[xs(GDN_WIDTH),
                   pl.BlockSpec((1, H, Dh, Dh), lambda b, i: (b, 0, 0, 0)),
                   pl.BlockSpec((1, CONV_WIDTH - 1, CONV_CH), lambda b, i: (b, 0, 0))],
        out_shape=[jax.ShapeDtypeStruct((B, L, GDN_WIDTH), F32),
                   jax.ShapeDtypeStruct((B, H, Dh, Dh), F32),
                   jax.ShapeDtypeStruct((B, CONV_WIDTH - 1, CONV_CH), F32)],
        scratch_shapes=[pltpu.VMEM((lt + 8, CONV_CH), F32),
                        pltpu.VMEM((H, Dh, Dh), F32)],
        compiler_params=_cparams("arbitrary", "arbitrary"),
        name="gdn",
    )(qkv, z, ba, s0, conv_buf, conv_w, place(a_log), place(dt_bias), norm_g.reshape(1, Dh))


FF_BLK = 1408


def _gelu_tanh(x):
    return 0.5 * x * (1.0 + jnp.tanh(0.7978845608028654 * (x + 0.044715 * x * x * x)))


def _outffn_kernel(x_ref, ya_ref, yb_ref, g1_ref, sh2_ref, sc2_ref, g2_ref,
                   wglu_ref, wout_ref, l1g_ref, l1b_ref, wup_ref, wdn_ref, l2g_ref, l2b_ref,
                   o_ref):
    bt, lt, d = x_ref.shape
    m = bt * lt
    flat = lambda t: t.reshape(m, t.shape[-1])
    unflat = lambda t: t.reshape(bt, lt, t.shape[-1])
    zz = _mm(_gelu_tanh(flat(ya_ref[...])), wglu_ref[...])
    y_a = zz[:, :S5_WIDTH] * _sigmoid(zz[:, S5_WIDTH:])
    mix = _mm(y_a, wout_ref[:S5_WIDTH, :]) + _mm(flat(yb_ref[...]), wout_ref[S5_WIDTH:, :])
    x1 = _ln(ALPHA * x_ref[...] + (1.0 + g1_ref[:, 0]) * unflat(mix)) * l1g_ref[...] + l1b_ref[...]
    h = flat(_ln(x1) * (1.0 + sc2_ref[:, 0]) + sh2_ref[:, 0]).astype(BF16)
    acc = None
    for j in range(D_FF // FF_BLK):
        gate = jnp.dot(h, wup_ref[:, j * FF_BLK:(j + 1) * FF_BLK], preferred_element_type=F32)
        up = jnp.dot(h, wup_ref[:, D_FF + j * FF_BLK:D_FF + (j + 1) * FF_BLK], preferred_element_type=F32)
        part = _mm(_silu(gate) * up, wdn_ref[j * FF_BLK:(j + 1) * FF_BLK, :])
        acc = part if acc is None else acc + part
    o_ref[...] = _ln(ALPHA * x1 + (1.0 + g2_ref[:, 0]) * unflat(acc)) * l2g_ref[...] + l2b_ref[...]


def _outffn(x, ya, yb, mod, w_glu, w_out, ln1_g, ln1_b, w_up, w_dn, ln2_g, ln2_b, bt, lt):
    B, L, _ = x.shape
    xs = lambda w: pl.BlockSpec((bt, lt, w), lambda b, i: (b, i, 0))
    ms = lambda idx: pl.BlockSpec((bt, 1, 1, D_MODEL), lambda b, i: (b, idx, 0, 0))
    const = lambda a: pl.BlockSpec(a.shape, lambda b, i: (0,) * a.ndim, pipeline_mode=pl.Buffered(1))
    vec = lambda t: t.reshape(1, D_MODEL)
    weights = (w_glu, w_out, vec(ln1_g), vec(ln1_b), w_up, w_dn, vec(ln2_g), vec(ln2_b))
    return pl.pallas_call(
        _outffn_kernel,
        grid=(B // bt, L // lt),
        in_specs=[xs(D_MODEL), xs(S5_WIDTH), xs(GDN_WIDTH), ms(2), ms(3), ms(4), ms(5)]
                 + [const(a) for a in weights],
        out_specs=xs(D_MODEL),
        out_shape=jax.ShapeDtypeStruct((B, L, D_MODEL), F32),
        compiler_params=_cparams("arbitrary", "arbitrary"),
        name="outffn",
    )(x, ya, yb, mod, mod, mod, mod, *weights)


def _layer(x, mod, h_re, h_im, s_gdn, conv_buf, wts, bt, lt, gdn_chunk):
    (w_in_p, s5_mats, s5_d, w_glu, conv_w, a_log, dt_bias, norm_g, w_out,
     ln1_g, ln1_b, w_up, w_dn, ln2_g, ln2_b) = wts
    u, qkv, z, ba = _inproj(x, mod, w_in_p, bt, lt)
    w, pmat, qmat, a_t = s5_mats
    ys5, hT = _s5(u, _s5_state_in(h_re, h_im), w, pmat, qmat, a_t, s5_d, bt)
    yb, sT, cbo = _gdn(qkv, z, ba, s_gdn, conv_buf, conv_w, a_log, dt_bias, norm_g, gdn_chunk)
    y = _outffn(x, ys5, yb, mod, w_glu, w_out, ln1_g, ln1_b, w_up, w_dn, ln2_g, ln2_b, bt, lt)
    o_re, o_im = _s5_state_out(hT)
    return y, o_re, o_im, sT, cbo


def kernel(x_prompt, x_sample, state_s5_re, state_s5_im, state_gdn, cache_gdn_conv, c_prompt, c_sample, w_ada, b_ada, w_in, s5_a_re, s5_a_im, s5_log_dt, s5_b_re, s5_b_im, s5_c_re, s5_c_im, s5_d, w_s5_glu, gdn_conv_w, gdn_a_log, gdn_dt_bias, gdn_norm_g, w_out, ln1_g, ln1_b, w_ffn_up, w_ffn_down, ln2_g, ln2_b):
    bp, bs = x_prompt.shape[0], x_sample.shape[0]
    assert w_ada.shape[0] == DEPTH == 1
    l = 0
    mod = _ada(jnp.concatenate([c_prompt, c_sample], 0), w_ada[l], b_ada[l])
    mod = mod.reshape(bp + bs, 6, 1, D_MODEL)
    w_in_p = jnp.pad(w_in[l], ((0, 0), (0, IN_PAD - w_in.shape[-1]))).astype(BF16)
    s5_mats = _s5_prep(s5_a_re[l], s5_a_im[l], s5_log_dt[l], s5_b_re[l], s5_b_im[l],
                       s5_c_re[l], s5_c_im[l])
    wts = (w_in_p, s5_mats, s5_d[l], w_s5_glu[l].astype(BF16), gdn_conv_w[l], gdn_a_log[l],
           gdn_dt_bias[l], gdn_norm_g[l], w_out[l].astype(BF16), ln1_g[l], ln1_b[l],
           w_ffn_up[l].astype(BF16), w_ffn_down[l].astype(BF16), ln2_g[l], ln2_b[l])
    zeros = lambda *s: jnp.zeros(s, F32)
    yp, p_re, p_im, p_gdn, p_conv = _layer(
        x_prompt, mod[:bp],
        zeros(bp, S5_GROUPS, S5_STATE), zeros(bp, S5_GROUPS, S5_STATE),
        zeros(bp, GDN_HEADS, GDN_HEAD_DIM, GDN_HEAD_DIM), zeros(bp, CONV_WIDTH - 1, CONV_CH),
        wts, bt=1, lt=512, gdn_chunk=256)
    ys, s_re, s_im, s_gdn, s_conv = _layer(
        x_sample, mod[bp:], state_s5_re[l], state_s5_im[l], state_gdn[l], cache_gdn_conv[l],
        wts, bt=bs, lt=x_sample.shape[1], gdn_chunk=x_sample.shape[1])
    st = lambda t: t[None]
    return (yp, ys, st(p_re), st(p_im), st(p_gdn), st(p_conv),
            st(s_re), st(s_im), st(s_gdn), st(s_conv))
```

```python
import functools

import jax
import jax.numpy as jnp
from jax import lax
from jax.experimental import pallas as pl
from jax.experimental.pallas import tpu as pltpu

F32 = jnp.float32
BF16 = jnp.bfloat16

D_MODEL = 1024
S5_WIDTH = 512
S5_GROUP = 16
S5_GROUPS = 32
S5_STATE = 64
GDN_WIDTH = 512
GDN_HEAD_DIM = 128
GDN_HEADS = 4
CONV_WIDTH = 4
CONV_CH = 3 * GDN_WIDTH
D_FF = 2816
GDN_CHUNK = 64
LN_EPS = 1e-5
RMS_EPS = 1e-6
DEPTH = 1
ALPHA = (2 * DEPTH) ** 0.25

LANES = 128
S5_T = 8
S5_BLK_GROUPS = LANES // S5_GROUP
S5_NBLK = S5_GROUPS // S5_BLK_GROUPS
S5_BLK_STATE = S5_BLK_GROUPS * S5_STATE
IN_PAD = S5_WIDTH + 4 * GDN_WIDTH + LANES
VMEM_LIMIT = 56 * 1024 * 1024

HI = lax.Precision.HIGHEST


def _cparams(*sem):
    return pltpu.CompilerParams(dimension_semantics=sem, vmem_limit_bytes=VMEM_LIMIT)


def _ln(x):
    mu = jnp.mean(x, -1, keepdims=True)
    xc = x - mu
    var = jnp.mean(xc * xc, -1, keepdims=True)
    return xc * lax.rsqrt(var + LN_EPS)


def _sigmoid(x):
    return 1.0 / (1.0 + jnp.exp(-x))


def _silu(x):
    return x * _sigmoid(x)


def _ada_kernel(c_ref, w_ref, b_ref, o_ref):
    c = c_ref[...]
    o_ref[...] = jnp.dot(_silu(c), w_ref[...], preferred_element_type=F32,
                         precision=HI) + b_ref[...]


def _ada(c, w_ada, b_ada):
    rows = c.shape[0]
    ncol = w_ada.shape[1]
    blk = D_MODEL
    return pl.pallas_call(
        _ada_kernel,
        grid=(ncol // blk,),
        in_specs=[pl.BlockSpec((rows, D_MODEL), lambda j: (0, 0)),
                  pl.BlockSpec((D_MODEL, blk), lambda j: (0, j)),
                  pl.BlockSpec((1, blk), lambda j: (0, j))],
        out_specs=pl.BlockSpec((rows, blk), lambda j: (0, j)),
        out_shape=jax.ShapeDtypeStruct((rows, ncol), F32),
        compiler_params=_cparams("arbitrary"),
        name="ada",
    )(c, w_ada, b_ada.reshape(1, ncol))


def _s5prep_kernel(are_ref, aim_ref, ldt_ref, btre_ref, btim_ref, ctre_ref, ctim_ref,
                   w_ref, p_ref, qt_ref, at_ref):
    T, half = S5_T, S5_BLK_STATE
    a_re = are_ref[0]
    a_im = aim_ref[0]
    dt = jnp.exp(ldt_ref[0])
    den = a_re * a_re + a_im * a_im

    def lam_pow(tau):
        mag = jnp.exp(a_re * dt * tau)
        return mag * jnp.cos(a_im * dt * tau), mag * jnp.sin(a_im * dt * tau)

    l_re, l_im = lam_pow(1.0)
    n_re, n_im = l_re - 1.0, l_im
    f_re = (n_re * a_re + n_im * a_im) / den
    f_im = (n_im * a_re - n_re * a_im) / den
    same_group = (lax.broadcasted_iota(jnp.int32, (LANES, half), 0) // S5_GROUP
                  == lax.broadcasted_iota(jnp.int32, (LANES, half), 1) // S5_STATE)
    bm_re = jnp.where(same_group, btre_ref[0], 0.0)
    bm_im = jnp.where(same_group, btim_ref[0], 0.0)
    cm_re = jnp.where(same_group, ctre_ref[0], 0.0)
    cm_im = jnp.where(same_group, ctim_ref[0], 0.0)
    dot_nt = lambda a, b: lax.dot_general(a, b, (((1,), (1,)), ((), ())), precision=HI,
                                          preferred_element_type=F32)
    blk = lambda i: slice(i * LANES, (i + 1) * LANES)
    for s in range(T):
        for t in range(s):
            w_ref[0, blk(s), blk(t)] = jnp.zeros((LANES, LANES), BF16)
    for tau in range(T + 1):
        p_re, p_im = (jnp.ones_like(a_re), jnp.zeros_like(a_re)) if tau == 0 else lam_pow(float(tau))
        if tau == T:
            at_ref[0, :, :half] = p_re
            at_ref[0, :, half:] = p_im
        if tau < T:
            g_re = p_re * f_re - p_im * f_im
            g_im = p_re * f_im + p_im * f_re
            inj_re = g_re * bm_re - g_im * bm_im
            inj_im = g_re * bm_im + g_im * bm_re
            s = T - 1 - tau
            p_ref[0, blk(s), :half] = inj_re.astype(BF16)
            p_ref[0, blk(s), half:] = inj_im.astype(BF16)
            k_tau = (dot_nt(inj_re, cm_re) - dot_nt(inj_im, cm_im)).astype(BF16)
            for s in range(T - tau):
                w_ref[0, blk(s), blk(s + tau)] = k_tau
        if tau >= 1:
            qt_ref[0, blk(tau - 1), :half] = (cm_re * p_re - cm_im * p_im).astype(BF16)
            qt_ref[0, blk(tau - 1), half:] = (-(cm_re * p_im + cm_im * p_re)).astype(BF16)


def _s5_prep(a_re, a_im, log_dt, b_re, b_im, c_re, c_im):
    NB, T, half = S5_NBLK, S5_T, S5_BLK_STATE
    tl, st = T * LANES, 2 * half
    lanes = lambda t: t.reshape(NB, 1, half)
    tiled = lambda t: jnp.tile(t.reshape(NB, LANES, S5_STATE), (1, 1, S5_BLK_GROUPS))
    row = pl.BlockSpec((1, 1, half), lambda k: (k, 0, 0))
    mat = pl.BlockSpec((1, LANES, half), lambda k: (k, 0, 0))
    return pl.pallas_call(
        _s5prep_kernel,
        grid=(NB,),
        in_specs=[row, row, row, mat, mat, mat, mat],
        out_specs=[pl.BlockSpec((1, tl, tl), lambda k: (k, 0, 0)),
                   pl.BlockSpec((1, tl, st), lambda k: (k, 0, 0)),
                   pl.BlockSpec((1, tl, st), lambda k: (k, 0, 0)),
                   pl.BlockSpec((1, 1, st), lambda k: (k, 0, 0))],
        out_shape=[jax.ShapeDtypeStruct((NB, tl, tl), BF16),
                   jax.ShapeDtypeStruct((NB, tl, st), BF16),
                   jax.ShapeDtypeStruct((NB, tl, st), BF16),
                   jax.ShapeDtypeStruct((NB, 1, st), F32)],
        compiler_params=_cparams("arbitrary"),
        name="s5prep",
    )(lanes(a_re), lanes(a_im), lanes(jnp.repeat(log_dt, S5_STATE)),
      tiled(jnp.swapaxes(b_re, 1, 2)), tiled(jnp.swapaxes(b_im, 1, 2)), tiled(c_re), tiled(c_im))


def _s5_kernel(u_ref, h0_ref, w_ref, p_ref, q_ref, at_ref, d_ref, y_ref, hT_ref,
               lhs_ref, x_ref, hp_ref, *, bt, n):
    T, half = S5_T, S5_BLK_STATE
    for b in range(bt):
        for t in range(T):
            lhs_ref[b * n:(b + 1) * n, t * LANES:(t + 1) * LANES] = (
                u_ref[b, pl.ds(t, n, stride=T), :].astype(BF16))
    lhs = lhs_ref[...]
    x_ref[...] = jnp.dot(lhs, p_ref[0], preferred_element_type=F32)
    a_re = at_ref[0, :, :half]
    a_im = at_ref[0, :, half:]
    for b in range(bt):
        def step(i, h):
            h_re, h_im = h
            row = b * n + i
            hp_ref[pl.ds(row, 1), :half] = h_re
            hp_ref[pl.ds(row, 1), half:] = h_im
            x_re = x_ref[pl.ds(row, 1), :half]
            x_im = x_ref[pl.ds(row, 1), half:]
            return (a_re * h_re - a_im * h_im + x_re, a_re * h_im + a_im * h_re + x_im)
        h_re, h_im = lax.fori_loop(0, n, step, (h0_ref[b, 0, :, :half], h0_ref[b, 0, :, half:]))
        hT_ref[b, 0, :, :half] = h_re
        hT_ref[b, 0, :, half:] = h_im
    y = (jnp.dot(lhs, w_ref[0], preferred_element_type=F32)
         + lax.dot_general(hp_ref[...].astype(BF16), q_ref[0], (((1,), (1,)), ((), ())),
                           preferred_element_type=F32))
    d = d_ref[...]
    for b in range(bt):
        for t in range(T):
            y_ref[b, pl.ds(t, n, stride=T), :] = (
                y[b * n:(b + 1) * n, t * LANES:(t + 1) * LANES]
                + d * u_ref[b, pl.ds(t, n, stride=T), :])


def _s5(u, h0, w, pmat, qmat, a_t, d_skip, bt):
    B, L, _ = u.shape
    n = L // S5_T
    tl, st = S5_T * LANES, 2 * S5_BLK_STATE
    return pl.pallas_call(
        functools.partial(_s5_kernel, bt=bt, n=n),
        grid=(S5_NBLK, B // bt),
        in_specs=[pl.BlockSpec((bt, L, LANES), lambda k, b: (b, 0, k)),
                  pl.BlockSpec((bt, 1, 1, st), lambda k, b: (b, k, 0, 0)),
                  pl.BlockSpec((1, tl, tl), lambda k, b: (k, 0, 0)),
                  pl.BlockSpec((1, tl, st), lambda k, b: (k, 0, 0)),
                  pl.BlockSpec((1, tl, st), lambda k, b: (k, 0, 0)),
                  pl.BlockSpec((1, 1, st), lambda k, b: (k, 0, 0)),
                  pl.BlockSpec((1, LANES), lambda k, b: (0, k))],
        out_specs=[pl.BlockSpec((bt, L, LANES), lambda k, b: (b, 0, k)),
                   pl.BlockSpec((bt, 1, 1, st), lambda k, b: (b, k, 0, 0))],
        out_shape=[jax.ShapeDtypeStruct((B, L, S5_WIDTH), F32),
                   jax.ShapeDtypeStruct((B, S5_NBLK, 1, st), F32)],
        scratch_shapes=[pltpu.VMEM((bt * n, tl), BF16),
                        pltpu.VMEM((bt * n, st), F32),
                        pltpu.VMEM((bt * n, st), F32)],
        compiler_params=_cparams("arbitrary", "arbitrary"),
        name="s5",
    )(u, h0, w, pmat, qmat, a_t, d_skip.reshape(1, S5_WIDTH))


def _s5_state_in(h_re, h_im):
    B = h_re.shape[0]
    r = h_re.reshape(B, S5_NBLK, 1, S5_BLK_STATE)
    i = h_im.reshape(B, S5_NBLK, 1, S5_BLK_STATE)
    return jnp.concatenate([r, i], -1)


def _s5_state_out(h):
    B = h.shape[0]
    h = h.reshape(B, S5_NBLK, 2, S5_BLK_GROUPS, S5_STATE)
    return (h[:, :, 0].reshape(B, S5_GROUPS, S5_STATE), h[:, :, 1].reshape(B, S5_GROUPS, S5_STATE))


def _inproj_kernel(x_ref, sh_ref, sc_ref, w_ref, u_ref, qkv_ref, z_ref, ba_ref):
    bt, lt, d = x_ref.shape
    h = _ln(x_ref[...]) * (1.0 + sc_ref[:, 0]) + sh_ref[:, 0]
    out = jnp.dot(h.reshape(bt * lt, d).astype(BF16), w_ref[...], preferred_element_type=F32)
    o_q, o_z, o_b = S5_WIDTH, S5_WIDTH + CONV_CH, S5_WIDTH + CONV_CH + GDN_WIDTH
    u_ref[...] = out[:, :o_q].reshape(bt, lt, S5_WIDTH)
    qkv_ref[...] = out[:, o_q:o_z].reshape(bt, lt, CONV_CH)
    z_ref[...] = out[:, o_z:o_b].reshape(bt, lt, GDN_WIDTH)
    ba_ref[...] = out[:, o_b:].reshape(bt, lt, LANES)


def _inproj(x, mod, w_in_p, bt, lt):
    B, L, _ = x.shape
    xs = lambda w: pl.BlockSpec((bt, lt, w), lambda b, i: (b, i, 0))
    ms = lambda idx: pl.BlockSpec((bt, 1, 1, D_MODEL), lambda b, i: (b, idx, 0, 0))
    return pl.pallas_call(
        _inproj_kernel,
        grid=(B // bt, L // lt),
        in_specs=[xs(D_MODEL), ms(0), ms(1),
                  pl.BlockSpec((D_MODEL, IN_PAD), lambda b, i: (0, 0))],
        out_specs=[xs(S5_WIDTH), xs(CONV_CH), xs(GDN_WIDTH), xs(LANES)],
        out_shape=[jax.ShapeDtypeStruct((B, L, w), F32) for w in (S5_WIDTH, CONV_CH, GDN_WIDTH, LANES)],
        compiler_params=_cparams("arbitrary", "arbitrary"),
        name="inproj",
    )(x, mod, mod, w_in_p)


def _mm(a, b):
    return jnp.dot(a.astype(BF16), b.astype(BF16), preferred_element_type=F32)


def _mm_nt(a, b):
    return lax.dot_general(a.astype(BF16), b.astype(BF16), (((1,), (1,)), ((), ())),
                           preferred_element_type=F32)


def _mm_tn(a, b):
    return lax.dot_general(a.astype(BF16), b.astype(BF16), (((0,), (0,)), ((), ())),
                           preferred_element_type=F32)


def _mm_hi(a, b):
    return jnp.dot(a, b, preferred_element_type=F32, precision=HI)


GDN_INV_BASE = 2
GDN_INV_JOIN = 4


def _bdot(a, b):
    return jnp.dot(a, b, preferred_element_type=F32)


def _unit_lower_inverse(a_list, ri, ci, size):
    eye_f = (ri == ci).astype(F32)
    blk = lambda s: (ri // s) == (ci // s)
    prev = min(GDN_INV_BASE, size)
    d_list = [(eye_f - jnp.where(blk(prev), a, 0.0)).astype(BF16) for a in a_list]
    while prev < size:
        cur = min(prev * GDN_INV_JOIN, size)
        off_mask = blk(cur) & jnp.logical_not(blk(prev))
        ms = [_bdot(d, jnp.where(off_mask, a, 0.0).astype(BF16)) for a, d in zip(a_list, d_list)]
        rs = [eye_f - m for m in ms]
        mps = [m.astype(BF16) for m in ms]
        span = 2
        while span < cur // prev:
            mps = [_bdot(mp, mp).astype(BF16) for mp in mps]
            rs = [r + _bdot(r.astype(BF16), mp) for r, mp in zip(rs, mps)]
            span *= 2
        d_list = [_bdot(r.astype(BF16), d).astype(BF16) for r, d in zip(rs, d_list)]
        prev = cur
    return d_list


def _gdn_kernel(qkv_ref, z_ref, ba_ref, s0_ref, cb_ref, cw_ref, alog_ref, dtb_ref, ng_ref,
                y_ref, sT_ref, cbo_ref, xbuf_ref, s_ref, *, chunk, bt):
    C, Dh, H = chunk, GDN_HEAD_DIM, GDN_HEADS
    lt = C
    tail = 8

    @pl.when(pl.program_id(1) == 0)
    def _():
        s_ref[...] = s0_ref[...]
        xbuf_ref[:, 0:tail, :] = jnp.zeros((bt, tail, CONV_CH), F32)
        xbuf_ref[:, tail - (CONV_WIDTH - 1):tail, :] = cb_ref[...]

    xbuf_ref[:, tail:tail + lt, :] = qkv_ref[...]
    conv = None
    for j in range(CONV_WIDTH):
        off = tail - (CONV_WIDTH - 1) + j
        term = xbuf_ref[:, off:off + lt, :] * cw_ref[j:j + 1, :]
        conv = term if conv is None else conv + term
    act_all = _silu(conv)
    cbo_ref[...] = xbuf_ref[:, lt + tail - (CONV_WIDTH - 1):lt + tail, :]
    xbuf_ref[:, 0:tail, :] = xbuf_ref[:, lt:lt + tail, :]

    ba = ba_ref[...]
    beta_all = _sigmoid(ba)
    xa = ba + dtb_ref[...]
    softplus = jnp.maximum(xa, 0.0) + jnp.log1p(jnp.exp(-jnp.abs(xa)))
    g = -jnp.exp(alog_ref[...]) * softplus

    ri = lax.broadcasted_iota(jnp.int32, (C, C), 0)
    ci = lax.broadcasted_iota(jnp.int32, (C, C), 1)
    causal = ri >= ci
    strict = ri > ci
    eye = ri == ci
    ng = ng_ref[...]
    tril = causal.astype(F32)
    decay_all = [_mm_hi(tril, g[b]) for b in range(bt)]
    units = [(b, h) for b in range(bt) for h in range(H)]

    e_list, kbq_list, kT_list, rhs_list, qd_list, kdT_list, gl_list = [], [], [], [], [], [], []
    for b, h in units:
        decay, beta, act = decay_all[b], beta_all[b], act_all[b]
        dcol = decay[:, H + h:H + h + 1]
        dmat = jnp.broadcast_to(dcol, (C, C))
        drow = jnp.sum(jnp.where(eye, dmat, 0.0), axis=0, keepdims=True)
        e = jnp.exp(dmat - drow)
        bcol = beta[:, h:h + 1]
        q = act[:, h * Dh:(h + 1) * Dh]
        k = act[:, GDN_WIDTH + h * Dh:GDN_WIDTH + (h + 1) * Dh]
        v = act[:, 2 * GDN_WIDTH + h * Dh:2 * GDN_WIDTH + (h + 1) * Dh]
        q = q * (lax.rsqrt(jnp.sum(q * q, -1, keepdims=True) + RMS_EPS) * (Dh ** -0.5))
        k = k * lax.rsqrt(jnp.sum(k * k, -1, keepdims=True) + RMS_EPS)
        kb = k * bcol
        kT = k.T
        kT_b = kT.astype(BF16)
        edec = jnp.exp(dcol)
        dlast = decay[C - 1:C, H + h:H + h + 1]
        e_list.append(e)
        kbq_list.append(jnp.concatenate([kb, q], axis=0).astype(BF16))
        kT_list.append(kT_b)
        rhs_list.append(jnp.concatenate([v * bcol, kb * edec], axis=1).astype(BF16))
        qd_list.append((q * edec).astype(BF16))
        kdT_list.append((kT * jnp.exp(dlast - drow)).astype(BF16))
        gl_list.append(jnp.exp(dlast))
    sc_list = [_bdot(kbq, kT_b) for kbq, kT_b in zip(kbq_list, kT_list)]
    a_list = [jnp.where(strict, sc[:C] * e, 0.0) for sc, e in zip(sc_list, e_list)]
    qk_list = [jnp.where(causal, sc[C:] * e, 0.0).astype(BF16) for sc, e in zip(sc_list, e_list)]

    tinv_list = _unit_lower_inverse(a_list, ri, ci, C)

    uw_list = [_bdot(t, rhs) for t, rhs in zip(tinv_list, rhs_list)]
    s_list = [s_ref[b, h] for b, h in units]
    ws_list = [_bdot(jnp.concatenate([uw[:, Dh:].astype(BF16), qd], axis=0), s.astype(BF16))
               for uw, qd, s in zip(uw_list, qd_list, s_list)]
    vn_list = [(uw[:, :Dh] - ws[:C]).astype(BF16) for uw, ws in zip(uw_list, ws_list)]
    o_list = [ws[C:] + _bdot(qk, vn) for ws, qk, vn in zip(ws_list, qk_list, vn_list)]
    for i, (b, h) in enumerate(units):
        s_ref[b, h] = s_list[i] * gl_list[i] + _bdot(kdT_list[i], vn_list[i])
    for i, (b, h) in enumerate(units):
        o = o_list[i]
        o = o * lax.rsqrt(jnp.mean(o * o, -1, keepdims=True) + RMS_EPS) * ng
        y_ref[b, :, h * Dh:(h + 1) * Dh] = o * _silu(z_ref[b, :, h * Dh:(h + 1) * Dh])

    sT_ref[...] = s_ref[...]


def _gdn(qkv, z, ba, s0, conv_buf, conv_w, a_log, dt_bias, norm_g, chunk, bt):
    B, L, _ = qkv.shape
    H, Dh = GDN_HEADS, GDN_HEAD_DIM
    lt = chunk
    place = lambda t: jnp.zeros((1, LANES), F32).at[0, H:2 * H].set(t)
    xs = lambda w: pl.BlockSpec((bt, lt, w), lambda b, i: (b, i, 0))
    full = lambda *shape: pl.BlockSpec(shape, lambda b, i: (0,) * len(shape))
    return pl.pallas_call(
        functools.partial(_gdn_kernel, chunk=chunk, bt=bt),
        grid=(B // bt, L // lt),
        in_specs=[xs(CONV_CH), xs(GDN_WIDTH), xs(LANES),
                  pl.BlockSpec((bt, H, Dh, Dh), lambda b, i: (b, 0, 0, 0)),
                  pl.BlockSpec((bt, CONV_WIDTH - 1, CONV_CH), lambda b, i: (b, 0, 0)),
                  full(CONV_WIDTH, CONV_CH), full(1, LANES), full(1, LANES), full(1, Dh)],
        out_specs=[xs(GDN_WIDTH),
                   pl.BlockSpec((bt, H, Dh, Dh), lambda b, i: (b, 0, 0, 0)),
                   pl.BlockSpec((bt, CONV_WIDTH - 1, CONV_CH), lambda b, i: (b, 0, 0))],
        out_shape=[jax.ShapeDtypeStruct((B, L, GDN_WIDTH), F32),
                   jax.ShapeDtypeStruct((B, H, Dh, Dh), F32),
                   jax.ShapeDtypeStruct((B, CONV_WIDTH - 1, CONV_CH), F32)],
        scratch_shapes=[pltpu.VMEM((bt, lt + 8, CONV_CH), F32),
                        pltpu.VMEM((bt, H, Dh, Dh), F32)],
        compiler_params=_cparams("arbitrary", "arbitrary"),
        name="gdn",
    )(qkv, z, ba, s0, conv_buf, conv_w, place(a_log), place(dt_bias), norm_g.reshape(1, Dh))


FF_BLK = 256


def _gelu_tanh(x):
    return 0.5 * x * (1.0 + jnp.tanh(0.7978845608028654 * (x + 0.044715 * x * x * x)))


def _outffn_kernel(x_ref, ya_ref, yb_ref, g1_ref, sh2_ref, sc2_ref, g2_ref,
                   wglu_ref, wout_ref, l1g_ref, l1b_ref, wup_ref, wdn_ref, l2g_ref, l2b_ref,
                   o_ref):
    bt, lt, d = x_ref.shape
    if bt % 2 == 0:
        subs = [(slice(i * bt // 2, (i + 1) * bt // 2), slice(0, lt)) for i in range(2)]
    else:
        subs = [(slice(0, bt), slice(i * lt // 2, (i + 1) * lt // 2)) for i in range(2)]
    rd = lambda ref, sub: ref[sub[0], sub[1], :]
    flat = lambda t: t.reshape(t.shape[0] * t.shape[1], t.shape[-1])
    unflat = lambda t, like: t.reshape(like.shape[0], like.shape[1], t.shape[-1])
    mod = lambda ref, sub: ref[sub[0], 0]
    nsub = range(len(subs))

    zz = [_mm(_gelu_tanh(flat(rd(ya_ref, s))), wglu_ref[...]) for s in subs]
    y_a = [z[:, :S5_WIDTH] * _sigmoid(z[:, S5_WIDTH:]) for z in zz]
    mix_b = [_mm(flat(rd(yb_ref, s)), wout_ref[S5_WIDTH:, :]) for s in subs]
    mix = [mb + _mm(ya, wout_ref[:S5_WIDTH, :]) for mb, ya in zip(mix_b, y_a)]
    xs = [rd(x_ref, s) for s in subs]
    x1 = [_ln(ALPHA * x + (1.0 + mod(g1_ref, s)) * unflat(mx, x)) * l1g_ref[...] + l1b_ref[...]
          for x, mx, s in zip(xs, mix, subs)]
    h = [flat(_ln(x) * (1.0 + mod(sc2_ref, s)) + mod(sh2_ref, s)).astype(BF16) for x, s in zip(x1, subs)]

    def gate_up(i, j):
        gate = _bdot(h[i], wup_ref[:, j * FF_BLK:(j + 1) * FF_BLK])
        up = _bdot(h[i], wup_ref[:, D_FF + j * FF_BLK:D_FF + (j + 1) * FF_BLK])
        return gate, up

    nblk = D_FF // FF_BLK
    acc = [None for _ in nsub]
    gu = [gate_up(i, 0) for i in nsub]
    for j in range(nblk):
        gu_next = [gate_up(i, j + 1) for i in nsub] if j + 1 < nblk else None
        for i in nsub:
            gate, up = gu[i]
            part = _mm(_silu(gate) * up, wdn_ref[j * FF_BLK:(j + 1) * FF_BLK, :])
            acc[i] = part if acc[i] is None else acc[i] + part
        gu = gu_next
    for i, s in enumerate(subs):
        o_ref[s[0], s[1], :] = (_ln(ALPHA * x1[i] + (1.0 + mod(g2_ref, s)) * unflat(acc[i], x1[i]))
                                * l2g_ref[...] + l2b_ref[...])


def _outffn(x, ya, yb, mod, w_glu, w_out, ln1_g, ln1_b, w_up, w_dn, ln2_g, ln2_b, bt, lt):
    B, L, _ = x.shape
    xs = lambda w: pl.BlockSpec((bt, lt, w), lambda b, i: (b, i, 0))
    ms = lambda idx: pl.BlockSpec((bt, 1, 1, D_MODEL), lambda b, i: (b, idx, 0, 0))
    const = lambda a: pl.BlockSpec(a.shape, lambda b, i: (0,) * a.ndim, pipeline_mode=pl.Buffered(1))
    vec = lambda t: t.reshape(1, D_MODEL)
    weights = (w_glu, w_out, vec(ln1_g), vec(ln1_b), w_up, w_dn, vec(ln2_g), vec(ln2_b))
    return pl.pallas_call(
        _outffn_kernel,
        grid=(B // bt, L // lt),
        in_specs=[xs(D_MODEL), xs(S5_WIDTH), xs(GDN_WIDTH), ms(2), ms(3), ms(4), ms(5)]
                 + [const(a) for a in weights],
        out_specs=xs(D_MODEL),
        out_shape=jax.ShapeDtypeStruct((B, L, D_MODEL), F32),
        compiler_params=_cparams("arbitrary", "arbitrary"),
        name="outffn",
    )(x, ya, yb, mod, mod, mod, mod, *weights)


def _layer(x, mod, h_re, h_im, s_gdn, conv_buf, wts, bt, lt, gdn_chunk, gdn_bt):
    (w_in_p, s5_mats, s5_d, w_glu, conv_w, a_log, dt_bias, norm_g, w_out,
     ln1_g, ln1_b, w_up, w_dn, ln2_g, ln2_b) = wts
    u, qkv, z, ba = _inproj(x, mod, w_in_p, bt, lt)
    w, pmat, qmat, a_t = s5_mats
    ys5, hT = _s5(u, _s5_state_in(h_re, h_im), w, pmat, qmat, a_t, s5_d, bt)
    yb, sT, cbo = _gdn(qkv, z, ba, s_gdn, conv_buf, conv_w, a_log, dt_bias, norm_g, gdn_chunk, gdn_bt)
    y = _outffn(x, ys5, yb, mod, w_glu, w_out, ln1_g, ln1_b, w_up, w_dn, ln2_g, ln2_b, bt, lt)
    o_re, o_im = _s5_state_out(hT)
    return y, o_re, o_im, sT, cbo


def kernel(x_prompt, x_sample, state_s5_re, state_s5_im, state_gdn, cache_gdn_conv, c_prompt, c_sample, w_ada, b_ada, w_in, s5_a_re, s5_a_im, s5_log_dt, s5_b_re, s5_b_im, s5_c_re, s5_c_im, s5_d, w_s5_glu, gdn_conv_w, gdn_a_log, gdn_dt_bias, gdn_norm_g, w_out, ln1_g, ln1_b, w_ffn_up, w_ffn_down, ln2_g, ln2_b):
    bp, bs = x_prompt.shape[0], x_sample.shape[0]
    assert w_ada.shape[0] == DEPTH == 1
    l = 0
    mod = _ada(jnp.concatenate([c_prompt, c_sample], 0), w_ada[l], b_ada[l])
    mod = mod.reshape(bp + bs, 6, 1, D_MODEL)
    w_in_p = jnp.pad(w_in[l], ((0, 0), (0, IN_PAD - w_in.shape[-1]))).astype(BF16)
    s5_mats = _s5_prep(s5_a_re[l], s5_a_im[l], s5_log_dt[l], s5_b_re[l], s5_b_im[l],
                       s5_c_re[l], s5_c_im[l])
    wts = (w_in_p, s5_mats, s5_d[l], w_s5_glu[l].astype(BF16), gdn_conv_w[l], gdn_a_log[l],
           gdn_dt_bias[l], gdn_norm_g[l], w_out[l].astype(BF16), ln1_g[l], ln1_b[l],
           w_ffn_up[l].astype(BF16), w_ffn_down[l].astype(BF16), ln2_g[l], ln2_b[l])
    zeros = lambda *s: jnp.zeros(s, F32)
    yp, p_re, p_im, p_gdn, p_conv = _layer(
        x_prompt, mod[:bp],
        zeros(bp, S5_GROUPS, S5_STATE), zeros(bp, S5_GROUPS, S5_STATE),
        zeros(bp, GDN_HEADS, GDN_HEAD_DIM, GDN_HEAD_DIM), zeros(bp, CONV_WIDTH - 1, CONV_CH),
        wts, bt=1, lt=512, gdn_chunk=256, gdn_bt=2)
    ys, s_re, s_im, s_gdn, s_conv = _layer(
        x_sample, mod[bp:], state_s5_re[l], state_s5_im[l], state_gdn[l], cache_gdn_conv[l],
        wts, bt=bs, lt=x_sample.shape[1], gdn_chunk=x_sample.shape[1], gdn_bt=4)
    st = lambda t: t[None]
    return (yp, ys, st(p_re), st(p_im), st(p_gdn), st(p_conv),
            st(s_re), st(s_im), st(s_gdn), st(s_conv))
```

```python
import functools

import jax
import jax.numpy as jnp
from jax import lax
from jax.experimental import pallas as pl
from jax.experimental.pallas import tpu as pltpu

F32 = jnp.float32
BF16 = jnp.bfloat16

D_MODEL = 1024
S5_WIDTH = 512
S5_GROUP = 16
S5_GROUPS = 32
S5_STATE = 64
GDN_WIDTH = 512
GDN_HEAD_DIM = 128
GDN_HEADS = 4
CONV_WIDTH = 4
CONV_CH = 3 * GDN_WIDTH
D_FF = 2816
GDN_CHUNK = 64
LN_EPS = 1e-5
RMS_EPS = 1e-6
DEPTH = 1
ALPHA = (2 * DEPTH) ** 0.25

LANES = 128
S5_T = 8
S5_BLK_GROUPS = LANES // S5_GROUP
S5_NBLK = S5_GROUPS // S5_BLK_GROUPS
S5_BLK_STATE = S5_BLK_GROUPS * S5_STATE
SUBLANES = 8
S5_MAX_SEG = 64
S5_POW_ROWS = S5_MAX_SEG + SUBLANES
IN_PAD = S5_WIDTH + 4 * GDN_WIDTH + LANES
VMEM_LIMIT = 56 * 1024 * 1024

HI = lax.Precision.HIGHEST


def _cparams(*sem):
    return pltpu.CompilerParams(dimension_semantics=sem, vmem_limit_bytes=VMEM_LIMIT)


def _ln(x):
    mu = jnp.mean(x, -1, keepdims=True)
    xc = x - mu
    var = jnp.mean(xc * xc, -1, keepdims=True)
    return xc * lax.rsqrt(var + LN_EPS)


def _sigmoid(x):
    return 1.0 / (1.0 + jnp.exp(-x))


def _silu(x):
    return x * _sigmoid(x)


def _ada_kernel(c_ref, w_ref, b_ref, o_ref):
    c = c_ref[...]
    o_ref[...] = jnp.dot(_silu(c), w_ref[...], preferred_element_type=F32,
                         precision=HI) + b_ref[...]


def _ada(c, w_ada, b_ada):
    rows = c.shape[0]
    ncol = w_ada.shape[1]
    blk = D_MODEL
    return pl.pallas_call(
        _ada_kernel,
        grid=(ncol // blk,),
        in_specs=[pl.BlockSpec((rows, D_MODEL), lambda j: (0, 0)),
                  pl.BlockSpec((D_MODEL, blk), lambda j: (0, j)),
                  pl.BlockSpec((1, blk), lambda j: (0, j))],
        out_specs=pl.BlockSpec((rows, blk), lambda j: (0, j)),
        out_shape=jax.ShapeDtypeStruct((rows, ncol), F32),
        compiler_params=_cparams("arbitrary"),
        name="ada",
    )(c, w_ada, b_ada.reshape(1, ncol))


def _s5prep_kernel(are_ref, aim_ref, ldt_ref, btre_ref, btim_ref, ctre_ref, ctim_ref,
                   w_ref, p_ref, qt_ref, at_ref):
    T, half = S5_T, S5_BLK_STATE
    a_re = are_ref[0]
    a_im = aim_ref[0]
    dt = jnp.exp(ldt_ref[0])
    den = a_re * a_re + a_im * a_im

    def lam_pow(tau):
        mag = jnp.exp(a_re * dt * tau)
        return mag * jnp.cos(a_im * dt * tau), mag * jnp.sin(a_im * dt * tau)

    l_re, l_im = lam_pow(1.0)
    n_re, n_im = l_re - 1.0, l_im
    f_re = (n_re * a_re + n_im * a_im) / den
    f_im = (n_im * a_re - n_re * a_im) / den
    same_group = (lax.broadcasted_iota(jnp.int32, (LANES, half), 0) // S5_GROUP
                  == lax.broadcasted_iota(jnp.int32, (LANES, half), 1) // S5_STATE)
    bm_re = jnp.where(same_group, btre_ref[0], 0.0)
    bm_im = jnp.where(same_group, btim_ref[0], 0.0)
    cm_re = jnp.where(same_group, ctre_ref[0], 0.0)
    cm_im = jnp.where(same_group, ctim_ref[0], 0.0)
    dot_nt = lambda a, b: lax.dot_general(a, b, (((1,), (1,)), ((), ())), precision=HI,
                                          preferred_element_type=F32)
    blk = lambda i: slice(i * LANES, (i + 1) * LANES)
    for s in range(T):
        for t in range(s):
            w_ref[0, blk(s), blk(t)] = jnp.zeros((LANES, LANES), BF16)
    m_t = lax.broadcasted_iota(jnp.int32, (S5_POW_ROWS, half), 0).astype(F32) * float(T)
    mag = jnp.exp(a_re * dt * m_t)
    at_ref[0, :, :half] = mag * jnp.cos(a_im * dt * m_t)
    at_ref[0, :, half:] = mag * jnp.sin(a_im * dt * m_t)
    for tau in range(T + 1):
        p_re, p_im = (jnp.ones_like(a_re), jnp.zeros_like(a_re)) if tau == 0 else lam_pow(float(tau))
        if tau < T:
            g_re = p_re * f_re - p_im * f_im
            g_im = p_re * f_im + p_im * f_re
            inj_re = g_re * bm_re - g_im * bm_im
            inj_im = g_re * bm_im + g_im * bm_re
            s = T - 1 - tau
            p_ref[0, blk(s), :half] = inj_re.astype(BF16)
            p_ref[0, blk(s), half:] = inj_im.astype(BF16)
            k_tau = (dot_nt(inj_re, cm_re) - dot_nt(inj_im, cm_im)).astype(BF16)
            for s in range(T - tau):
                w_ref[0, blk(s), blk(s + tau)] = k_tau
        if tau >= 1:
            qt_ref[0, blk(tau - 1), :half] = (cm_re * p_re - cm_im * p_im).astype(BF16)
            qt_ref[0, blk(tau - 1), half:] = (-(cm_re * p_im + cm_im * p_re)).astype(BF16)


def _s5_prep(a_re, a_im, log_dt, b_re, b_im, c_re, c_im):
    NB, T, half = S5_NBLK, S5_T, S5_BLK_STATE
    tl, st = T * LANES, 2 * half
    lanes = lambda t: t.reshape(NB, 1, half)
    tiled = lambda t: jnp.tile(t.reshape(NB, LANES, S5_STATE), (1, 1, S5_BLK_GROUPS))
    row = pl.BlockSpec((1, 1, half), lambda k: (k, 0, 0))
    mat = pl.BlockSpec((1, LANES, half), lambda k: (k, 0, 0))
    return pl.pallas_call(
        _s5prep_kernel,
        grid=(NB,),
        in_specs=[row, row, row, mat, mat, mat, mat],
        out_specs=[pl.BlockSpec((1, tl, tl), lambda k: (k, 0, 0)),
                   pl.BlockSpec((1, tl, st), lambda k: (k, 0, 0)),
                   pl.BlockSpec((1, tl, st), lambda k: (k, 0, 0)),
                   pl.BlockSpec((1, S5_POW_ROWS, st), lambda k: (k, 0, 0))],
        out_shape=[jax.ShapeDtypeStruct((NB, tl, tl), BF16),
                   jax.ShapeDtypeStruct((NB, tl, st), BF16),
                   jax.ShapeDtypeStruct((NB, tl, st), BF16),
                   jax.ShapeDtypeStruct((NB, S5_POW_ROWS, st), F32)],
        compiler_params=_cparams("arbitrary"),
        name="s5prep",
    )(lanes(a_re), lanes(a_im), lanes(jnp.repeat(log_dt, S5_STATE)),
      tiled(jnp.swapaxes(b_re, 1, 2)), tiled(jnp.swapaxes(b_im, 1, 2)), tiled(c_re), tiled(c_im))


def _s5_kernel(u_ref, h0_ref, w_ref, p_ref, q_ref, at_ref, d_ref, y_ref, hT_ref,
               lhs_ref, x_ref, hp_ref, *, bt, n):
    T, half = S5_T, S5_BLK_STATE
    for b in range(bt):
        for t in range(T):
            lhs_ref[b * n:(b + 1) * n, t * LANES:(t + 1) * LANES] = (
                u_ref[b, pl.ds(t, n, stride=T), :].astype(BF16))
    lhs = lhs_ref[...]
    nl = half // LANES
    lane = lambda j: slice(j * LANES, (j + 1) * LANES)
    x = jnp.dot(lhs, p_ref[0], preferred_element_type=F32)
    for j in range(2 * nl):
        x_ref[j] = x[:, lane(j)]
    seg = bt * n // SUBLANES
    per_seq = n // seg
    a_re = [at_ref[0, 1:2, lane(j)] for j in range(nl)]
    a_im = [at_ref[0, 1:2, lane(nl + j)] for j in range(nl)]

    def step(m, h):
        rows = pl.ds(m, SUBLANES, stride=seg)
        new = [None] * (2 * nl)
        for j in range(nl):
            h_re, h_im = h[j], h[nl + j]
            hp_ref[j, rows, :] = h_re
            hp_ref[nl + j, rows, :] = h_im
            new[j] = a_re[j] * h_re - a_im[j] * h_im + x_ref[j, rows, :]
            new[nl + j] = a_re[j] * h_im + a_im[j] * h_re + x_ref[nl + j, rows, :]
        return tuple(new)
    end = lax.fori_loop(0, seg, step, tuple(jnp.zeros((SUBLANES, LANES), F32) for _ in range(2 * nl)))
    for j in range(nl):
        pw_re, pw_im = at_ref[0, 0:seg, lane(j)], at_ref[0, 0:seg, lane(nl + j)]
        as_re, as_im = at_ref[0, seg:seg + 1, lane(j)], at_ref[0, seg:seg + 1, lane(nl + j)]
        for s in range(SUBLANES):
            b = s // per_seq
            if s % per_seq == 0:
                c_re, c_im = h0_ref[b, 0, :, lane(j)], h0_ref[b, 0, :, lane(nl + j)]
            rows = slice(s * seg, (s + 1) * seg)
            hp_ref[j, rows, :] = hp_ref[j, rows, :] + (pw_re * c_re - pw_im * c_im)
            hp_ref[nl + j, rows, :] = hp_ref[nl + j, rows, :] + (pw_re * c_im + pw_im * c_re)
            c_re, c_im = (end[j][s:s + 1] + (as_re * c_re - as_im * c_im),
                          end[nl + j][s:s + 1] + (as_re * c_im + as_im * c_re))
            if s % per_seq == per_seq - 1:
                hT_ref[b, 0, :, lane(j)] = c_re
                hT_ref[b, 0, :, lane(nl + j)] = c_im
    hp = jnp.concatenate([hp_ref[j].astype(BF16) for j in range(2 * nl)], axis=1)
    y = (jnp.dot(lhs, w_ref[0], preferred_element_type=F32)
         + lax.dot_general(hp, q_ref[0], (((1,), (1,)), ((), ())), preferred_element_type=F32))
    d = d_ref[...]
    for b in range(bt):
        for t in range(T):
            y_ref[b, pl.ds(t, n, stride=T), :] = (
                y[b * n:(b + 1) * n, t * LANES:(t + 1) * LANES]
                + d * u_ref[b, pl.ds(t, n, stride=T), :])


def _s5(u, h0, w, pmat, qmat, a_t, d_skip, bt):
    B, L, _ = u.shape
    n = L // S5_T
    tl, st = S5_T * LANES, 2 * S5_BLK_STATE
    seg = bt * n // SUBLANES
    assert bt * n % SUBLANES == 0 and n % seg == 0 and seg <= S5_MAX_SEG
    return pl.pallas_call(
        functools.partial(_s5_kernel, bt=bt, n=n),
        grid=(S5_NBLK, B // bt),
        in_specs=[pl.BlockSpec((bt, L, LANES), lambda k, b: (b, 0, k)),
                  pl.BlockSpec((bt, 1, 1, st), lambda k, b: (b, k, 0, 0)),
                  pl.BlockSpec((1, tl, tl), lambda k, b: (k, 0, 0)),
                  pl.BlockSpec((1, tl, st), lambda k, b: (k, 0, 0)),
                  pl.BlockSpec((1, tl, st), lambda k, b: (k, 0, 0)),
                  pl.BlockSpec((1, S5_POW_ROWS, st), lambda k, b: (k, 0, 0)),
                  pl.BlockSpec((1, LANES), lambda k, b: (0, k))],
        out_specs=[pl.BlockSpec((bt, L, LANES), lambda k, b: (b, 0, k)),
                   pl.BlockSpec((bt, 1, 1, st), lambda k, b: (b, k, 0, 0))],
        out_shape=[jax.ShapeDtypeStruct((B, L, S5_WIDTH), F32),
                   jax.ShapeDtypeStruct((B, S5_NBLK, 1, st), F32)],
        scratch_shapes=[pltpu.VMEM((bt * n, tl), BF16),
                        pltpu.VMEM((st // LANES, bt * n, LANES), F32),
                        pltpu.VMEM((st // LANES, bt * n, LANES), F32)],
        compiler_params=_cparams("arbitrary", "arbitrary"),
        name="s5",
    )(u, h0, w, pmat, qmat, a_t, d_skip.reshape(1, S5_WIDTH))


def _s5_state_in(h_re, h_im):
    B = h_re.shape[0]
    r = h_re.reshape(B, S5_NBLK, 1, S5_BLK_STATE)
    i = h_im.reshape(B, S5_NBLK, 1, S5_BLK_STATE)
    return jnp.concatenate([r, i], -1)


def _s5_state_out(h):
    B = h.shape[0]
    h = h.reshape(B, S5_NBLK, 2, S5_BLK_GROUPS, S5_STATE)
    return (h[:, :, 0].reshape(B, S5_GROUPS, S5_STATE), h[:, :, 1].reshape(B, S5_GROUPS, S5_STATE))


CONV_TAIL = SUBLANES


def _inproj_kernel(x_ref, sh_ref, sc_ref, w_ref, cb_ref, cw_ref,
                   u_ref, act_ref, z_ref, ba_ref, cbo_ref, xbuf_ref):
    bt, lt, d = x_ref.shape
    keep = CONV_WIDTH - 1
    o_q, o_z, o_b = S5_WIDTH, S5_WIDTH + CONV_CH, S5_WIDTH + CONV_CH + GDN_WIDTH

    @pl.when(pl.program_id(1) == 0)
    def _():
        xbuf_ref[:, 0:CONV_TAIL, :] = jnp.zeros((bt, CONV_TAIL, CONV_CH), F32)
        xbuf_ref[:, CONV_TAIL - keep:CONV_TAIL, :] = cb_ref[...]

    h = _ln(x_ref[...]) * (1.0 + sc_ref[:, 0]) + sh_ref[:, 0]
    h = h.reshape(bt * lt, d).astype(BF16)
    xbuf_ref[:, CONV_TAIL:CONV_TAIL + lt, :] = _bdot(h, w_ref[:, o_q:o_z]).reshape(bt, lt, CONV_CH)
    u_ref[...] = _bdot(h, w_ref[:, :o_q]).reshape(bt, lt, S5_WIDTH)
    z_ref[...] = _bdot(h, w_ref[:, o_z:o_b]).reshape(bt, lt, GDN_WIDTH)
    ba_ref[...] = _bdot(h, w_ref[:, o_b:]).reshape(bt, lt, LANES)
    conv = None
    for j in range(CONV_WIDTH):
        off = CONV_TAIL - keep + j
        term = xbuf_ref[:, off:off + lt, :] * cw_ref[j:j + 1, :]
        conv = term if conv is None else conv + term
    act_ref[...] = _silu(conv)
    cbo_ref[...] = xbuf_ref[:, lt + CONV_TAIL - keep:lt + CONV_TAIL, :]
    xbuf_ref[:, 0:CONV_TAIL, :] = xbuf_ref[:, lt:lt + CONV_TAIL, :]


def _inproj(x, mod, w_in_p, conv_buf, conv_w, bt, lt):
    B, L, _ = x.shape
    xs = lambda w: pl.BlockSpec((bt, lt, w), lambda b, i: (b, i, 0))
    ms = lambda idx: pl.BlockSpec((bt, 1, 1, D_MODEL), lambda b, i: (b, idx, 0, 0))
    cache = pl.BlockSpec((bt, CONV_WIDTH - 1, CONV_CH), lambda b, i: (b, 0, 0))
    return pl.pallas_call(
        _inproj_kernel,
        grid=(B // bt, L // lt),
        in_specs=[xs(D_MODEL), ms(0), ms(1),
                  pl.BlockSpec((D_MODEL, IN_PAD), lambda b, i: (0, 0)), cache,
                  pl.BlockSpec((CONV_WIDTH, CONV_CH), lambda b, i: (0, 0))],
        out_specs=[xs(S5_WIDTH), xs(CONV_CH), xs(GDN_WIDTH), xs(LANES), cache],
        out_shape=[jax.ShapeDtypeStruct((B, L, w), F32) for w in (S5_WIDTH, CONV_CH, GDN_WIDTH, LANES)]
                  + [jax.ShapeDtypeStruct((B, CONV_WIDTH - 1, CONV_CH), F32)],
        scratch_shapes=[pltpu.VMEM((bt, lt + CONV_TAIL, CONV_CH), F32)],
        compiler_params=_cparams("arbitrary", "arbitrary"),
        name="inproj",
    )(x, mod, mod, w_in_p, conv_buf, conv_w)


def _mm(a, b):
    return jnp.dot(a.astype(BF16), b.astype(BF16), preferred_element_type=F32)


def _mm_nt(a, b):
    return lax.dot_general(a.astype(BF16), b.astype(BF16), (((1,), (1,)), ((), ())),
                           preferred_element_type=F32)


def _mm_tn(a, b):
    return lax.dot_general(a.astype(BF16), b.astype(BF16), (((0,), (0,)), ((), ())),
                           preferred_element_type=F32)


def _mm_hi(a, b):
    return jnp.dot(a, b, preferred_element_type=F32, precision=HI)


GDN_INV_BASE = 2
GDN_INV_JOIN = 4


def _bdot(a, b):
    return jnp.dot(a, b, preferred_element_type=F32)


def _unit_lower_inverse(a_list, ri, ci, size):
    eye_f = (ri == ci).astype(F32)
    blk = lambda s: (ri // s) == (ci // s)
    prev = min(GDN_INV_BASE, size)
    base_mask = blk(prev)
    d_list = [(eye_f - jnp.where(base_mask, a, 0.0)).astype(BF16) for a in a_list]
    while prev < size:
        cur = min(prev * GDN_INV_JOIN, size)
        off_mask = blk(cur) & jnp.logical_not(blk(prev))
        ms = [_bdot(d, jnp.where(off_mask, a, 0.0).astype(BF16)) for a, d in zip(a_list, d_list)]
        rs = [eye_f - m for m in ms]
        mps = [m.astype(BF16) for m in ms]
        span = 2
        while span < cur // prev:
            mps = [_bdot(mp, mp).astype(BF16) for mp in mps]
            rs = [r + _bdot(r.astype(BF16), mp) for r, mp in zip(rs, mps)]
            span *= 2
        d_list = [_bdot(r.astype(BF16), d).astype(BF16) for r, d in zip(rs, d_list)]
        prev = cur
    return d_list


def _gdn_kernel(act_ref, z_ref, ba_ref, s0_ref, alog_ref, dtb_ref, ng_ref,
                y_ref, sT_ref, s_ref, *, chunk, bt):
    C, Dh, H = chunk, GDN_HEAD_DIM, GDN_HEADS

    @pl.when(pl.program_id(1) == 0)
    def _():
        s_ref[...] = s0_ref[...]

    act_all = act_ref[...]
    ba = ba_ref[...]
    beta_all = _sigmoid(ba)
    xa = ba + dtb_ref[...]
    softplus = jnp.maximum(xa, 0.0) + jnp.log1p(jnp.exp(-jnp.abs(xa)))
    g = -jnp.exp(alog_ref[...]) * softplus

    ri = lax.broadcasted_iota(jnp.int32, (C, C), 0)
    ci = lax.broadcasted_iota(jnp.int32, (C, C), 1)
    causal = ri >= ci
    strict = ri > ci
    eye = ri == ci
    ng = ng_ref[...]
    tril = causal.astype(F32)
    decay_all = [_mm_hi(tril, g[b]) for b in range(bt)]
    units = [(b, h) for b in range(bt) for h in range(H)]

    e_list, kbq_list, kT_list, rhs_list, qd_list, kdT_list, gl_list = [], [], [], [], [], [], []
    for b, h in units:
        decay, beta, act = decay_all[b], beta_all[b], act_all[b]
        dcol = decay[:, H + h:H + h + 1]
        dmat = jnp.broadcast_to(dcol, (C, C))
        drow = jnp.sum(jnp.where(eye, dmat, 0.0), axis=0, keepdims=True)
        e = jnp.exp(dmat - drow)
        bcol = beta[:, h:h + 1]
        q = act[:, h * Dh:(h + 1) * Dh]
        k = act[:, GDN_WIDTH + h * Dh:GDN_WIDTH + (h + 1) * Dh]
        v = act[:, 2 * GDN_WIDTH + h * Dh:2 * GDN_WIDTH + (h + 1) * Dh]
        q = q * (lax.rsqrt(jnp.sum(q * q, -1, keepdims=True) + RMS_EPS) * (Dh ** -0.5))
        k = k * lax.rsqrt(jnp.sum(k * k, -1, keepdims=True) + RMS_EPS)
        kb = k * bcol
        kT = k.T
        kT_b = kT.astype(BF16)
        edec = jnp.exp(dcol)
        dlast = decay[C - 1:C, H + h:H + h + 1]
        e_list.append(e)
        kbq_list.append(jnp.concatenate([kb, q], axis=0).astype(BF16))
        kT_list.append(kT_b)
        rhs_list.append(jnp.concatenate([v * bcol, kb * edec], axis=1).astype(BF16))
        qd_list.append((q * edec).astype(BF16))
        kdT_list.append((kT * jnp.exp(dlast - drow)).astype(BF16))
        gl_list.append(jnp.exp(dlast))
    sc_list = [_bdot(kbq, kT_b) for kbq, kT_b in zip(kbq_list, kT_list)]
    a_list = [jnp.where(strict, sc[:C] * e, 0.0) for sc, e in zip(sc_list, e_list)]
    qk_list = [jnp.where(causal, sc[C:] * e, 0.0).astype(BF16) for sc, e in zip(sc_list, e_list)]

    tinv_list = _unit_lower_inverse(a_list, ri, ci, C)

    uw_list = [_bdot(t, rhs) for t, rhs in zip(tinv_list, rhs_list)]
    s_list = [s_ref[b, h] for b, h in units]
    ws_list = [_bdot(jnp.concatenate([uw[:, Dh:].astype(BF16), qd], axis=0), s.astype(BF16))
               for uw, qd, s in zip(uw_list, qd_list, s_list)]
    vn_list = [(uw[:, :Dh] - ws[:C]).astype(BF16) for uw, ws in zip(uw_list, ws_list)]
    o_list = [ws[C:] + _bdot(qk, vn) for ws, qk, vn in zip(ws_list, qk_list, vn_list)]
    for i, (b, h) in enumerate(units):
        s_ref[b, h] = s_list[i] * gl_list[i] + _bdot(kdT_list[i], vn_list[i])
    for i, (b, h) in enumerate(units):
        o = o_list[i]
        o = o * lax.rsqrt(jnp.mean(o * o, -1, keepdims=True) + RMS_EPS) * ng
        y_ref[b, :, h * Dh:(h + 1) * Dh] = o * _silu(z_ref[b, :, h * Dh:(h + 1) * Dh])

    sT_ref[...] = s_ref[...]


def _gdn(act, z, ba, s0, a_log, dt_bias, norm_g, chunk, bt):
    B, L, _ = act.shape
    H, Dh = GDN_HEADS, GDN_HEAD_DIM
    lt = chunk
    place = lambda t: jnp.zeros((1, LANES), F32).at[0, H:2 * H].set(t)
    xs = lambda w: pl.BlockSpec((bt, lt, w), lambda b, i: (b, i, 0))
    full = lambda *shape: pl.BlockSpec(shape, lambda b, i: (0,) * len(shape))
    state = pl.BlockSpec((bt, H, Dh, Dh), lambda b, i: (b, 0, 0, 0))
    return pl.pallas_call(
        functools.partial(_gdn_kernel, chunk=chunk, bt=bt),
        grid=(B // bt, L // lt),
        in_specs=[xs(CONV_CH), xs(GDN_WIDTH), xs(LANES), state,
                  full(1, LANES), full(1, LANES), full(1, Dh)],
        out_specs=[xs(GDN_WIDTH), state],
        out_shape=[jax.ShapeDtypeStruct((B, L, GDN_WIDTH), F32),
                   jax.ShapeDtypeStruct((B, H, Dh, Dh), F32)],
        scratch_shapes=[pltpu.VMEM((bt, H, Dh, Dh), F32)],
        compiler_params=_cparams("arbitrary", "arbitrary"),
        name="gdn",
    )(act, z, ba, s0, place(a_log), place(dt_bias), norm_g.reshape(1, Dh))


FF_BLK = 256


def _gelu_tanh(x):
    return 0.5 * x * (1.0 + jnp.tanh(0.7978845608028654 * (x + 0.044715 * x * x * x)))


def _outffn_kernel(x_ref, ya_ref, yb_ref, g1_ref, sh2_ref, sc2_ref, g2_ref,
                   wglu_ref, wout_ref, l1g_ref, l1b_ref, wup_ref, wdn_ref, l2g_ref, l2b_ref,
                   o_ref):
    bt, lt, d = x_ref.shape
    if bt % 2 == 0:
        subs = [(slice(i * bt // 2, (i + 1) * bt // 2), slice(0, lt)) for i in range(2)]
    else:
        subs = [(slice(0, bt), slice(i * lt // 2, (i + 1) * lt // 2)) for i in range(2)]
    rd = lambda ref, sub: ref[sub[0], sub[1], :]
    flat = lambda t: t.reshape(t.shape[0] * t.shape[1], t.shape[-1])
    unflat = lambda t, like: t.reshape(like.shape[0], like.shape[1], t.shape[-1])
    mod = lambda ref, sub: ref[sub[0], 0]
    nsub = range(len(subs))

    zz = [_mm(_gelu_tanh(flat(rd(ya_ref, s))), wglu_ref[...]) for s in subs]
    y_a = [z[:, :S5_WIDTH] * _sigmoid(z[:, S5_WIDTH:]) for z in zz]
    mix_b = [_mm(flat(rd(yb_ref, s)), wout_ref[S5_WIDTH:, :]) for s in subs]
    mix = [mb + _mm(ya, wout_ref[:S5_WIDTH, :]) for mb, ya in zip(mix_b, y_a)]
    xs = [rd(x_ref, s) for s in subs]
    x1 = [_ln(ALPHA * x + (1.0 + mod(g1_ref, s)) * unflat(mx, x)) * l1g_ref[...] + l1b_ref[...]
          for x, mx, s in zip(xs, mix, subs)]
    h = [flat(_ln(x) * (1.0 + mod(sc2_ref, s)) + mod(sh2_ref, s)).astype(BF16) for x, s in zip(x1, subs)]

    def gate_up(i, j):
        gate = _bdot(h[i], wup_ref[:, j * FF_BLK:(j + 1) * FF_BLK])
        up = _bdot(h[i], wup_ref[:, D_FF + j * FF_BLK:D_FF + (j + 1) * FF_BLK])
        return gate, up

    nblk = D_FF // FF_BLK
    acc = [None for _ in nsub]
    gu = [gate_up(i, 0) for i in nsub]
    for j in range(nblk):
        gu_next = [gate_up(i, j + 1) for i in nsub] if j + 1 < nblk else None
        for i in nsub:
            gate, up = gu[i]
            part = _mm(_silu(gate) * up, wdn_ref[j * FF_BLK:(j + 1) * FF_BLK, :])
            acc[i] = part if acc[i] is None else acc[i] + part
        gu = gu_next
    for i, s in enumerate(subs):
        o_ref[s[0], s[1], :] = (_ln(ALPHA * x1[i] + (1.0 + mod(g2_ref, s)) * unflat(acc[i], x1[i]))
                                * l2g_ref[...] + l2b_ref[...])


def _outffn(x, ya, yb, mod, w_glu, w_out, ln1_g, ln1_b, w_up, w_dn, ln2_g, ln2_b, bt, lt):
    B, L, _ = x.shape
    xs = lambda w: pl.BlockSpec((bt, lt, w), lambda b, i: (b, i, 0))
    ms = lambda idx: pl.BlockSpec((bt, 1, 1, D_MODEL), lambda b, i: (b, idx, 0, 0))
    const = lambda a: pl.BlockSpec(a.shape, lambda b, i: (0,) * a.ndim, pipeline_mode=pl.Buffered(1))
    vec = lambda t: t.reshape(1, D_MODEL)
    weights = (w_glu, w_out, vec(ln1_g), vec(ln1_b), w_up, w_dn, vec(ln2_g), vec(ln2_b))
    return pl.pallas_call(
        _outffn_kernel,
        grid=(B // bt, L // lt),
        in_specs=[xs(D_MODEL), xs(S5_WIDTH), xs(GDN_WIDTH), ms(2), ms(3), ms(4), ms(5)]
                 + [const(a) for a in weights],
        out_specs=xs(D_MODEL),
        out_shape=jax.ShapeDtypeStruct((B, L, D_MODEL), F32),
        compiler_params=_cparams("arbitrary", "arbitrary"),
        name="outffn",
    )(x, ya, yb, mod, mod, mod, mod, *weights)


def _layer(x, mod, h_re, h_im, s_gdn, conv_buf, wts, bt, lt, gdn_chunk, gdn_bt):
    (w_in_p, s5_mats, s5_d, w_glu, conv_w, a_log, dt_bias, norm_g, w_out,
     ln1_g, ln1_b, w_up, w_dn, ln2_g, ln2_b) = wts
    u, act, z, ba, cbo = _inproj(x, mod, w_in_p, conv_buf, conv_w, bt, lt)
    w, pmat, qmat, a_t = s5_mats
    ys5, hT = _s5(u, _s5_state_in(h_re, h_im), w, pmat, qmat, a_t, s5_d, bt)
    yb, sT = _gdn(act, z, ba, s_gdn, a_log, dt_bias, norm_g, gdn_chunk, gdn_bt)
    y = _outffn(x, ys5, yb, mod, w_glu, w_out, ln1_g, ln1_b, w_up, w_dn, ln2_g, ln2_b, bt, lt)
    o_re, o_im = _s5_state_out(hT)
    return y, o_re, o_im, sT, cbo


def kernel(x_prompt, x_sample, state_s5_re, state_s5_im, state_gdn, cache_gdn_conv, c_prompt, c_sample, w_ada, b_ada, w_in, s5_a_re, s5_a_im, s5_log_dt, s5_b_re, s5_b_im, s5_c_re, s5_c_im, s5_d, w_s5_glu, gdn_conv_w, gdn_a_log, gdn_dt_bias, gdn_norm_g, w_out, ln1_g, ln1_b, w_ffn_up, w_ffn_down, ln2_g, ln2_b):
    bp, bs = x_prompt.shape[0], x_sample.shape[0]
    assert w_ada.shape[0] == DEPTH == 1
    l = 0
    mod = _ada(jnp.concatenate([c_prompt, c_sample], 0), w_ada[l], b_ada[l])
    mod = mod.reshape(bp + bs, 6, 1, D_MODEL)
    w_in_p = jnp.pad(w_in[l], ((0, 0), (0, IN_PAD - w_in.shape[-1]))).astype(BF16)
    s5_mats = _s5_prep(s5_a_re[l], s5_a_im[l], s5_log_dt[l], s5_b_re[l], s5_b_im[l],
                       s5_c_re[l], s5_c_im[l])
    wts = (w_in_p, s5_mats, s5_d[l], w_s5_glu[l].astype(BF16), gdn_conv_w[l], gdn_a_log[l],
           gdn_dt_bias[l], gdn_norm_g[l], w_out[l].astype(BF16), ln1_g[l], ln1_b[l],
           w_ffn_up[l].astype(BF16), w_ffn_down[l].astype(BF16), ln2_g[l], ln2_b[l])
    zeros = lambda *s: jnp.zeros(s, F32)
    yp, p_re, p_im, p_gdn, p_conv = _layer(
        x_prompt, mod[:bp],
        zeros(bp, S5_GROUPS, S5_STATE), zeros(bp, S5_GROUPS, S5_STATE),
        zeros(bp, GDN_HEADS, GDN_HEAD_DIM, GDN_HEAD_DIM), zeros(bp, CONV_WIDTH - 1, CONV_CH),
        wts, bt=1, lt=512, gdn_chunk=256, gdn_bt=2)
    ys, s_re, s_im, s_gdn, s_conv = _layer(
        x_sample, mod[bp:], state_s5_re[l], state_s5_im[l], state_gdn[l], cache_gdn_conv[l],
        wts, bt=bs, lt=x_sample.shape[1], gdn_chunk=x_sample.shape[1], gdn_bt=4)
    st = lambda t: t[None]
    return (yp, ys, st(p_re), st(p_im), st(p_gdn), st(p_conv),
            st(s_re), st(s_im), st(s_gdn), st(s_conv))
```

```python
import functools

import jax
import jax.numpy as jnp
from jax import lax
from jax.experimental import pallas as pl
from jax.experimental.pallas import tpu as pltpu

F32 = jnp.float32
BF16 = jnp.bfloat16

D_MODEL = 1024
S5_WIDTH = 512
S5_GROUP = 16
S5_GROUPS = 32
S5_STATE = 64
GDN_WIDTH = 512
GDN_HEAD_DIM = 128
GDN_HEADS = 4
CONV_WIDTH = 4
CONV_CH = 3 * GDN_WIDTH
D_FF = 2816
GDN_CHUNK = 64
LN_EPS = 1e-5
RMS_EPS = 1e-6
DEPTH = 1
ALPHA = (2 * DEPTH) ** 0.25

LANES = 128
S5_T = 8
S5_BLK_GROUPS = LANES // S5_GROUP
S5_NBLK = S5_GROUPS // S5_BLK_GROUPS
S5_BLK_STATE = S5_BLK_GROUPS * S5_STATE
SUBLANES = 8
S5_POW_ROWS = 2 * SUBLANES
IN_PAD = S5_WIDTH + 4 * GDN_WIDTH + LANES
VMEM_LIMIT = 56 * 1024 * 1024

HI = lax.Precision.HIGHEST


def _cparams(*sem):
    return pltpu.CompilerParams(dimension_semantics=sem, vmem_limit_bytes=VMEM_LIMIT)


def _ln(x):
    mu = jnp.mean(x, -1, keepdims=True)
    xc = x - mu
    var = jnp.mean(xc * xc, -1, keepdims=True)
    return xc * lax.rsqrt(var + LN_EPS)


def _sigmoid(x):
    return 1.0 / (1.0 + jnp.exp(-x))


def _silu(x):
    return x * _sigmoid(x)


def _ada_kernel(c_ref, w_ref, b_ref, o_ref):
    c = c_ref[...]
    o_ref[...] = jnp.dot(_silu(c), w_ref[...], preferred_element_type=F32,
                         precision=HI) + b_ref[...]


def _ada(c, w_ada, b_ada):
    rows = c.shape[0]
    ncol = w_ada.shape[1]
    blk = D_MODEL
    return pl.pallas_call(
        _ada_kernel,
        grid=(ncol // blk,),
        in_specs=[pl.BlockSpec((rows, D_MODEL), lambda j: (0, 0)),
                  pl.BlockSpec((D_MODEL, blk), lambda j: (0, j)),
                  pl.BlockSpec((1, blk), lambda j: (0, j))],
        out_specs=pl.BlockSpec((rows, blk), lambda j: (0, j)),
        out_shape=jax.ShapeDtypeStruct((rows, ncol), F32),
        compiler_params=_cparams("arbitrary"),
        name="ada",
    )(c, w_ada, b_ada.reshape(1, ncol))


def _s5prep_kernel(are_ref, aim_ref, ldt_ref, btre_ref, btim_ref, ctre_ref, ctim_ref,
                   w_ref, p_ref, qt_ref, at_ref):
    T, half = S5_T, S5_BLK_STATE
    a_re = are_ref[0]
    a_im = aim_ref[0]
    dt = jnp.exp(ldt_ref[0])
    den = a_re * a_re + a_im * a_im

    def lam_pow(tau):
        mag = jnp.exp(a_re * dt * tau)
        return mag * jnp.cos(a_im * dt * tau), mag * jnp.sin(a_im * dt * tau)

    l_re, l_im = lam_pow(1.0)
    n_re, n_im = l_re - 1.0, l_im
    f_re = (n_re * a_re + n_im * a_im) / den
    f_im = (n_im * a_re - n_re * a_im) / den
    same_group = (lax.broadcasted_iota(jnp.int32, (LANES, half), 0) // S5_GROUP
                  == lax.broadcasted_iota(jnp.int32, (LANES, half), 1) // S5_STATE)
    bm_re = jnp.where(same_group, btre_ref[0], 0.0)
    bm_im = jnp.where(same_group, btim_ref[0], 0.0)
    cm_re = jnp.where(same_group, ctre_ref[0], 0.0)
    cm_im = jnp.where(same_group, ctim_ref[0], 0.0)
    dot_nt = lambda a, b: lax.dot_general(a, b, (((1,), (1,)), ((), ())), precision=HI,
                                          preferred_element_type=F32)
    blk = lambda i: slice(i * LANES, (i + 1) * LANES)
    for s in range(T):
        for t in range(s):
            w_ref[0, blk(s), blk(t)] = jnp.zeros((LANES, LANES), BF16)
    m_t = lax.broadcasted_iota(jnp.int32, (S5_POW_ROWS, half), 0).astype(F32) * float(T)
    mag = jnp.exp(a_re * dt * m_t)
    at_ref[0, :, :half] = mag * jnp.cos(a_im * dt * m_t)
    at_ref[0, :, half:] = mag * jnp.sin(a_im * dt * m_t)
    for tau in range(T + 1):
        p_re, p_im = (jnp.ones_like(a_re), jnp.zeros_like(a_re)) if tau == 0 else lam_pow(float(tau))
        if tau < T:
            g_re = p_re * f_re - p_im * f_im
            g_im = p_re * f_im + p_im * f_re
            inj_re = g_re * bm_re - g_im * bm_im
            inj_im = g_re * bm_im + g_im * bm_re
            s = T - 1 - tau
            p_ref[0, blk(s), :half] = inj_re.astype(BF16)
            p_ref[0, blk(s), half:] = inj_im.astype(BF16)
            k_tau = (dot_nt(inj_re, cm_re) - dot_nt(inj_im, cm_im)).astype(BF16)
            for s in range(T - tau):
                w_ref[0, blk(s), blk(s + tau)] = k_tau
        if tau >= 1:
            qt_ref[0, blk(tau - 1), :half] = (cm_re * p_re - cm_im * p_im).astype(BF16)
            qt_ref[0, blk(tau - 1), half:] = (-(cm_re * p_im + cm_im * p_re)).astype(BF16)


def _s5_prep(a_re, a_im, log_dt, b_re, b_im, c_re, c_im):
    NB, T, half = S5_NBLK, S5_T, S5_BLK_STATE
    tl, st = T * LANES, 2 * half
    lanes = lambda t: t.reshape(NB, 1, half)
    tiled = lambda t: jnp.tile(t.reshape(NB, LANES, S5_STATE), (1, 1, S5_BLK_GROUPS))
    row = pl.BlockSpec((1, 1, half), lambda k: (k, 0, 0))
    mat = pl.BlockSpec((1, LANES, half), lambda k: (k, 0, 0))
    return pl.pallas_call(
        _s5prep_kernel,
        grid=(NB,),
        in_specs=[row, row, row, mat, mat, mat, mat],
        out_specs=[pl.BlockSpec((1, tl, tl), lambda k: (k, 0, 0)),
                   pl.BlockSpec((1, tl, st), lambda k: (k, 0, 0)),
                   pl.BlockSpec((1, tl, st), lambda k: (k, 0, 0)),
                   pl.BlockSpec((1, S5_POW_ROWS, st), lambda k: (k, 0, 0))],
        out_shape=[jax.ShapeDtypeStruct((NB, tl, tl), BF16),
                   jax.ShapeDtypeStruct((NB, tl, st), BF16),
                   jax.ShapeDtypeStruct((NB, tl, st), BF16),
                   jax.ShapeDtypeStruct((NB, S5_POW_ROWS, st), F32)],
        compiler_params=_cparams("arbitrary"),
        name="s5prep",
    )(lanes(a_re), lanes(a_im), lanes(jnp.repeat(log_dt, S5_STATE)),
      tiled(jnp.swapaxes(b_re, 1, 2)), tiled(jnp.swapaxes(b_im, 1, 2)), tiled(c_re), tiled(c_im))


def _s5_kernel(u_ref, h0_ref, w_ref, p_ref, q_ref, at_ref, d_ref, y_ref, hT_ref,
               lhs_ref, x_ref, hp_ref, *, bt, n):
    T, half = S5_T, S5_BLK_STATE
    for b in range(bt):
        for t in range(T):
            lhs_ref[b * n:(b + 1) * n, t * LANES:(t + 1) * LANES] = (
                u_ref[b, pl.ds(t, n, stride=T), :].astype(BF16))
    lhs = lhs_ref[...]
    x_ref[...] = jnp.dot(lhs, p_ref[0], preferred_element_type=F32)
    cplx = lambda m: (at_ref[0, m:m + 1, :half], at_ref[0, m:m + 1, half:])
    rowi = lax.broadcasted_iota(jnp.int32, (SUBLANES, half), 0)
    shifts = []
    d = 1
    while d < SUBLANES:
        p_re, p_im = cplx(d)
        shifts.append((d, jnp.where(rowi >= d, p_re, 0.0), jnp.where(rowi >= d, p_im, 0.0)))
        d *= 2
    row_re = at_ref[0, 0:SUBLANES, :half]
    row_im = at_ref[0, 0:SUBLANES, half:]
    full_re, full_im = cplx(SUBLANES)
    for b in range(bt):
        def tile(i, c):
            c_re, c_im = c
            rows = pl.ds(pl.multiple_of(b * n + i * SUBLANES, SUBLANES), SUBLANES)
            y_re = x_ref[rows, :half]
            y_im = x_ref[rows, half:]
            for d, m_re, m_im in shifts:
                s_re = pltpu.roll(y_re, d, axis=0)
                s_im = pltpu.roll(y_im, d, axis=0)
                y_re, y_im = y_re + (m_re * s_re - m_im * s_im), y_im + (m_re * s_im + m_im * s_re)
            e_re = jnp.where(rowi == 0, 0.0, pltpu.roll(y_re, 1, axis=0))
            e_im = jnp.where(rowi == 0, 0.0, pltpu.roll(y_im, 1, axis=0))
            hp_ref[rows, :half] = e_re + (row_re * c_re - row_im * c_im)
            hp_ref[rows, half:] = e_im + (row_re * c_im + row_im * c_re)
            last = SUBLANES - 1
            return (y_re[last:] + (full_re * c_re - full_im * c_im),
                    y_im[last:] + (full_re * c_im + full_im * c_re))
        c_re, c_im = lax.fori_loop(0, n // SUBLANES, tile,
                                   (h0_ref[b, 0, :, :half], h0_ref[b, 0, :, half:]))
        hT_ref[b, 0, :, :half] = c_re
        hT_ref[b, 0, :, half:] = c_im
    y = (jnp.dot(lhs, w_ref[0], preferred_element_type=F32)
         + lax.dot_general(hp_ref[...].astype(BF16), q_ref[0], (((1,), (1,)), ((), ())),
                           preferred_element_type=F32))
    d = d_ref[...]
    for b in range(bt):
        for t in range(T):
            y_ref[b, pl.ds(t, n, stride=T), :] = (
                y[b * n:(b + 1) * n, t * LANES:(t + 1) * LANES]
                + d * u_ref[b, pl.ds(t, n, stride=T), :])


def _s5(u, h0, w, pmat, qmat, a_t, d_skip, bt):
    B, L, _ = u.shape
    n = L // S5_T
    tl, st = S5_T * LANES, 2 * S5_BLK_STATE
    assert n % SUBLANES == 0
    return pl.pallas_call(
        functools.partial(_s5_kernel, bt=bt, n=n),
        grid=(S5_NBLK, B // bt),
        in_specs=[pl.BlockSpec((bt, L, LANES), lambda k, b: (b, 0, k)),
                  pl.BlockSpec((bt, 1, 1, st), lambda k, b: (b, k, 0, 0)),
                  pl.BlockSpec((1, tl, tl), lambda k, b: (k, 0, 0)),
                  pl.BlockSpec((1, tl, st), lambda k, b: (k, 0, 0)),
                  pl.BlockSpec((1, tl, st), lambda k, b: (k, 0, 0)),
                  pl.BlockSpec((1, S5_POW_ROWS, st), lambda k, b: (k, 0, 0)),
                  pl.BlockSpec((1, LANES), lambda k, b: (0, k))],
        out_specs=[pl.BlockSpec((bt, L, LANES), lambda k, b: (b, 0, k)),
                   pl.BlockSpec((bt, 1, 1, st), lambda k, b: (b, k, 0, 0))],
        out_shape=[jax.ShapeDtypeStruct((B, L, S5_WIDTH), F32),
                   jax.ShapeDtypeStruct((B, S5_NBLK, 1, st), F32)],
        scratch_shapes=[pltpu.VMEM((bt * n, tl), BF16),
                        pltpu.VMEM((bt * n, st), F32),
                        pltpu.VMEM((bt * n, st), F32)],
        compiler_params=_cparams("arbitrary", "arbitrary"),
        name="s5",
    )(u, h0, w, pmat, qmat, a_t, d_skip.reshape(1, S5_WIDTH))


def _s5_state_in(h_re, h_im):
    B = h_re.shape[0]
    r = h_re.reshape(B, S5_NBLK, 1, S5_BLK_STATE)
    i = h_im.reshape(B, S5_NBLK, 1, S5_BLK_STATE)
    return jnp.concatenate([r, i], -1)


def _s5_state_out(h):
    B = h.shape[0]
    h = h.reshape(B, S5_NBLK, 2, S5_BLK_GROUPS, S5_STATE)
    return (h[:, :, 0].reshape(B, S5_GROUPS, S5_STATE), h[:, :, 1].reshape(B, S5_GROUPS, S5_STATE))


CONV_TAIL = SUBLANES


def _inproj_kernel(x_ref, sh_ref, sc_ref, w_ref, cb_ref, cw_ref,
                   u_ref, act_ref, z_ref, ba_ref, cbo_ref, xbuf_ref):
    bt, lt, d = x_ref.shape
    keep = CONV_WIDTH - 1
    o_q, o_z, o_b = S5_WIDTH, S5_WIDTH + CONV_CH, S5_WIDTH + CONV_CH + GDN_WIDTH

    @pl.when(pl.program_id(1) == 0)
    def _():
        xbuf_ref[:, 0:CONV_TAIL, :] = jnp.zeros((bt, CONV_TAIL, CONV_CH), F32)
        xbuf_ref[:, CONV_TAIL - keep:CONV_TAIL, :] = cb_ref[...]

    h = _ln(x_ref[...]) * (1.0 + sc_ref[:, 0]) + sh_ref[:, 0]
    h = h.reshape(bt * lt, d).astype(BF16)
    parts = [slice(p * GDN_WIDTH, (p + 1) * GDN_WIDTH) for p in range(CONV_CH // GDN_WIDTH)]

    def project(cs):
        xbuf_ref[:, CONV_TAIL:CONV_TAIL + lt, cs] = (
            _bdot(h, w_ref[:, o_q + cs.start:o_q + cs.stop]).reshape(bt, lt, GDN_WIDTH))

    def conv_silu(cs):
        conv = None
        for j in range(CONV_WIDTH):
            off = CONV_TAIL - keep + j
            term = xbuf_ref[:, off:off + lt, cs] * cw_ref[j:j + 1, cs]
            conv = term if conv is None else conv + term
        act_ref[:, :, cs] = _silu(conv)

    project(parts[0])
    project(parts[1])
    conv_silu(parts[0])
    project(parts[2])
    conv_silu(parts[1])
    u_ref[...] = _bdot(h, w_ref[:, :o_q]).reshape(bt, lt, S5_WIDTH)
    conv_silu(parts[2])
    z_ref[...] = _bdot(h, w_ref[:, o_z:o_b]).reshape(bt, lt, GDN_WIDTH)
    ba_ref[...] = _bdot(h, w_ref[:, o_b:]).reshape(bt, lt, LANES)
    cbo_ref[...] = xbuf_ref[:, lt + CONV_TAIL - keep:lt + CONV_TAIL, :]
    xbuf_ref[:, 0:CONV_TAIL, :] = xbuf_ref[:, lt:lt + CONV_TAIL, :]


def _inproj(x, mod, w_in_p, conv_buf, conv_w, bt, lt):
    B, L, _ = x.shape
    xs = lambda w: pl.BlockSpec((bt, lt, w), lambda b, i: (b, i, 0))
    ms = lambda idx: pl.BlockSpec((bt, 1, 1, D_MODEL), lambda b, i: (b, idx, 0, 0))
    cache = pl.BlockSpec((bt, CONV_WIDTH - 1, CONV_CH), lambda b, i: (b, 0, 0))
    return pl.pallas_call(
        _inproj_kernel,
        grid=(B // bt, L // lt),
        in_specs=[xs(D_MODEL), ms(0), ms(1),
                  pl.BlockSpec((D_MODEL, IN_PAD), lambda b, i: (0, 0)), cache,
                  pl.BlockSpec((CONV_WIDTH, CONV_CH), lambda b, i: (0, 0))],
        out_specs=[xs(S5_WIDTH), xs(CONV_CH), xs(GDN_WIDTH), xs(LANES), cache],
        out_shape=[jax.ShapeDtypeStruct((B, L, w), F32) for w in (S5_WIDTH, CONV_CH, GDN_WIDTH, LANES)]
                  + [jax.ShapeDtypeStruct((B, CONV_WIDTH - 1, CONV_CH), F32)],
        scratch_shapes=[pltpu.VMEM((bt, lt + CONV_TAIL, CONV_CH), F32)],
        compiler_params=_cparams("arbitrary", "arbitrary"),
        name="inproj",
    )(x, mod, mod, w_in_p, conv_buf, conv_w)


def _mm(a, b):
    return jnp.dot(a.astype(BF16), b.astype(BF16), preferred_element_type=F32)


def _mm_nt(a, b):
    return lax.dot_general(a.astype(BF16), b.astype(BF16), (((1,), (1,)), ((), ())),
                           preferred_element_type=F32)


def _mm_tn(a, b):
    return lax.dot_general(a.astype(BF16), b.astype(BF16), (((0,), (0,)), ((), ())),
                           preferred_element_type=F32)


def _mm_hi(a, b):
    return jnp.dot(a, b, preferred_element_type=F32, precision=HI)


GDN_INV_BASE = 2
GDN_INV_JOIN = 4


def _bdot(a, b):
    return jnp.dot(a, b, preferred_element_type=F32)


def _unit_lower_inverse(a_list, ri, ci, size):
    eye_f = (ri == ci).astype(F32)
    blk = lambda s: (ri // s) == (ci // s)
    prev = min(GDN_INV_BASE, size)
    base_mask = blk(prev)
    d_list = [(eye_f - jnp.where(base_mask, a, 0.0)).astype(BF16) for a in a_list]
    while prev < size:
        cur = min(prev * GDN_INV_JOIN, size)
        off_mask = blk(cur) & jnp.logical_not(blk(prev))
        ms = [_bdot(d, jnp.where(off_mask, a, 0.0).astype(BF16)) for a, d in zip(a_list, d_list)]
        rs = [eye_f - m for m in ms]
        mps = [m.astype(BF16) for m in ms]
        span = 2
        while span < cur // prev:
            mps = [_bdot(mp, mp).astype(BF16) for mp in mps]
            rs = [r + _bdot(r.astype(BF16), mp) for r, mp in zip(rs, mps)]
            span *= 2
        d_list = [_bdot(r.astype(BF16), d).astype(BF16) for r, d in zip(rs, d_list)]
        prev = cur
    return d_list


def _gdn_kernel(act_ref, z_ref, ba_ref, s0_ref, alog_ref, dtb_ref, ng_ref,
                y_ref, sT_ref, s_ref, *, chunk, bt):
    C, Dh, H = chunk, GDN_HEAD_DIM, GDN_HEADS

    @pl.when(pl.program_id(1) == 0)
    def _():
        s_ref[...] = s0_ref[...]

    act_all = act_ref[...]
    ba = ba_ref[...]
    beta_all = _sigmoid(ba)
    xa = ba + dtb_ref[...]
    softplus = jnp.maximum(xa, 0.0) + jnp.log1p(jnp.exp(-jnp.abs(xa)))
    g = -jnp.exp(alog_ref[...]) * softplus

    ri = lax.broadcasted_iota(jnp.int32, (C, C), 0)
    ci = lax.broadcasted_iota(jnp.int32, (C, C), 1)
    causal = ri >= ci
    strict = ri > ci
    eye = ri == ci
    ng = ng_ref[...]
    tril = causal.astype(F32)
    decay_all = [_mm_hi(tril, g[b]) for b in range(bt)]
    units = [(b, h) for b in range(bt) for h in range(H)]

    e_list, kbq_list, kT_list, rhs_list, qd_list, kdT_list, gl_list = [], [], [], [], [], [], []
    for b, h in units:
        decay, beta, act = decay_all[b], beta_all[b], act_all[b]
        dcol = decay[:, H + h:H + h + 1]
        dmat = jnp.broadcast_to(dcol, (C, C))
        drow = jnp.sum(jnp.where(eye, dmat, 0.0), axis=0, keepdims=True)
        e = jnp.exp(dmat - drow)
        bcol = beta[:, h:h + 1]
        q = act[:, h * Dh:(h + 1) * Dh]
        k = act[:, GDN_WIDTH + h * Dh:GDN_WIDTH + (h + 1) * Dh]
        v = act[:, 2 * GDN_WIDTH + h * Dh:2 * GDN_WIDTH + (h + 1) * Dh]
        q = q * (lax.rsqrt(jnp.sum(q * q, -1, keepdims=True) + RMS_EPS) * (Dh ** -0.5))
        k = k * lax.rsqrt(jnp.sum(k * k, -1, keepdims=True) + RMS_EPS)
        kb = k * bcol
        kT = k.T
        kT_b = kT.astype(BF16)
        edec = jnp.exp(dcol)
        dlast = decay[C - 1:C, H + h:H + h + 1]
        e_list.append(e)
        kbq_list.append(jnp.concatenate([kb, q], axis=0).astype(BF16))
        kT_list.append(kT_b)
        rhs_list.append(jnp.concatenate([v * bcol, kb * edec], axis=1).astype(BF16))
        qd_list.append((q * edec).astype(BF16))
        kdT_list.append((kT * jnp.exp(dlast - drow)).astype(BF16))
        gl_list.append(jnp.exp(dlast))
    sc_list = [_bdot(kbq, kT_b) for kbq, kT_b in zip(kbq_list, kT_list)]
    a_list = [jnp.where(strict, sc[:C] * e, 0.0) for sc, e in zip(sc_list, e_list)]
    qk_list = [jnp.where(causal, sc[C:] * e, 0.0).astype(BF16) for sc, e in zip(sc_list, e_list)]

    tinv_list = _unit_lower_inverse(a_list, ri, ci, C)

    uw_list = [_bdot(t, rhs) for t, rhs in zip(tinv_list, rhs_list)]
    s_list = [s_ref[b, h] for b, h in units]
    ws_list = [_bdot(jnp.concatenate([uw[:, Dh:].astype(BF16), qd], axis=0), s.astype(BF16))
               for uw, qd, s in zip(uw_list, qd_list, s_list)]
    vn_list = [(uw[:, :Dh] - ws[:C]).astype(BF16) for uw, ws in zip(uw_list, ws_list)]
    o_list = [ws[C:] + _bdot(qk, vn) for ws, qk, vn in zip(ws_list, qk_list, vn_list)]
    for i, (b, h) in enumerate(units):
        s_ref[b, h] = s_list[i] * gl_list[i] + _bdot(kdT_list[i], vn_list[i])
    for i, (b, h) in enumerate(units):
        o = o_list[i]
        o = o * lax.rsqrt(jnp.mean(o * o, -1, keepdims=True) + RMS_EPS) * ng
        y_ref[b, :, h * Dh:(h + 1) * Dh] = o * _silu(z_ref[b, :, h * Dh:(h + 1) * Dh])

    sT_ref[...] = s_ref[...]


def _gdn(act, z, ba, s0, a_log, dt_bias, norm_g, chunk, bt):
    B, L, _ = act.shape
    H, Dh = GDN_HEADS, GDN_HEAD_DIM
    lt = chunk
    place = lambda t: jnp.zeros((1, LANES), F32).at[0, H:2 * H].set(t)
    xs = lambda w: pl.BlockSpec((bt, lt, w), lambda b, i: (b, i, 0))
    full = lambda *shape: pl.BlockSpec(shape, lambda b, i: (0,) * len(shape))
    state = pl.BlockSpec((bt, H, Dh, Dh), lambda b, i: (b, 0, 0, 0))
    return pl.pallas_call(
        functools.partial(_gdn_kernel, chunk=chunk, bt=bt),
        grid=(B // bt, L // lt),
        in_specs=[xs(CONV_CH), xs(GDN_WIDTH), xs(LANES), state,
                  full(1, LANES), full(1, LANES), full(1, Dh)],
        out_specs=[xs(GDN_WIDTH), state],
        out_shape=[jax.ShapeDtypeStruct((B, L, GDN_WIDTH), F32),
                   jax.ShapeDtypeStruct((B, H, Dh, Dh), F32)],
        scratch_shapes=[pltpu.VMEM((bt, H, Dh, Dh), F32)],
        compiler_params=_cparams("arbitrary", "arbitrary"),
        name="gdn",
    )(act, z, ba, s0, place(a_log), place(dt_bias), norm_g.reshape(1, Dh))


FF_BLK = 256


def _gelu_tanh(x):
    return 0.5 * x * (1.0 + jnp.tanh(0.7978845608028654 * (x + 0.044715 * x * x * x)))


def _outffn_kernel(x_ref, ya_ref, yb_ref, g1_ref, sh2_ref, sc2_ref, g2_ref,
                   wglu_ref, wout_ref, l1g_ref, l1b_ref, wup_ref, wdn_ref, l2g_ref, l2b_ref,
                   o_ref):
    bt, lt, d = x_ref.shape
    if bt % 2 == 0:
        subs = [(slice(i * bt // 2, (i + 1) * bt // 2), slice(0, lt)) for i in range(2)]
    else:
        subs = [(slice(0, bt), slice(i * lt // 2, (i + 1) * lt // 2)) for i in range(2)]
    rd = lambda ref, sub: ref[sub[0], sub[1], :]
    flat = lambda t: t.reshape(t.shape[0] * t.shape[1], t.shape[-1])
    unflat = lambda t, like: t.reshape(like.shape[0], like.shape[1], t.shape[-1])
    mod = lambda ref, sub: ref[sub[0], 0]
    nsub = range(len(subs))

    zz = [_mm(_gelu_tanh(flat(rd(ya_ref, s))), wglu_ref[...]) for s in subs]
    y_a = [z[:, :S5_WIDTH] * _sigmoid(z[:, S5_WIDTH:]) for z in zz]
    mix_b = [_mm(flat(rd(yb_ref, s)), wout_ref[S5_WIDTH:, :]) for s in subs]
    mix = [mb + _mm(ya, wout_ref[:S5_WIDTH, :]) for mb, ya in zip(mix_b, y_a)]
    xs = [rd(x_ref, s) for s in subs]
    x1 = [_ln(ALPHA * x + (1.0 + mod(g1_ref, s)) * unflat(mx, x)) * l1g_ref[...] + l1b_ref[...]
          for x, mx, s in zip(xs, mix, subs)]
    h = [flat(_ln(x) * (1.0 + mod(sc2_ref, s)) + mod(sh2_ref, s)).astype(BF16) for x, s in zip(x1, subs)]

    def gate_up(i, j):
        gate = _bdot(h[i], wup_ref[:, j * FF_BLK:(j + 1) * FF_BLK])
        up = _bdot(h[i], wup_ref[:, D_FF + j * FF_BLK:D_FF + (j + 1) * FF_BLK])
        return gate, up

    nblk = D_FF // FF_BLK
    acc = [None for _ in nsub]
    gu = [gate_up(i, 0) for i in nsub]
    for j in range(nblk):
        gu_next = [gate_up(i, j + 1) for i in nsub] if j + 1 < nblk else None
        for i in nsub:
            gate, up = gu[i]
            part = _mm(_silu(gate) * up, wdn_ref[j * FF_BLK:(j + 1) * FF_BLK, :])
            acc[i] = part if acc[i] is None else acc[i] + part
        gu = gu_next
    for i, s in enumerate(subs):
        o_ref[s[0], s[1], :] = (_ln(ALPHA * x1[i] + (1.0 + mod(g2_ref, s)) * unflat(acc[i], x1[i]))
                                * l2g_ref[...] + l2b_ref[...])


def _outffn(x, ya, yb, mod, w_glu, w_out, ln1_g, ln1_b, w_up, w_dn, ln2_g, ln2_b, bt, lt):
    B, L, _ = x.shape
    xs = lambda w: pl.BlockSpec((bt, lt, w), lambda b, i: (b, i, 0))
    ms = lambda idx: pl.BlockSpec((bt, 1, 1, D_MODEL), lambda b, i: (b, idx, 0, 0))
    const = lambda a: pl.BlockSpec(a.shape, lambda b, i: (0,) * a.ndim, pipeline_mode=pl.Buffered(1))
    vec = lambda t: t.reshape(1, D_MODEL)
    weights = (w_glu, w_out, vec(ln1_g), vec(ln1_b), w_up, w_dn, vec(ln2_g), vec(ln2_b))
    return pl.pallas_call(
        _outffn_kernel,
        grid=(B // bt, L // lt),
        in_specs=[xs(D_MODEL), xs(S5_WIDTH), xs(GDN_WIDTH), ms(2), ms(3), ms(4), ms(5)]
                 + [const(a) for a in weights],
        out_specs=xs(D_MODEL),
        out_shape=jax.ShapeDtypeStruct((B, L, D_MODEL), F32),
        compiler_params=_cparams("arbitrary", "arbitrary"),
        name="outffn",
    )(x, ya, yb, mod, mod, mod, mod, *weights)


def _layer(x, mod, h_re, h_im, s_gdn, conv_buf, wts, bt, lt, gdn_chunk, gdn_bt):
    (w_in_p, s5_mats, s5_d, w_glu, conv_w, a_log, dt_bias, norm_g, w_out,
     ln1_g, ln1_b, w_up, w_dn, ln2_g, ln2_b) = wts
    u, act, z, ba, cbo = _inproj(x, mod, w_in_p, conv_buf, conv_w, bt, lt)
    w, pmat, qmat, a_t = s5_mats
    ys5, hT = _s5(u, _s5_state_in(h_re, h_im), w, pmat, qmat, a_t, s5_d, bt)
    yb, sT = _gdn(act, z, ba, s_gdn, a_log, dt_bias, norm_g, gdn_chunk, gdn_bt)
    y = _outffn(x, ys5, yb, mod, w_glu, w_out, ln1_g, ln1_b, w_up, w_dn, ln2_g, ln2_b, bt, lt)
    o_re, o_im = _s5_state_out(hT)
    return y, o_re, o_im, sT, cbo


def kernel(x_prompt, x_sample, state_s5_re, state_s5_im, state_gdn, cache_gdn_conv, c_prompt, c_sample, w_ada, b_ada, w_in, s5_a_re, s5_a_im, s5_log_dt, s5_b_re, s5_b_im, s5_c_re, s5_c_im, s5_d, w_s5_glu, gdn_conv_w, gdn_a_log, gdn_dt_bias, gdn_norm_g, w_out, ln1_g, ln1_b, w_ffn_up, w_ffn_down, ln2_g, ln2_b):
    bp, bs = x_prompt.shape[0], x_sample.shape[0]
    assert w_ada.shape[0] == DEPTH == 1
    l = 0
    mod = _ada(jnp.concatenate([c_prompt, c_sample], 0), w_ada[l], b_ada[l])
    mod = mod.reshape(bp + bs, 6, 1, D_MODEL)
    w_in_p = jnp.pad(w_in[l], ((0, 0), (0, IN_PAD - w_in.shape[-1]))).astype(BF16)
    s5_mats = _s5_prep(s5_a_re[l], s5_a_im[l], s5_log_dt[l], s5_b_re[l], s5_b_im[l],
                       s5_c_re[l], s5_c_im[l])
    wts = (w_in_p, s5_mats, s5_d[l], w_s5_glu[l].astype(BF16), gdn_conv_w[l], gdn_a_log[l],
           gdn_dt_bias[l], gdn_norm_g[l], w_out[l].astype(BF16), ln1_g[l], ln1_b[l],
           w_ffn_up[l].astype(BF16), w_ffn_down[l].astype(BF16), ln2_g[l], ln2_b[l])
    zeros = lambda *s: jnp.zeros(s, F32)
    yp, p_re, p_im, p_gdn, p_conv = _layer(
        x_prompt, mod[:bp],
        zeros(bp, S5_GROUPS, S5_STATE), zeros(bp, S5_GROUPS, S5_STATE),
        zeros(bp, GDN_HEADS, GDN_HEAD_DIM, GDN_HEAD_DIM), zeros(bp, CONV_WIDTH - 1, CONV_CH),
        wts, bt=1, lt=512, gdn_chunk=256, gdn_bt=2)
    ys, s_re, s_im, s_gdn, s_conv = _layer(
        x_sample, mod[bp:], state_s5_re[l], state_s5_im[l], state_gdn[l], cache_gdn_conv[l],
        wts, bt=bs, lt=x_sample.shape[1], gdn_chunk=x_sample.shape[1], gdn_bt=4)
    st = lambda t: t[None]
    return (yp, ys, st(p_re), st(p_im), st(p_gdn), st(p_conv),
            st(s_re), st(s_im), st(s_gdn), st(s_conv))
```

```python
import functools

import jax
import jax.numpy as jnp
from jax import lax
from jax.experimental import pallas as pl
from jax.experimental.pallas import tpu as pltpu

F32 = jnp.float32
BF16 = jnp.bfloat16

D_MODEL = 1024
S5_WIDTH = 512
S5_GROUP = 16
S5_GROUPS = 32
S5_STATE = 64
GDN_WIDTH = 512
GDN_HEAD_DIM = 128
GDN_HEADS = 4
CONV_WIDTH = 4
CONV_CH = 3 * GDN_WIDTH
D_FF = 2816
GDN_CHUNK = 64
LN_EPS = 1e-5
RMS_EPS = 1e-6
DEPTH = 1
ALPHA = (2 * DEPTH) ** 0.25

LANES = 128
S5_T = 8
S5_BLK_GROUPS = LANES // S5_GROUP
S5_NBLK = S5_GROUPS // S5_BLK_GROUPS
S5_BLK_STATE = S5_BLK_GROUPS * S5_STATE
SUBLANES = 8
S5_POW_ROWS = 2 * SUBLANES
IN_PAD = S5_WIDTH + 4 * GDN_WIDTH + LANES
VMEM_LIMIT = 56 * 1024 * 1024

HI = lax.Precision.HIGHEST


def _cparams(*sem):
    return pltpu.CompilerParams(dimension_semantics=sem, vmem_limit_bytes=VMEM_LIMIT)


def _ln(x):
    mu = jnp.mean(x, -1, keepdims=True)
    xc = x - mu
    var = jnp.mean(xc * xc, -1, keepdims=True)
    return xc * lax.rsqrt(var + LN_EPS)


def _sigmoid(x):
    return 1.0 / (1.0 + jnp.exp(-x))


def _silu(x):
    return x * _sigmoid(x)


def _split_bf16(x):
    hi = x.astype(BF16)
    return hi, (x - hi.astype(F32)).astype(BF16)


def _ada_kernel(c_ref, w_ref, b_ref, o_ref):
    c_hi, c_lo = _split_bf16(_silu(c_ref[...]))
    w_hi, w_lo = _split_bf16(w_ref[...])
    dot = functools.partial(jnp.dot, preferred_element_type=F32)
    o_ref[...] = dot(c_hi, w_hi) + (dot(c_hi, w_lo) + dot(c_lo, w_hi)) + b_ref[...]


def _ada(c, w_ada, b_ada):
    rows = c.shape[0]
    ncol = w_ada.shape[1]
    blk = D_MODEL
    return pl.pallas_call(
        _ada_kernel,
        grid=(ncol // blk,),
        in_specs=[pl.BlockSpec((rows, D_MODEL), lambda j: (0, 0)),
                  pl.BlockSpec((D_MODEL, blk), lambda j: (0, j)),
                  pl.BlockSpec((1, blk), lambda j: (0, j))],
        out_specs=pl.BlockSpec((rows, blk), lambda j: (0, j)),
        out_shape=jax.ShapeDtypeStruct((rows, ncol), F32),
        compiler_params=_cparams("arbitrary"),
        name="ada",
    )(c, w_ada, b_ada.reshape(1, ncol))


def _s5prep_kernel(are_ref, aim_ref, ldt_ref, btre_ref, btim_ref, ctre_ref, ctim_ref,
                   w_ref, p_ref, qt_ref, at_ref):
    T, half = S5_T, S5_BLK_STATE
    a_re = are_ref[0]
    a_im = aim_ref[0]
    dt = jnp.exp(ldt_ref[0])
    den = a_re * a_re + a_im * a_im

    def lam_pow(tau):
        mag = jnp.exp(a_re * dt * tau)
        return mag * jnp.cos(a_im * dt * tau), mag * jnp.sin(a_im * dt * tau)

    l_re, l_im = lam_pow(1.0)
    n_re, n_im = l_re - 1.0, l_im
    f_re = (n_re * a_re + n_im * a_im) / den
    f_im = (n_im * a_re - n_re * a_im) / den
    same_group = (lax.broadcasted_iota(jnp.int32, (LANES, half), 0) // S5_GROUP
                  == lax.broadcasted_iota(jnp.int32, (LANES, half), 1) // S5_STATE)
    bm_re = jnp.where(same_group, btre_ref[0], 0.0)
    bm_im = jnp.where(same_group, btim_ref[0], 0.0)
    cm_re = jnp.where(same_group, ctre_ref[0], 0.0)
    cm_im = jnp.where(same_group, ctim_ref[0], 0.0)
    nt = lambda a, b: lax.dot_general(a, b, (((1,), (1,)), ((), ())), preferred_element_type=F32)
    cm_re_s, cm_im_s = _split_bf16(cm_re), _split_bf16(cm_im)

    def dot_nt(a, b_split):
        a_hi, a_lo = _split_bf16(a)
        return nt(a_hi, b_split[0]) + (nt(a_hi, b_split[1]) + nt(a_lo, b_split[0]))
    blk = lambda i: slice(i * LANES, (i + 1) * LANES)
    for s in range(T):
        for t in range(s):
            w_ref[0, blk(s), blk(t)] = jnp.zeros((LANES, LANES), BF16)
    m_t = lax.broadcasted_iota(jnp.int32, (S5_POW_ROWS, half), 0).astype(F32) * float(T)
    mag = jnp.exp(a_re * dt * m_t)
    at_ref[0, :, :half] = mag * jnp.cos(a_im * dt * m_t)
    at_ref[0, :, half:] = mag * jnp.sin(a_im * dt * m_t)
    for tau in range(T + 1):
        p_re, p_im = (jnp.ones_like(a_re), jnp.zeros_like(a_re)) if tau == 0 else lam_pow(float(tau))
        if tau < T:
            g_re = p_re * f_re - p_im * f_im
            g_im = p_re * f_im + p_im * f_re
            inj_re = g_re * bm_re - g_im * bm_im
            inj_im = g_re * bm_im + g_im * bm_re
            s = T - 1 - tau
            p_ref[0, blk(s), :half] = inj_re.astype(BF16)
            p_ref[0, blk(s), half:] = inj_im.astype(BF16)
            k_tau = (dot_nt(inj_re, cm_re_s) - dot_nt(inj_im, cm_im_s)).astype(BF16)
            for s in range(T - tau):
                w_ref[0, blk(s), blk(s + tau)] = k_tau
        if tau >= 1:
            qt_ref[0, blk(tau - 1), :half] = (cm_re * p_re - cm_im * p_im).astype(BF16)
            qt_ref[0, blk(tau - 1), half:] = (-(cm_re * p_im + cm_im * p_re)).astype(BF16)


def _s5_prep(a_re, a_im, log_dt, b_re, b_im, c_re, c_im):
    NB, T, half = S5_NBLK, S5_T, S5_BLK_STATE
    tl, st = T * LANES, 2 * half
    lanes = lambda t: t.reshape(NB, 1, half)
    tiled = lambda t: jnp.tile(t.reshape(NB, LANES, S5_STATE), (1, 1, S5_BLK_GROUPS))
    row = pl.BlockSpec((1, 1, half), lambda k: (k, 0, 0))
    mat = pl.BlockSpec((1, LANES, half), lambda k: (k, 0, 0))
    return pl.pallas_call(
        _s5prep_kernel,
        grid=(NB,),
        in_specs=[row, row, row, mat, mat, mat, mat],
        out_specs=[pl.BlockSpec((1, tl, tl), lambda k: (k, 0, 0)),
                   pl.BlockSpec((1, tl, st), lambda k: (k, 0, 0)),
                   pl.BlockSpec((1, tl, st), lambda k: (k, 0, 0)),
                   pl.BlockSpec((1, S5_POW_ROWS, st), lambda k: (k, 0, 0))],
        out_shape=[jax.ShapeDtypeStruct((NB, tl, tl), BF16),
                   jax.ShapeDtypeStruct((NB, tl, st), BF16),
                   jax.ShapeDtypeStruct((NB, tl, st), BF16),
                   jax.ShapeDtypeStruct((NB, S5_POW_ROWS, st), F32)],
        compiler_params=_cparams("arbitrary"),
        name="s5prep",
    )(lanes(a_re), lanes(a_im), lanes(jnp.repeat(log_dt, S5_STATE)),
      tiled(jnp.swapaxes(b_re, 1, 2)), tiled(jnp.swapaxes(b_im, 1, 2)), tiled(c_re), tiled(c_im))


def _s5_kernel(u_ref, h0_ref, w_ref, p_ref, q_ref, at_ref, d_ref, y_ref, hT_ref,
               lhs_ref, x_ref, hp_ref, *, bt, n):
    T, half = S5_T, S5_BLK_STATE
    for b in range(bt):
        for t in range(T):
            lhs_ref[b * n:(b + 1) * n, t * LANES:(t + 1) * LANES] = (
                u_ref[b, pl.ds(t, n, stride=T), :].astype(BF16))
    lhs = lhs_ref[...]
    x_ref[...] = jnp.dot(lhs, p_ref[0], preferred_element_type=F32)
    cplx = lambda m: (at_ref[0, m:m + 1, :half], at_ref[0, m:m + 1, half:])
    rowi = lax.broadcasted_iota(jnp.int32, (SUBLANES, half), 0)
    shifts = []
    d = 1
    while d < SUBLANES:
        p_re, p_im = cplx(d)
        shifts.append((d, jnp.where(rowi >= d, p_re, 0.0), jnp.where(rowi >= d, p_im, 0.0)))
        d *= 2
    row_re = at_ref[0, 0:SUBLANES, :half]
    row_im = at_ref[0, 0:SUBLANES, half:]
    full_re, full_im = cplx(SUBLANES)
    y_in = jnp.dot(lhs, w_ref[0], preferred_element_type=F32)
    for b in range(bt):
        def tile(i, c):
            c_re, c_im = c
            rows = slice(b * n + i * SUBLANES, b * n + (i + 1) * SUBLANES)
            y_re = x_ref[rows, :half]
            y_im = x_ref[rows, half:]
            for d, m_re, m_im in shifts:
                s_re = pltpu.roll(y_re, d, axis=0)
                s_im = pltpu.roll(y_im, d, axis=0)
                y_re, y_im = y_re + (m_re * s_re - m_im * s_im), y_im + (m_re * s_im + m_im * s_re)
            e_re = jnp.where(rowi == 0, 0.0, pltpu.roll(y_re, 1, axis=0))
            e_im = jnp.where(rowi == 0, 0.0, pltpu.roll(y_im, 1, axis=0))
            hp_ref[rows, :half] = e_re + (row_re * c_re - row_im * c_im)
            hp_ref[rows, half:] = e_im + (row_re * c_im + row_im * c_re)
            last = SUBLANES - 1
            return (y_re[last:] + (full_re * c_re - full_im * c_im),
                    y_im[last:] + (full_re * c_im + full_im * c_re))
        c = (h0_ref[b, 0, :, :half], h0_ref[b, 0, :, half:])
        for i in range(n // SUBLANES):
            c = tile(i, c)
        hT_ref[b, 0, :, :half] = c[0]
        hT_ref[b, 0, :, half:] = c[1]
    y = (y_in
         + lax.dot_general(hp_ref[...].astype(BF16), q_ref[0], (((1,), (1,)), ((), ())),
                           preferred_element_type=F32))
    d = d_ref[...]
    for b in range(bt):
        for t in range(T):
            y_ref[b, pl.ds(t, n, stride=T), :] = (
                y[b * n:(b + 1) * n, t * LANES:(t + 1) * LANES]
                + d * u_ref[b, pl.ds(t, n, stride=T), :])


def _s5(u, h0, w, pmat, qmat, a_t, d_skip, bt):
    B, L, _ = u.shape
    n = L // S5_T
    tl, st = S5_T * LANES, 2 * S5_BLK_STATE
    assert n % SUBLANES == 0
    return pl.pallas_call(
        functools.partial(_s5_kernel, bt=bt, n=n),
        grid=(S5_NBLK, B // bt),
        in_specs=[pl.BlockSpec((bt, L, LANES), lambda k, b: (b, 0, k)),
                  pl.BlockSpec((bt, 1, 1, st), lambda k, b: (b, k, 0, 0)),
                  pl.BlockSpec((1, tl, tl), lambda k, b: (k, 0, 0)),
                  pl.BlockSpec((1, tl, st), lambda k, b: (k, 0, 0)),
                  pl.BlockSpec((1, tl, st), lambda k, b: (k, 0, 0)),
                  pl.BlockSpec((1, S5_POW_ROWS, st), lambda k, b: (k, 0, 0)),
                  pl.BlockSpec((1, LANES), lambda k, b: (0, k))],
        out_specs=[pl.BlockSpec((bt, L, LANES), lambda k, b: (b, 0, k)),
                   pl.BlockSpec((bt, 1, 1, st), lambda k, b: (b, k, 0, 0))],
        out_shape=[jax.ShapeDtypeStruct((B, L, S5_WIDTH), F32),
                   jax.ShapeDtypeStruct((B, S5_NBLK, 1, st), F32)],
        scratch_shapes=[pltpu.VMEM((bt * n, tl), BF16),
                        pltpu.VMEM((bt * n, st), F32),
                        pltpu.VMEM((bt * n, st), F32)],
        compiler_params=_cparams("arbitrary", "arbitrary"),
        name="s5",
    )(u, h0, w, pmat, qmat, a_t, d_skip.reshape(1, S5_WIDTH))


def _s5_state_in(h_re, h_im):
    B = h_re.shape[0]
    r = h_re.reshape(B, S5_NBLK, 1, S5_BLK_STATE)
    i = h_im.reshape(B, S5_NBLK, 1, S5_BLK_STATE)
    return jnp.concatenate([r, i], -1)


def _s5_state_out(h):
    B = h.shape[0]
    h = h.reshape(B, S5_NBLK, 2, S5_BLK_GROUPS, S5_STATE)
    return (h[:, :, 0].reshape(B, S5_GROUPS, S5_STATE), h[:, :, 1].reshape(B, S5_GROUPS, S5_STATE))


CONV_TAIL = SUBLANES


def _inproj_kernel(x_ref, sh_ref, sc_ref, w_ref, cb_ref, cw_ref,
                   u_ref, act_ref, z_ref, ba_ref, cbo_ref, *xbuf_refs):
    bt, lt, d = x_ref.shape
    keep = CONV_WIDTH - 1
    o_q, o_z, o_b = S5_WIDTH, S5_WIDTH + CONV_CH, S5_WIDTH + CONV_CH + GDN_WIDTH
    parts = [slice(p * GDN_WIDTH, (p + 1) * GDN_WIDTH) for p in range(len(xbuf_refs))]

    @pl.when(pl.program_id(1) == 0)
    def _():
        for cs, xbuf_ref in zip(parts, xbuf_refs):
            xbuf_ref[:, 0:CONV_TAIL, :] = jnp.zeros((bt, CONV_TAIL, GDN_WIDTH), F32)
            xbuf_ref[:, CONV_TAIL - keep:CONV_TAIL, :] = cb_ref[:, :, cs]

    h = _ln(x_ref[...]) * (1.0 + sc_ref[:, 0]) + sh_ref[:, 0]
    h = h.reshape(bt * lt, d).astype(BF16)
    for cs, xbuf_ref in zip(parts, xbuf_refs):
        xbuf_ref[:, CONV_TAIL:CONV_TAIL + lt, :] = (
            _bdot(h, w_ref[:, o_q + cs.start:o_q + cs.stop]).reshape(bt, lt, GDN_WIDTH))
    u_ref[...] = _bdot(h, w_ref[:, :o_q]).reshape(bt, lt, S5_WIDTH)
    z_ref[...] = _bdot(h, w_ref[:, o_z:o_b]).reshape(bt, lt, GDN_WIDTH)
    ba_ref[...] = _bdot(h, w_ref[:, o_b:]).reshape(bt, lt, LANES)
    for cs, xbuf_ref in zip(parts, xbuf_refs):
        conv = None
        for j in range(CONV_WIDTH):
            off = CONV_TAIL - keep + j
            term = xbuf_ref[:, off:off + lt, :] * cw_ref[j:j + 1, cs]
            conv = term if conv is None else conv + term
        act_ref[:, :, cs] = _silu(conv)
        cbo_ref[:, :, cs] = xbuf_ref[:, lt + CONV_TAIL - keep:lt + CONV_TAIL, :]
        xbuf_ref[:, 0:CONV_TAIL, :] = xbuf_ref[:, lt:lt + CONV_TAIL, :]


def _inproj(x, mod, w_in_p, conv_buf, conv_w, bt, lt):
    B, L, _ = x.shape
    xs = lambda w: pl.BlockSpec((bt, lt, w), lambda b, i: (b, i, 0))
    ms = lambda idx: pl.BlockSpec((bt, 1, 1, D_MODEL), lambda b, i: (b, idx, 0, 0))
    cache = pl.BlockSpec((bt, CONV_WIDTH - 1, CONV_CH), lambda b, i: (b, 0, 0))
    return pl.pallas_call(
        _inproj_kernel,
        grid=(B // bt, L // lt),
        in_specs=[xs(D_MODEL), ms(0), ms(1),
                  pl.BlockSpec((D_MODEL, IN_PAD), lambda b, i: (0, 0)), cache,
                  pl.BlockSpec((CONV_WIDTH, CONV_CH), lambda b, i: (0, 0))],
        out_specs=[xs(S5_WIDTH), xs(CONV_CH), xs(GDN_WIDTH), xs(LANES), cache],
        out_shape=[jax.ShapeDtypeStruct((B, L, w), F32) for w in (S5_WIDTH, CONV_CH, GDN_WIDTH, LANES)]
                  + [jax.ShapeDtypeStruct((B, CONV_WIDTH - 1, CONV_CH), F32)],
        scratch_shapes=[pltpu.VMEM((bt, lt + CONV_TAIL, GDN_WIDTH), F32)
                        for _ in range(CONV_CH // GDN_WIDTH)],
        compiler_params=_cparams("arbitrary", "arbitrary"),
        name="inproj",
    )(x, mod, mod, w_in_p, conv_buf, conv_w)


def _mm(a, b):
    return jnp.dot(a.astype(BF16), b.astype(BF16), preferred_element_type=F32)


def _mm_nt(a, b):
    return lax.dot_general(a.astype(BF16), b.astype(BF16), (((1,), (1,)), ((), ())),
                           preferred_element_type=F32)


def _mm_tn(a, b):
    return lax.dot_general(a.astype(BF16), b.astype(BF16), (((0,), (0,)), ((), ())),
                           preferred_element_type=F32)


def _mm_hi(a, b):
    return jnp.dot(a, b, preferred_element_type=F32, precision=HI)


GDN_INV_BASE = 2
GDN_INV_JOIN = 4


def _bdot(a, b):
    return jnp.dot(a, b, preferred_element_type=F32)


def _unit_lower_inverse(a_list, ri, ci, size):
    eye_f = (ri == ci).astype(F32)
    blk = lambda s: (ri // s) == (ci // s)
    prev = min(GDN_INV_BASE, size)
    base_mask = blk(prev)
    d_list = [(eye_f - jnp.where(base_mask, a, 0.0)).astype(BF16) for a in a_list]
    while prev < size:
        cur = min(prev * GDN_INV_JOIN, size)
        off_mask = blk(cur) & jnp.logical_not(blk(prev))
        ms = [_bdot(d, jnp.where(off_mask, a, 0.0).astype(BF16)) for a, d in zip(a_list, d_list)]
        rs = [eye_f - m for m in ms]
        mps = [m.astype(BF16) for m in ms]
        span = 2
        while span < cur // prev:
            mps = [_bdot(mp, mp).astype(BF16) for mp in mps]
            rs = [r + _bdot(r.astype(BF16), mp) for r, mp in zip(rs, mps)]
            span *= 2
        d_list = [_bdot(r.astype(BF16), d).astype(BF16) for r, d in zip(rs, d_list)]
        prev = cur
    return d_list


def _gdn_kernel(act_ref, z_ref, ba_ref, s0_ref, alog_ref, dtb_ref, ng_ref,
                y_ref, sT_ref, s_ref, *, chunk, bt):
    C, Dh, H = chunk, GDN_HEAD_DIM, GDN_HEADS

    @pl.when(pl.program_id(1) == 0)
    def _():
        s_ref[...] = s0_ref[...]

    act_all = act_ref[...]
    ba = ba_ref[...]
    beta_all = _sigmoid(ba)
    xa = ba + dtb_ref[...]
    softplus = jnp.maximum(xa, 0.0) + jnp.log1p(jnp.exp(-jnp.abs(xa)))
    g = -jnp.exp(alog_ref[...]) * softplus

    ri = lax.broadcasted_iota(jnp.int32, (C, C), 0)
    ci = lax.broadcasted_iota(jnp.int32, (C, C), 1)
    causal = ri >= ci
    strict = ri > ci
    eye = ri == ci
    ng = ng_ref[...]
    tril = causal.astype(BF16)
    decay_all = []
    for b in range(bt):
        g_hi = g[b].astype(BF16)
        g_rest = g[b] - g_hi.astype(F32)
        g_mid = g_rest.astype(BF16)
        g_lo = (g_rest - g_mid.astype(F32)).astype(BF16)
        d3 = _bdot(tril, jnp.concatenate([g_hi, g_mid, g_lo], axis=1))
        decay_all.append(d3[:, :LANES] + (d3[:, LANES:2 * LANES] + d3[:, 2 * LANES:]))
    units = [(b, h) for b in range(bt) for h in range(H)]

    e_list, kbq_list, kT_list, rhs_list, qd_list, kdT_list, gl_list = [], [], [], [], [], [], []
    for b, h in units:
        decay, beta, act = decay_all[b], beta_all[b], act_all[b]
        dcol = decay[:, H + h:H + h + 1]
        dmat = jnp.broadcast_to(dcol, (C, C))
        drow = jnp.sum(jnp.where(eye, dmat, 0.0), axis=0, keepdims=True)
        e = jnp.exp(dmat - drow)
        bcol = beta[:, h:h + 1]
        q = act[:, h * Dh:(h + 1) * Dh]
        k = act[:, GDN_WIDTH + h * Dh:GDN_WIDTH + (h + 1) * Dh]
        v = act[:, 2 * GDN_WIDTH + h * Dh:2 * GDN_WIDTH + (h + 1) * Dh]
        q = q * (lax.rsqrt(jnp.sum(q * q, -1, keepdims=True) + RMS_EPS) * (Dh ** -0.5))
        k = k * lax.rsqrt(jnp.sum(k * k, -1, keepdims=True) + RMS_EPS)
        kb = k * bcol
        kT = k.T
        kT_b = kT.astype(BF16)
        edec = jnp.exp(dcol)
        dlast = decay[C - 1:C, H + h:H + h + 1]
        e_list.append(e)
        kbq_list.append(jnp.concatenate([kb, q], axis=0).astype(BF16))
        kT_list.append(kT_b)
        rhs_list.append(jnp.concatenate([v * bcol, kb * edec], axis=1).astype(BF16))
        qd_list.append((q * edec).astype(BF16))
        kdT_list.append((kT * jnp.exp(dlast - drow)).astype(BF16))
        gl_list.append(jnp.exp(dlast))
    sc_list = [_bdot(kbq, kT_b) for kbq, kT_b in zip(kbq_list, kT_list)]
    a_list = [jnp.where(strict, sc[:C] * e, 0.0) for sc, e in zip(sc_list, e_list)]
    qk_list = [jnp.where(causal, sc[C:] * e, 0.0).astype(BF16) for sc, e in zip(sc_list, e_list)]

    tinv_list = _unit_lower_inverse(a_list, ri, ci, C)

    uw_list = [_bdot(t, rhs) for t, rhs in zip(tinv_list, rhs_list)]
    s_list = [s_ref[b, h] for b, h in units]
    ws_list = [_bdot(jnp.concatenate([uw[:, Dh:].astype(BF16), qd], axis=0), s.astype(BF16))
               for uw, qd, s in zip(uw_list, qd_list, s_list)]
    vn_list = [(uw[:, :Dh] - ws[:C]).astype(BF16) for uw, ws in zip(uw_list, ws_list)]
    o_list = [ws[C:] + _bdot(qk, vn) for ws, qk, vn in zip(ws_list, qk_list, vn_list)]
    for i, (b, h) in enumerate(units):
        s_ref[b, h] = s_list[i] * gl_list[i] + _bdot(kdT_list[i], vn_list[i])
    for i, (b, h) in enumerate(units):
        o = o_list[i]
        o = o * lax.rsqrt(jnp.mean(o * o, -1, keepdims=True) + RMS_EPS) * ng
        y_ref[b, :, h * Dh:(h + 1) * Dh] = o * _silu(z_ref[b, :, h * Dh:(h + 1) * Dh])

    sT_ref[...] = s_ref[...]


def _gdn(act, z, ba, s0, a_log, dt_bias, norm_g, chunk, bt):
    B, L, _ = act.shape
    H, Dh = GDN_HEADS, GDN_HEAD_DIM
    lt = chunk
    place = lambda t: jnp.zeros((1, LANES), F32).at[0, H:2 * H].set(t)
    xs = lambda w: pl.BlockSpec((bt, lt, w), lambda b, i: (b, i, 0))
    full = lambda *shape: pl.BlockSpec(shape, lambda b, i: (0,) * len(shape))
    state = pl.BlockSpec((bt, H, Dh, Dh), lambda b, i: (b, 0, 0, 0))
    return pl.pallas_call(
        functools.partial(_gdn_kernel, chunk=chunk, bt=bt),
        grid=(B // bt, L // lt),
        in_specs=[xs(CONV_CH), xs(GDN_WIDTH), xs(LANES), state,
                  full(1, LANES), full(1, LANES), full(1, Dh)],
        out_specs=[xs(GDN_WIDTH), state],
        out_shape=[jax.ShapeDtypeStruct((B, L, GDN_WIDTH), F32),
                   jax.ShapeDtypeStruct((B, H, Dh, Dh), F32)],
        scratch_shapes=[pltpu.VMEM((bt, H, Dh, Dh), F32)],
        compiler_params=_cparams("arbitrary", "arbitrary"),
        name="gdn",
    )(act, z, ba, s0, place(a_log), place(dt_bias), norm_g.reshape(1, Dh))


FF_BLK = 256


def _gelu_tanh(x):
    return 0.5 * x * (1.0 + jnp.tanh(0.7978845608028654 * (x + 0.044715 * x * x * x)))


def _outffn_kernel(x_ref, ya_ref, yb_ref, g1_ref, sh2_ref, sc2_ref, g2_ref,
                   wglu_ref, wout_ref, l1g_ref, l1b_ref, wup_ref, wdn_ref, l2g_ref, l2b_ref,
                   o_ref):
    bt, lt, d = x_ref.shape
    if bt % 2 == 0:
        subs = [(slice(i * bt // 2, (i + 1) * bt // 2), slice(0, lt)) for i in range(2)]
    else:
        subs = [(slice(0, bt), slice(i * lt // 2, (i + 1) * lt // 2)) for i in range(2)]
    rd = lambda ref, sub: ref[sub[0], sub[1], :]
    flat = lambda t: t.reshape(t.shape[0] * t.shape[1], t.shape[-1])
    unflat = lambda t, like: t.reshape(like.shape[0], like.shape[1], t.shape[-1])
    mod = lambda ref, sub: ref[sub[0], 0]
    nsub = range(len(subs))

    zz = [_mm(_gelu_tanh(flat(rd(ya_ref, s))), wglu_ref[...]) for s in subs]
    y_a = [z[:, :S5_WIDTH] * _sigmoid(z[:, S5_WIDTH:]) for z in zz]
    mix_b = [_mm(flat(rd(yb_ref, s)), wout_ref[S5_WIDTH:, :]) for s in subs]
    mix = [mb + _mm(ya, wout_ref[:S5_WIDTH, :]) for mb, ya in zip(mix_b, y_a)]
    xs = [rd(x_ref, s) for s in subs]
    x1 = [_ln(ALPHA * x + (1.0 + mod(g1_ref, s)) * unflat(mx, x)) * l1g_ref[...] + l1b_ref[...]
          for x, mx, s in zip(xs, mix, subs)]
    h = [flat(_ln(x) * (1.0 + mod(sc2_ref, s)) + mod(sh2_ref, s)).astype(BF16) for x, s in zip(x1, subs)]

    def gate_up(i, j):
        gate = _bdot(h[i], wup_ref[:, j * FF_BLK:(j + 1) * FF_BLK])
        up = _bdot(h[i], wup_ref[:, D_FF + j * FF_BLK:D_FF + (j + 1) * FF_BLK])
        return gate, up

    nblk = D_FF // FF_BLK
    acc = [None for _ in nsub]
    gu = [gate_up(i, 0) for i in nsub]
    for j in range(nblk):
        gu_next = [gate_up(i, j + 1) for i in nsub] if j + 1 < nblk else None
        for i in nsub:
            gate, up = gu[i]
            part = _mm(_silu(gate) * up, wdn_ref[j * FF_BLK:(j + 1) * FF_BLK, :])
            acc[i] = part if acc[i] is None else acc[i] + part
        gu = gu_next
    for i, s in enumerate(subs):
        o_ref[s[0], s[1], :] = (_ln(ALPHA * x1[i] + (1.0 + mod(g2_ref, s)) * unflat(acc[i], x1[i]))
                                * l2g_ref[...] + l2b_ref[...])


def _outffn(x, ya, yb, mod, w_glu, w_out, ln1_g, ln1_b, w_up, w_dn, ln2_g, ln2_b, bt, lt):
    B, L, _ = x.shape
    xs = lambda w: pl.BlockSpec((bt, lt, w), lambda b, i: (b, i, 0))
    ms = lambda idx: pl.BlockSpec((bt, 1, 1, D_MODEL), lambda b, i: (b, idx, 0, 0))
    const = lambda a: pl.BlockSpec(a.shape, lambda b, i: (0,) * a.ndim, pipeline_mode=pl.Buffered(1))
    vec = lambda t: t.reshape(1, D_MODEL)
    weights = (w_glu, w_out, vec(ln1_g), vec(ln1_b), w_up, w_dn, vec(ln2_g), vec(ln2_b))
    return pl.pallas_call(
        _outffn_kernel,
        grid=(B // bt, L // lt),
        in_specs=[xs(D_MODEL), xs(S5_WIDTH), xs(GDN_WIDTH), ms(2), ms(3), ms(4), ms(5)]
                 + [const(a) for a in weights],
        out_specs=xs(D_MODEL),
        out_shape=jax.ShapeDtypeStruct((B, L, D_MODEL), F32),
        compiler_params=_cparams("arbitrary", "arbitrary"),
        name="outffn",
    )(x, ya, yb, mod, mod, mod, mod, *weights)


def _layer(x, mod, h_re, h_im, s_gdn, conv_buf, wts, bt, lt, gdn_chunk, gdn_bt):
    (w_in_p, s5_mats, s5_d, w_glu, conv_w, a_log, dt_bias, norm_g, w_out,
     ln1_g, ln1_b, w_up, w_dn, ln2_g, ln2_b) = wts
    u, act, z, ba, cbo = _inproj(x, mod, w_in_p, conv_buf, conv_w, bt, lt)
    w, pmat, qmat, a_t = s5_mats
    ys5, hT = _s5(u, _s5_state_in(h_re, h_im), w, pmat, qmat, a_t, s5_d, bt)
    yb, sT = _gdn(act, z, ba, s_gdn, a_log, dt_bias, norm_g, gdn_chunk, gdn_bt)
    y = _outffn(x, ys5, yb, mod, w_glu, w_out, ln1_g, ln1_b, w_up, w_dn, ln2_g, ln2_b, bt, lt)
    o_re, o_im = _s5_state_out(hT)
    return y, o_re, o_im, sT, cbo


def kernel(x_prompt, x_sample, state_s5_re, state_s5_im, state_gdn, cache_gdn_conv, c_prompt, c_sample, w_ada, b_ada, w_in, s5_a_re, s5_a_im, s5_log_dt, s5_b_re, s5_b_im, s5_c_re, s5_c_im, s5_d, w_s5_glu, gdn_conv_w, gdn_a_log, gdn_dt_bias, gdn_norm_g, w_out, ln1_g, ln1_b, w_ffn_up, w_ffn_down, ln2_g, ln2_b):
    bp, bs = x_prompt.shape[0], x_sample.shape[0]
    assert w_ada.shape[0] == DEPTH == 1
    l = 0
    c_all = jnp.concatenate([c_prompt, c_sample], 0)
    c_all = jnp.pad(c_all, ((0, -(bp + bs) % (2 * SUBLANES)), (0, 0)))
    mod = _ada(c_all, w_ada[l], b_ada[l])[:bp + bs]
    mod = mod.reshape(bp + bs, 6, 1, D_MODEL)
    w_in_p = jnp.pad(w_in[l], ((0, 0), (0, IN_PAD - w_in.shape[-1]))).astype(BF16)
    s5_mats = _s5_prep(s5_a_re[l], s5_a_im[l], s5_log_dt[l], s5_b_re[l], s5_b_im[l],
                       s5_c_re[l], s5_c_im[l])
    wts = (w_in_p, s5_mats, s5_d[l], w_s5_glu[l].astype(BF16), gdn_conv_w[l], gdn_a_log[l],
           gdn_dt_bias[l], gdn_norm_g[l], w_out[l].astype(BF16), ln1_g[l], ln1_b[l],
           w_ffn_up[l].astype(BF16), w_ffn_down[l].astype(BF16), ln2_g[l], ln2_b[l])
    zeros = lambda *s: jnp.zeros(s, F32)
    yp, p_re, p_im, p_gdn, p_conv = _layer(
        x_prompt, mod[:bp],
        zeros(bp, S5_GROUPS, S5_STATE), zeros(bp, S5_GROUPS, S5_STATE),
        zeros(bp, GDN_HEADS, GDN_HEAD_DIM, GDN_HEAD_DIM), zeros(bp, CONV_WIDTH - 1, CONV_CH),
        wts, bt=1, lt=512, gdn_chunk=256, gdn_bt=2)
    ys, s_re, s_im, s_gdn, s_conv = _layer(
        x_sample, mod[bp:], state_s5_re[l], state_s5_im[l], state_gdn[l], cache_gdn_conv[l],
        wts, bt=bs, lt=x_sample.shape[1], gdn_chunk=x_sample.shape[1], gdn_bt=4)
    st = lambda t: t[None]
    return (yp, ys, st(p_re), st(p_im), st(p_gdn), st(p_conv),
            st(s_re), st(s_im), st(s_gdn), st(s_conv))
```

```python
import functools

import jax
import jax.numpy as jnp
from jax import lax
from jax.experimental import pallas as pl
from jax.experimental.pallas import tpu as pltpu

F32 = jnp.float32
BF16 = jnp.bfloat16

D_MODEL = 1024
S5_WIDTH = 512
S5_GROUP = 16
S5_GROUPS = 32
S5_STATE = 64
GDN_WIDTH = 512
GDN_HEAD_DIM = 128
GDN_HEADS = 4
CONV_WIDTH = 4
CONV_CH = 3 * GDN_WIDTH
D_FF = 2816
GDN_CHUNK = 64
LN_EPS = 1e-5
RMS_EPS = 1e-6
DEPTH = 1
ALPHA = (2 * DEPTH) ** 0.25

LANES = 128
S5_T = 8
S5_BLK_GROUPS = LANES // S5_GROUP
S5_NBLK = S5_GROUPS // S5_BLK_GROUPS
S5_BLK_STATE = S5_BLK_GROUPS * S5_STATE
SUBLANES = 8
S5_POW_ROWS = 2 * SUBLANES
IN_PAD = S5_WIDTH + 4 * GDN_WIDTH + LANES
VMEM_LIMIT = 56 * 1024 * 1024

HI = lax.Precision.HIGHEST


def _cparams(*sem):
    return pltpu.CompilerParams(dimension_semantics=sem, vmem_limit_bytes=VMEM_LIMIT)


def _ln(x):
    mu = jnp.mean(x, -1, keepdims=True)
    xc = x - mu
    var = jnp.mean(xc * xc, -1, keepdims=True)
    return xc * lax.rsqrt(var + LN_EPS)


def _sigmoid(x):
    return 1.0 / (1.0 + jnp.exp(-x))


def _silu(x):
    return x * _sigmoid(x)


def _split_bf16(x):
    hi = x.astype(BF16)
    return hi, (x - hi.astype(F32)).astype(BF16)


def _ada_kernel(c_ref, w_ref, b_ref, o_ref):
    c_hi, c_lo = _split_bf16(_silu(c_ref[...]))
    w_hi, w_lo = _split_bf16(w_ref[...])
    dot = functools.partial(jnp.dot, preferred_element_type=F32)
    o_ref[...] = dot(c_hi, w_hi) + (dot(c_hi, w_lo) + dot(c_lo, w_hi)) + b_ref[...]


def _ada(c, w_ada, b_ada):
    rows = c.shape[0]
    ncol = w_ada.shape[1]
    blk = D_MODEL
    return pl.pallas_call(
        _ada_kernel,
        grid=(ncol // blk,),
        in_specs=[pl.BlockSpec((rows, D_MODEL), lambda j: (0, 0)),
                  pl.BlockSpec((D_MODEL, blk), lambda j: (0, j)),
                  pl.BlockSpec((1, blk), lambda j: (0, j))],
        out_specs=pl.BlockSpec((rows, blk), lambda j: (0, j)),
        out_shape=jax.ShapeDtypeStruct((rows, ncol), F32),
        compiler_params=_cparams("arbitrary"),
        name="ada",
    )(c, w_ada, b_ada.reshape(1, ncol))


def _s5prep_kernel(are_ref, aim_ref, ldt_ref, btre_ref, btim_ref, ctre_ref, ctim_ref,
                   w_ref, p_ref, qt_ref, at_ref):
    T, half = S5_T, S5_BLK_STATE
    a_re = are_ref[0]
    a_im = aim_ref[0]
    dt = jnp.exp(ldt_ref[0])
    den = a_re * a_re + a_im * a_im

    def lam_pow(tau):
        mag = jnp.exp(a_re * dt * tau)
        return mag * jnp.cos(a_im * dt * tau), mag * jnp.sin(a_im * dt * tau)

    l_re, l_im = lam_pow(1.0)
    n_re, n_im = l_re - 1.0, l_im
    f_re = (n_re * a_re + n_im * a_im) / den
    f_im = (n_im * a_re - n_re * a_im) / den
    same_group = (lax.broadcasted_iota(jnp.int32, (LANES, half), 0) // S5_GROUP
                  == lax.broadcasted_iota(jnp.int32, (LANES, half), 1) // S5_STATE)
    bm_re = jnp.where(same_group, btre_ref[0], 0.0)
    bm_im = jnp.where(same_group, btim_ref[0], 0.0)
    cm_re = jnp.where(same_group, ctre_ref[0], 0.0)
    cm_im = jnp.where(same_group, ctim_ref[0], 0.0)
    nt = lambda a, b: lax.dot_general(a, b, (((1,), (1,)), ((), ())), preferred_element_type=F32)
    cm_re_s, cm_im_s = _split_bf16(cm_re), _split_bf16(cm_im)

    def dot_nt(a, b_split):
        a_hi, a_lo = _split_bf16(a)
        return nt(a_hi, b_split[0]) + (nt(a_hi, b_split[1]) + nt(a_lo, b_split[0]))
    blk = lambda i: slice(i * LANES, (i + 1) * LANES)
    for s in range(T):
        for t in range(s):
            w_ref[0, blk(s), blk(t)] = jnp.zeros((LANES, LANES), BF16)
    m_t = lax.broadcasted_iota(jnp.int32, (S5_POW_ROWS, half), 0).astype(F32) * float(T)
    mag = jnp.exp(a_re * dt * m_t)
    at_ref[0, :, :half] = mag * jnp.cos(a_im * dt * m_t)
    at_ref[0, :, half:] = mag * jnp.sin(a_im * dt * m_t)
    for tau in range(T + 1):
        p_re, p_im = (jnp.ones_like(a_re), jnp.zeros_like(a_re)) if tau == 0 else lam_pow(float(tau))
        if tau < T:
            g_re = p_re * f_re - p_im * f_im
            g_im = p_re * f_im + p_im * f_re
            inj_re = g_re * bm_re - g_im * bm_im
            inj_im = g_re * bm_im + g_im * bm_re
            s = T - 1 - tau
            p_ref[0, blk(s), :half] = inj_re.astype(BF16)
            p_ref[0, blk(s), half:] = inj_im.astype(BF16)
            k_tau = (dot_nt(inj_re, cm_re_s) - dot_nt(inj_im, cm_im_s)).astype(BF16)
            for s in range(T - tau):
                w_ref[0, blk(s), blk(s + tau)] = k_tau
        if tau >= 1:
            qt_ref[0, blk(tau - 1), :half] = (cm_re * p_re - cm_im * p_im).astype(BF16)
            qt_ref[0, blk(tau - 1), half:] = (-(cm_re * p_im + cm_im * p_re)).astype(BF16)


def _s5_prep(a_re, a_im, log_dt, b_re, b_im, c_re, c_im):
    NB, T, half = S5_NBLK, S5_T, S5_BLK_STATE
    tl, st = T * LANES, 2 * half
    lanes = lambda t: t.reshape(NB, 1, half)
    tiled = lambda t: jnp.tile(t.reshape(NB, LANES, S5_STATE), (1, 1, S5_BLK_GROUPS))
    row = pl.BlockSpec((1, 1, half), lambda k: (k, 0, 0))
    mat = pl.BlockSpec((1, LANES, half), lambda k: (k, 0, 0))
    return pl.pallas_call(
        _s5prep_kernel,
        grid=(NB,),
        in_specs=[row, row, row, mat, mat, mat, mat],
        out_specs=[pl.BlockSpec((1, tl, tl), lambda k: (k, 0, 0)),
                   pl.BlockSpec((1, tl, st), lambda k: (k, 0, 0)),
                   pl.BlockSpec((1, tl, st), lambda k: (k, 0, 0)),
                   pl.BlockSpec((1, S5_POW_ROWS, st), lambda k: (k, 0, 0))],
        out_shape=[jax.ShapeDtypeStruct((NB, tl, tl), BF16),
                   jax.ShapeDtypeStruct((NB, tl, st), BF16),
                   jax.ShapeDtypeStruct((NB, tl, st), BF16),
                   jax.ShapeDtypeStruct((NB, S5_POW_ROWS, st), F32)],
        compiler_params=_cparams("arbitrary"),
        name="s5prep",
    )(lanes(a_re), lanes(a_im), lanes(jnp.repeat(log_dt, S5_STATE)),
      tiled(jnp.swapaxes(b_re, 1, 2)), tiled(jnp.swapaxes(b_im, 1, 2)), tiled(c_re), tiled(c_im))


def _s5_kernel(u_ref, h0_ref, w_ref, p_ref, q_ref, at_ref, d_ref, y_ref, hT_ref,
               lhs_ref, x_ref, hp_ref, *, bt, n):
    T, half = S5_T, S5_BLK_STATE
    for b in range(bt):
        for t in range(T):
            lhs_ref[b * n:(b + 1) * n, t * LANES:(t + 1) * LANES] = (
                u_ref[b, pl.ds(t, n, stride=T), :].astype(BF16))
    lhs = lhs_ref[...]
    x_ref[...] = jnp.dot(lhs, p_ref[0], preferred_element_type=F32)
    cplx = lambda m: (at_ref[0, m:m + 1, :half], at_ref[0, m:m + 1, half:])
    rowi = lax.broadcasted_iota(jnp.int32, (SUBLANES, half), 0)
    shifts = []
    d = 1
    while d < SUBLANES:
        p_re, p_im = cplx(d)
        shifts.append((d, jnp.where(rowi >= d, p_re, 0.0), jnp.where(rowi >= d, p_im, 0.0)))
        d *= 2
    row_re = at_ref[0, 0:SUBLANES, :half]
    row_im = at_ref[0, 0:SUBLANES, half:]
    full_re, full_im = cplx(SUBLANES)
    y_in = jnp.dot(lhs, w_ref[0], preferred_element_type=F32)
    for b in range(bt):
        def tile(i, c):
            c_re, c_im = c
            rows = slice(b * n + i * SUBLANES, b * n + (i + 1) * SUBLANES)
            y_re = x_ref[rows, :half]
            y_im = x_ref[rows, half:]
            for d, m_re, m_im in shifts:
                s_re = pltpu.roll(y_re, d, axis=0)
                s_im = pltpu.roll(y_im, d, axis=0)
                y_re, y_im = y_re + (m_re * s_re - m_im * s_im), y_im + (m_re * s_im + m_im * s_re)
            e_re = jnp.where(rowi == 0, 0.0, pltpu.roll(y_re, 1, axis=0))
            e_im = jnp.where(rowi == 0, 0.0, pltpu.roll(y_im, 1, axis=0))
            hp_ref[rows, :half] = e_re + (row_re * c_re - row_im * c_im)
            hp_ref[rows, half:] = e_im + (row_re * c_im + row_im * c_re)
            last = SUBLANES - 1
            return (y_re[last:] + (full_re * c_re - full_im * c_im),
                    y_im[last:] + (full_re * c_im + full_im * c_re))
        c = (h0_ref[b, 0, :, :half], h0_ref[b, 0, :, half:])
        for i in range(n // SUBLANES):
            c = tile(i, c)
        hT_ref[b, 0, :, :half] = c[0]
        hT_ref[b, 0, :, half:] = c[1]
    y = (y_in
         + lax.dot_general(hp_ref[...].astype(BF16), q_ref[0], (((1,), (1,)), ((), ())),
                           preferred_element_type=F32))
    d = d_ref[...]
    for b in range(bt):
        for t in range(T):
            y_ref[b, pl.ds(t, n, stride=T), :] = (
                y[b * n:(b + 1) * n, t * LANES:(t + 1) * LANES]
                + d * u_ref[b, pl.ds(t, n, stride=T), :])


def _s5(u, h0, w, pmat, qmat, a_t, d_skip, bt):
    B, L, _ = u.shape
    n = L // S5_T
    tl, st = S5_T * LANES, 2 * S5_BLK_STATE
    assert n % SUBLANES == 0
    return pl.pallas_call(
        functools.partial(_s5_kernel, bt=bt, n=n),
        grid=(S5_NBLK, B // bt),
        in_specs=[pl.BlockSpec((bt, L, LANES), lambda k, b: (b, 0, k)),
                  pl.BlockSpec((bt, 1, 1, st), lambda k, b: (b, k, 0, 0)),
                  pl.BlockSpec((1, tl, tl), lambda k, b: (k, 0, 0)),
                  pl.BlockSpec((1, tl, st), lambda k, b: (k, 0, 0)),
                  pl.BlockSpec((1, tl, st), lambda k, b: (k, 0, 0)),
                  pl.BlockSpec((1, S5_POW_ROWS, st), lambda k, b: (k, 0, 0)),
                  pl.BlockSpec((1, LANES), lambda k, b: (0, k))],
        out_specs=[pl.BlockSpec((bt, L, LANES), lambda k, b: (b, 0, k)),
                   pl.BlockSpec((bt, 1, 1, st), lambda k, b: (b, k, 0, 0))],
        out_shape=[jax.ShapeDtypeStruct((B, L, S5_WIDTH), F32),
                   jax.ShapeDtypeStruct((B, S5_NBLK, 1, st), F32)],
        scratch_shapes=[pltpu.VMEM((bt * n, tl), BF16),
                        pltpu.VMEM((bt * n, st), F32),
                        pltpu.VMEM((bt * n, st), F32)],
        compiler_params=_cparams("arbitrary", "arbitrary"),
        name="s5",
    )(u, h0, w, pmat, qmat, a_t, d_skip.reshape(1, S5_WIDTH))


def _s5_state_in(h_re, h_im):
    B = h_re.shape[0]
    r = h_re.reshape(B, S5_NBLK, 1, S5_BLK_STATE)
    i = h_im.reshape(B, S5_NBLK, 1, S5_BLK_STATE)
    return jnp.concatenate([r, i], -1)


def _s5_state_out(h):
    B = h.shape[0]
    h = h.reshape(B, S5_NBLK, 2, S5_BLK_GROUPS, S5_STATE)
    return (h[:, :, 0].reshape(B, S5_GROUPS, S5_STATE), h[:, :, 1].reshape(B, S5_GROUPS, S5_STATE))


CONV_TAIL = SUBLANES


def _inproj_kernel(x_ref, sh_ref, sc_ref, w_ref, cb_ref, cw_ref,
                   u_ref, act_ref, z_ref, ba_ref, cbo_ref, *xbuf_refs):
    bt, lt, d = x_ref.shape
    keep = CONV_WIDTH - 1
    o_q, o_z, o_b = S5_WIDTH, S5_WIDTH + CONV_CH, S5_WIDTH + CONV_CH + GDN_WIDTH
    parts = [slice(p * GDN_WIDTH, (p + 1) * GDN_WIDTH) for p in range(len(xbuf_refs))]

    @pl.when(pl.program_id(1) == 0)
    def _():
        for cs, xbuf_ref in zip(parts, xbuf_refs):
            xbuf_ref[:, 0:CONV_TAIL, :] = jnp.zeros((bt, CONV_TAIL, GDN_WIDTH), F32)
            xbuf_ref[:, CONV_TAIL - keep:CONV_TAIL, :] = cb_ref[:, :, cs]

    h = _ln(x_ref[...]) * (1.0 + sc_ref[:, 0]) + sh_ref[:, 0]
    h = h.reshape(bt * lt, d).astype(BF16)
    for cs, xbuf_ref in zip(parts, xbuf_refs):
        xbuf_ref[:, CONV_TAIL:CONV_TAIL + lt, :] = (
            _bdot(h, w_ref[:, o_q + cs.start:o_q + cs.stop]).reshape(bt, lt, GDN_WIDTH))
    u_ref[...] = _bdot(h, w_ref[:, :o_q]).reshape(bt, lt, S5_WIDTH)
    z_ref[...] = _bdot(h, w_ref[:, o_z:o_b]).reshape(bt, lt, GDN_WIDTH)
    ba_ref[...] = _bdot(h, w_ref[:, o_b:]).reshape(bt, lt, LANES)
    for cs, xbuf_ref in zip(parts, xbuf_refs):
        conv = None
        for j in range(CONV_WIDTH):
            off = CONV_TAIL - keep + j
            term = xbuf_ref[:, off:off + lt, :] * cw_ref[j:j + 1, cs]
            conv = term if conv is None else conv + term
        act_ref[:, :, cs] = _silu(conv)
        cbo_ref[:, :, cs] = xbuf_ref[:, lt + CONV_TAIL - keep:lt + CONV_TAIL, :]
        xbuf_ref[:, 0:CONV_TAIL, :] = xbuf_ref[:, lt:lt + CONV_TAIL, :]


def _inproj(x, mod, w_in_p, conv_buf, conv_w, bt, lt):
    B, L, _ = x.shape
    xs = lambda w: pl.BlockSpec((bt, lt, w), lambda b, i: (b, i, 0))
    ms = lambda idx: pl.BlockSpec((bt, 1, 1, D_MODEL), lambda b, i: (b, idx, 0, 0))
    cache = pl.BlockSpec((bt, CONV_WIDTH - 1, CONV_CH), lambda b, i: (b, 0, 0))
    return pl.pallas_call(
        _inproj_kernel,
        grid=(B // bt, L // lt),
        in_specs=[xs(D_MODEL), ms(0), ms(1),
                  pl.BlockSpec((D_MODEL, IN_PAD), lambda b, i: (0, 0)), cache,
                  pl.BlockSpec((CONV_WIDTH, CONV_CH), lambda b, i: (0, 0))],
        out_specs=[xs(S5_WIDTH), xs(CONV_CH), xs(GDN_WIDTH), xs(LANES), cache],
        out_shape=[jax.ShapeDtypeStruct((B, L, w), F32) for w in (S5_WIDTH, CONV_CH, GDN_WIDTH, LANES)]
                  + [jax.ShapeDtypeStruct((B, CONV_WIDTH - 1, CONV_CH), F32)],
        scratch_shapes=[pltpu.VMEM((bt, lt + CONV_TAIL, GDN_WIDTH), F32)
                        for _ in range(CONV_CH // GDN_WIDTH)],
        compiler_params=_cparams("arbitrary", "arbitrary"),
        name="inproj",
    )(x, mod, mod, w_in_p, conv_buf, conv_w)


def _mm(a, b):
    return jnp.dot(a.astype(BF16), b.astype(BF16), preferred_element_type=F32)


def _mm_nt(a, b):
    return lax.dot_general(a.astype(BF16), b.astype(BF16), (((1,), (1,)), ((), ())),
                           preferred_element_type=F32)


def _mm_tn(a, b):
    return lax.dot_general(a.astype(BF16), b.astype(BF16), (((0,), (0,)), ((), ())),
                           preferred_element_type=F32)


def _mm_hi(a, b):
    return jnp.dot(a, b, preferred_element_type=F32, precision=HI)


GDN_INV_BASE = 2
GDN_INV_JOIN = 4


def _bdot(a, b):
    return jnp.dot(a, b, preferred_element_type=F32)


def _unit_lower_inverse(a_list, ri, ci, size):
    eye_f = (ri == ci).astype(F32)
    blk = lambda s: (ri // s) == (ci // s)
    prev = min(GDN_INV_BASE, size)
    base_mask = blk(prev)
    d_list = [(eye_f - jnp.where(base_mask, a, 0.0)).astype(BF16) for a in a_list]
    while prev < size:
        cur = min(prev * GDN_INV_JOIN, size)
        off_mask = blk(cur) & jnp.logical_not(blk(prev))
        ms = [_bdot(d, jnp.where(off_mask, a, 0.0).astype(BF16)) for a, d in zip(a_list, d_list)]
        rs = [eye_f - m for m in ms]
        mps = [m.astype(BF16) for m in ms]
        span = 2
        while span < cur // prev:
            mps = [_bdot(mp, mp).astype(BF16) for mp in mps]
            rs = [r + _bdot(r.astype(BF16), mp) for r, mp in zip(rs, mps)]
            span *= 2
        d_list = [_bdot(r.astype(BF16), d).astype(BF16) for r, d in zip(rs, d_list)]
        prev = cur
    return d_list


def _gdn_kernel(act_ref, z_ref, ba_ref, s0_ref, alog_ref, dtb_ref, ng_ref,
                y_ref, sT_ref, s_ref, *, chunk, bt):
    C, Dh, H = chunk, GDN_HEAD_DIM, GDN_HEADS

    @pl.when(pl.program_id(1) == 0)
    def _():
        s_ref[...] = s0_ref[...]

    act_all = act_ref[...]
    ba = ba_ref[...]
    beta_all = _sigmoid(ba)
    xa = ba + dtb_ref[...]
    softplus = jnp.maximum(xa, 0.0) + jnp.log1p(jnp.exp(-jnp.abs(xa)))
    g = -jnp.exp(alog_ref[...]) * softplus

    ri = lax.broadcasted_iota(jnp.int32, (C, C), 0)
    ci = lax.broadcasted_iota(jnp.int32, (C, C), 1)
    causal = ri >= ci
    strict = ri > ci
    eye = ri == ci
    ng = ng_ref[...]
    tril = causal.astype(BF16)
    decay_all = []
    for b in range(bt):
        g_hi = g[b].astype(BF16)
        g_rest = g[b] - g_hi.astype(F32)
        g_mid = g_rest.astype(BF16)
        g_lo = (g_rest - g_mid.astype(F32)).astype(BF16)
        d3 = _bdot(tril, jnp.concatenate([g_hi, g_mid, g_lo], axis=1))
        decay_all.append(d3[:, :LANES] + (d3[:, LANES:2 * LANES] + d3[:, 2 * LANES:]))
    units = [(b, h) for b in range(bt) for h in range(H)]

    e_list, kbq_list, kT_list, rhs_list, qd_list, kdT_list, gl_list = [], [], [], [], [], [], []
    for b, h in units:
        decay, beta, act = decay_all[b], beta_all[b], act_all[b]
        dcol = decay[:, H + h:H + h + 1]
        dmat = jnp.broadcast_to(dcol, (C, C))
        drow = jnp.sum(jnp.where(eye, dmat, 0.0), axis=0, keepdims=True)
        e = jnp.exp(dmat - drow)
        bcol = beta[:, h:h + 1]
        q = act[:, h * Dh:(h + 1) * Dh]
        k = act[:, GDN_WIDTH + h * Dh:GDN_WIDTH + (h + 1) * Dh]
        v = act[:, 2 * GDN_WIDTH + h * Dh:2 * GDN_WIDTH + (h + 1) * Dh]
        q = q * (lax.rsqrt(jnp.sum(q * q, -1, keepdims=True) + RMS_EPS) * (Dh ** -0.5))
        k = k * lax.rsqrt(jnp.sum(k * k, -1, keepdims=True) + RMS_EPS)
        kb = k * bcol
        kT = k.T
        kT_b = kT.astype(BF16)
        edec = jnp.exp(dcol)
        dlast = decay[C - 1:C, H + h:H + h + 1]
        e_list.append(e)
        kbq_list.append(jnp.concatenate([kb, q], axis=0).astype(BF16))
        kT_list.append(kT_b)
        rhs_list.append(jnp.concatenate([v * bcol, kb * edec], axis=1).astype(BF16))
        qd_list.append((q * edec).astype(BF16))
        kdT_list.append((kT * jnp.exp(dlast - drow)).astype(BF16))
        gl_list.append(jnp.exp(dlast))
    sc_list = [_bdot(kbq, kT_b) for kbq, kT_b in zip(kbq_list, kT_list)]
    a_list = [jnp.where(strict, sc[:C] * e, 0.0) for sc, e in zip(sc_list, e_list)]
    qk_list = [jnp.where(causal, sc[C:] * e, 0.0).astype(BF16) for sc, e in zip(sc_list, e_list)]

    tinv_list = _unit_lower_inverse(a_list, ri, ci, C)

    uw_list = [_bdot(t, rhs) for t, rhs in zip(tinv_list, rhs_list)]
    s_list = [s_ref[b, h] for b, h in units]
    ws_list = [_bdot(jnp.concatenate([uw[:, Dh:].astype(BF16), qd], axis=0), s.astype(BF16))
               for uw, qd, s in zip(uw_list, qd_list, s_list)]
    vn_list = [(uw[:, :Dh] - ws[:C]).astype(BF16) for uw, ws in zip(uw_list, ws_list)]
    o_list = [ws[C:] + _bdot(qk, vn) for ws, qk, vn in zip(ws_list, qk_list, vn_list)]
    for i, (b, h) in enumerate(units):
        s_ref[b, h] = s_list[i] * gl_list[i] + _bdot(kdT_list[i], vn_list[i])
    for i, (b, h) in enumerate(units):
        o = o_list[i]
        o = o * lax.rsqrt(jnp.mean(o * o, -1, keepdims=True) + RMS_EPS) * ng
        y_ref[b, :, h * Dh:(h + 1) * Dh] = o * _silu(z_ref[b, :, h * Dh:(h + 1) * Dh])

    sT_ref[...] = s_ref[...]


def _gdn(act, z, ba, s0, a_log, dt_bias, norm_g, chunk, bt):
    B, L, _ = act.shape
    H, Dh = GDN_HEADS, GDN_HEAD_DIM
    lt = chunk
    place = lambda t: jnp.zeros((1, LANES), F32).at[0, H:2 * H].set(t)
    xs = lambda w: pl.BlockSpec((bt, lt, w), lambda b, i: (b, i, 0))
    full = lambda *shape: pl.BlockSpec(shape, lambda b, i: (0,) * len(shape))
    state = pl.BlockSpec((bt, H, Dh, Dh), lambda b, i: (b, 0, 0, 0))
    return pl.pallas_call(
        functools.partial(_gdn_kernel, chunk=chunk, bt=bt),
        grid=(B // bt, L // lt),
        in_specs=[xs(CONV_CH), xs(GDN_WIDTH), xs(LANES), state,
                  full(1, LANES), full(1, LANES), full(1, Dh)],
        out_specs=[xs(GDN_WIDTH), state],
        out_shape=[jax.ShapeDtypeStruct((B, L, GDN_WIDTH), F32),
                   jax.ShapeDtypeStruct((B, H, Dh, Dh), F32)],
        scratch_shapes=[pltpu.VMEM((bt, H, Dh, Dh), F32)],
        compiler_params=_cparams("arbitrary", "arbitrary"),
        name="gdn",
    )(act, z, ba, s0, place(a_log), place(dt_bias), norm_g.reshape(1, Dh))


FF_BLK = 256


def _gelu_tanh(x):
    return 0.5 * x * (1.0 + jnp.tanh(0.7978845608028654 * (x + 0.044715 * x * x * x)))


def _outffn_kernel(x_ref, ya_ref, yb_ref, g1_ref, sh2_ref, sc2_ref, g2_ref,
                   wglu_ref, wout_ref, l1g_ref, l1b_ref, wup_ref, wdn_ref, l2g_ref, l2b_ref,
                   o_ref):
    bt, lt, d = x_ref.shape
    if bt % 2 == 0:
        subs = [(slice(i * bt // 2, (i + 1) * bt // 2), slice(0, lt)) for i in range(2)]
    else:
        subs = [(slice(0, bt), slice(i * lt // 2, (i + 1) * lt // 2)) for i in range(2)]
    rd = lambda ref, sub: ref[sub[0], sub[1], :]
    flat = lambda t: t.reshape(t.shape[0] * t.shape[1], t.shape[-1])
    unflat = lambda t, like: t.reshape(like.shape[0], like.shape[1], t.shape[-1])
    mod = lambda ref, sub: ref[sub[0], 0]
    nsub = range(len(subs))

    zz = [_mm(_gelu_tanh(flat(rd(ya_ref, s))), wglu_ref[...]) for s in subs]
    y_a = [z[:, :S5_WIDTH] * _sigmoid(z[:, S5_WIDTH:]) for z in zz]
    mix_b = [_mm(flat(rd(yb_ref, s)), wout_ref[S5_WIDTH:, :]) for s in subs]
    mix = [mb + _mm(ya, wout_ref[:S5_WIDTH, :]) for mb, ya in zip(mix_b, y_a)]
    xs = [rd(x_ref, s) for s in subs]
    x1 = [_ln(ALPHA * x + (1.0 + mod(g1_ref, s)) * unflat(mx, x)) * l1g_ref[...] + l1b_ref[...]
          for x, mx, s in zip(xs, mix, subs)]
    h = [flat(_ln(x) * (1.0 + mod(sc2_ref, s)) + mod(sh2_ref, s)).astype(BF16) for x, s in zip(x1, subs)]

    def gate_up(i, j):
        gate = _bdot(h[i], wup_ref[:, j * FF_BLK:(j + 1) * FF_BLK])
        up = _bdot(h[i], wup_ref[:, D_FF + j * FF_BLK:D_FF + (j + 1) * FF_BLK])
        return gate, up

    nblk = D_FF // FF_BLK
    acc = [None for _ in nsub]
    gu = [gate_up(i, 0) for i in nsub]
    for j in range(nblk):
        gu_next = [gate_up(i, j + 1) for i in nsub] if j + 1 < nblk else None
        for i in nsub:
            gate, up = gu[i]
            part = _mm(_silu(gate) * up, wdn_ref[j * FF_BLK:(j + 1) * FF_BLK, :])
            acc[i] = part if acc[i] is None else acc[i] + part
        gu = gu_next
    for i, s in enumerate(subs):
        o_ref[s[0], s[1], :] = (_ln(ALPHA * x1[i] + (1.0 + mod(g2_ref, s)) * unflat(acc[i], x1[i]))
                                * l2g_ref[...] + l2b_ref[...])


def _outffn(x, ya, yb, mod, w_glu, w_out, ln1_g, ln1_b, w_up, w_dn, ln2_g, ln2_b, bt, lt):
    B, L, _ = x.shape
    xs = lambda w: pl.BlockSpec((bt, lt, w), lambda b, i: (b, i, 0))
    ms = lambda idx: pl.BlockSpec((bt, 1, 1, D_MODEL), lambda b, i: (b, idx, 0, 0))
    const = lambda a: pl.BlockSpec(a.shape, lambda b, i: (0,) * a.ndim, pipeline_mode=pl.Buffered(1))
    vec = lambda t: t.reshape(1, D_MODEL)
    weights = (w_glu, w_out, vec(ln1_g), vec(ln1_b), w_up, w_dn, vec(ln2_g), vec(ln2_b))
    return pl.pallas_call(
        _outffn_kernel,
        grid=(B // bt, L // lt),
        in_specs=[xs(D_MODEL), xs(S5_WIDTH), xs(GDN_WIDTH), ms(2), ms(3), ms(4), ms(5)]
                 + [const(a) for a in weights],
        out_specs=xs(D_MODEL),
        out_shape=jax.ShapeDtypeStruct((B, L, D_MODEL), F32),
        compiler_params=_cparams("arbitrary", "arbitrary"),
        name="outffn",
    )(x, ya, yb, mod, mod, mod, mod, *weights)


def _layer(x, mod, h_re, h_im, s_gdn, conv_buf, wts, bt, lt, gdn_chunk, gdn_bt):
    (w_in_p, s5_mats, s5_d, w_glu, conv_w, a_log, dt_bias, norm_g, w_out,
     ln1_g, ln1_b, w_up, w_dn, ln2_g, ln2_b) = wts
    u, act, z, ba, cbo = _inproj(x, mod, w_in_p, conv_buf, conv_w, bt, lt)
    w, pmat, qmat, a_t = s5_mats
    ys5, hT = _s5(u, _s5_state_in(h_re, h_im), w, pmat, qmat, a_t, s5_d, bt)
    yb, sT = _gdn(act, z, ba, s_gdn, a_log, dt_bias, norm_g, gdn_chunk, gdn_bt)
    y = _outffn(x, ys5, yb, mod, w_glu, w_out, ln1_g, ln1_b, w_up, w_dn, ln2_g, ln2_b, bt, lt)
    o_re, o_im = _s5_state_out(hT)
    return y, o_re, o_im, sT, cbo


def kernel(x_prompt, x_sample, state_s5_re, state_s5_im, state_gdn, cache_gdn_conv, c_prompt, c_sample, w_ada, b_ada, w_in, s5_a_re, s5_a_im, s5_log_dt, s5_b_re, s5_b_im, s5_c_re, s5_c_im, s5_d, w_s5_glu, gdn_conv_w, gdn_a_log, gdn_dt_bias, gdn_norm_g, w_out, ln1_g, ln1_b, w_ffn_up, w_ffn_down, ln2_g, ln2_b):
    bp, bs = x_prompt.shape[0], x_sample.shape[0]
    assert w_ada.shape[0] == DEPTH == 1
    l = 0
    c_all = jnp.concatenate([c_prompt, c_sample], 0)
    c_all = jnp.pad(c_all, ((0, -(bp + bs) % (2 * SUBLANES)), (0, 0)))
    mod = _ada(c_all, w_ada[l], b_ada[l])[:bp + bs]
    mod = mod.reshape(bp + bs, 6, 1, D_MODEL)
    w_in_p = jnp.pad(w_in[l], ((0, 0), (0, IN_PAD - w_in.shape[-1]))).astype(BF16)
    s5_mats = _s5_prep(s5_a_re[l], s5_a_im[l], s5_log_dt[l], s5_b_re[l], s5_b_im[l],
                       s5_c_re[l], s5_c_im[l])
    wts = (w_in_p, s5_mats, s5_d[l], w_s5_glu[l].astype(BF16), gdn_conv_w[l], gdn_a_log[l],
           gdn_dt_bias[l], gdn_norm_g[l], w_out[l].astype(BF16), ln1_g[l], ln1_b[l],
           w_ffn_up[l].astype(BF16), w_ffn_down[l].astype(BF16), ln2_g[l], ln2_b[l])
    zeros = lambda *s: jnp.zeros(s, F32)
    yp, p_re, p_im, p_gdn, p_conv = _layer(
        x_prompt, mod[:bp],
        zeros(bp, S5_GROUPS, S5_STATE), zeros(bp, S5_GROUPS, S5_STATE),
        zeros(bp, GDN_HEADS, GDN_HEAD_DIM, GDN_HEAD_DIM), zeros(bp, CONV_WIDTH - 1, CONV_CH),
        wts, bt=1, lt=512, gdn_chunk=128, gdn_bt=4)
    ys, s_re, s_im, s_gdn, s_conv = _layer(
        x_sample, mod[bp:], state_s5_re[l], state_s5_im[l], state_gdn[l], cache_gdn_conv[l],
        wts, bt=bs, lt=x_sample.shape[1], gdn_chunk=x_sample.shape[1], gdn_bt=4)
    st = lambda t: t[None]
    return (yp, ys, st(p_re), st(p_im), st(p_gdn), st(p_conv),
            st(s_re), st(s_im), st(s_gdn), st(s_conv))
```

```python
import functools

import jax
import jax.numpy as jnp
from jax import lax
from jax.experimental import pallas as pl
from jax.experimental.pallas import tpu as pltpu

F32 = jnp.float32
BF16 = jnp.bfloat16

D_MODEL = 1024
S5_WIDTH = 512
S5_GROUP = 16
S5_GROUPS = 32
S5_STATE = 64
GDN_WIDTH = 512
GDN_HEAD_DIM = 128
GDN_HEADS = 4
CONV_WIDTH = 4
CONV_CH = 3 * GDN_WIDTH
D_FF = 2816
LN_EPS = 1e-5
RMS_EPS = 1e-6
DEPTH = 1
ALPHA = (2 * DEPTH) ** 0.25

LANES = 128
MXU_TILE = 256
S5_T = 8
S5_BLK_GROUPS = LANES // S5_GROUP
S5_NBLK = S5_GROUPS // S5_BLK_GROUPS
S5_BLK_STATE = S5_BLK_GROUPS * S5_STATE
SUBLANES = 8
S5_POW_ROWS = 2 * SUBLANES
IN_PAD = S5_WIDTH + 4 * GDN_WIDTH + LANES
VMEM_LIMIT = 56 * 1024 * 1024


def _cparams(*sem):
    return pltpu.CompilerParams(dimension_semantics=sem, vmem_limit_bytes=VMEM_LIMIT)


def _ln(x):
    mu = jnp.mean(x, -1, keepdims=True)
    xc = x - mu
    var = jnp.mean(xc * xc, -1, keepdims=True)
    return xc * lax.rsqrt(var + LN_EPS)


def _sigmoid(x):
    return 1.0 / (1.0 + jnp.exp(-x))


def _silu(x):
    return x * _sigmoid(x)


def _split_bf16(x):
    hi = x.astype(BF16)
    return hi, (x - hi.astype(F32)).astype(BF16)


def _ada_kernel(c_ref, w_ref, b_ref, o_ref):
    c_hi, c_lo = _split_bf16(_silu(c_ref[...]))
    w_hi, w_lo = _split_bf16(w_ref[...])
    dot = functools.partial(jnp.dot, preferred_element_type=F32)
    o_ref[...] = dot(c_hi, w_hi) + (dot(c_hi, w_lo) + dot(c_lo, w_hi)) + b_ref[...]


def _ada(c, w_ada, b_ada):
    rows = c.shape[0]
    ncol = w_ada.shape[1]
    blk = D_MODEL
    return pl.pallas_call(
        _ada_kernel,
        grid=(ncol // blk,),
        in_specs=[pl.BlockSpec((rows, D_MODEL), lambda j: (0, 0)),
                  pl.BlockSpec((D_MODEL, blk), lambda j: (0, j)),
                  pl.BlockSpec((1, blk), lambda j: (0, j))],
        out_specs=pl.BlockSpec((rows, blk), lambda j: (0, j)),
        out_shape=jax.ShapeDtypeStruct((rows, ncol), F32),
        compiler_params=_cparams("arbitrary"),
        name="ada",
    )(c, w_ada, b_ada.reshape(1, ncol))


def _s5prep_kernel(are_ref, aim_ref, ldt_ref, btre_ref, btim_ref, ctre_ref, ctim_ref,
                   w_ref, p_ref, qt_ref, at_ref):
    T, half = S5_T, S5_BLK_STATE
    a_re = are_ref[0]
    a_im = aim_ref[0]
    dt = jnp.exp(ldt_ref[0])
    den = a_re * a_re + a_im * a_im

    def lam_pow(tau):
        mag = jnp.exp(a_re * dt * tau)
        return mag * jnp.cos(a_im * dt * tau), mag * jnp.sin(a_im * dt * tau)

    l_re, l_im = lam_pow(1.0)
    n_re, n_im = l_re - 1.0, l_im
    f_re = (n_re * a_re + n_im * a_im) / den
    f_im = (n_im * a_re - n_re * a_im) / den
    same_group = (lax.broadcasted_iota(jnp.int32, (LANES, half), 0) // S5_GROUP
                  == lax.broadcasted_iota(jnp.int32, (LANES, half), 1) // S5_STATE)
    bm_re = jnp.where(same_group, btre_ref[0], 0.0)
    bm_im = jnp.where(same_group, btim_ref[0], 0.0)
    cm_re = jnp.where(same_group, ctre_ref[0], 0.0)
    cm_im = jnp.where(same_group, ctim_ref[0], 0.0)
    nt = lambda a, b: lax.dot_general(a, b, (((1,), (1,)), ((), ())), preferred_element_type=F32)
    cm_re_s, cm_im_s = _split_bf16(cm_re), _split_bf16(cm_im)

    def dot_nt(a, b_split):
        a_hi, a_lo = _split_bf16(a)
        return nt(a_hi, b_split[0]) + (nt(a_hi, b_split[1]) + nt(a_lo, b_split[0]))
    blk = lambda i: slice(i * LANES, (i + 1) * LANES)
    for s in range(T):
        for t in range(s):
            w_ref[0, blk(s), blk(t)] = jnp.zeros((LANES, LANES), BF16)
    m_t = lax.broadcasted_iota(jnp.int32, (S5_POW_ROWS, half), 0).astype(F32) * float(T)
    mag = jnp.exp(a_re * dt * m_t)
    at_ref[0, :, :half] = mag * jnp.cos(a_im * dt * m_t)
    at_ref[0, :, half:] = mag * jnp.sin(a_im * dt * m_t)
    for tau in range(T + 1):
        p_re, p_im = (jnp.ones_like(a_re), jnp.zeros_like(a_re)) if tau == 0 else lam_pow(float(tau))
        if tau < T:
            g_re = p_re * f_re - p_im * f_im
            g_im = p_re * f_im + p_im * f_re
            inj_re = g_re * bm_re - g_im * bm_im
            inj_im = g_re * bm_im + g_im * bm_re
            s = T - 1 - tau
            p_ref[0, blk(s), :half] = inj_re.astype(BF16)
            p_ref[0, blk(s), half:] = inj_im.astype(BF16)
            k_tau = (dot_nt(inj_re, cm_re_s) - dot_nt(inj_im, cm_im_s)).astype(BF16)
            for s in range(T - tau):
                w_ref[0, blk(s), blk(s + tau)] = k_tau
        if tau >= 1:
            qt_ref[0, blk(tau - 1), :half] = (cm_re * p_re - cm_im * p_im).astype(BF16)
            qt_ref[0, blk(tau - 1), half:] = (-(cm_re * p_im + cm_im * p_re)).astype(BF16)


def _s5_prep(a_re, a_im, log_dt, b_re, b_im, c_re, c_im):
    NB, T, half = S5_NBLK, S5_T, S5_BLK_STATE
    tl, st = T * LANES, 2 * half
    lanes = lambda t: t.reshape(NB, 1, half)
    tiled = lambda t: jnp.tile(t.reshape(NB, LANES, S5_STATE), (1, 1, S5_BLK_GROUPS))
    row = pl.BlockSpec((1, 1, half), lambda k: (k, 0, 0))
    mat = pl.BlockSpec((1, LANES, half), lambda k: (k, 0, 0))
    return pl.pallas_call(
        _s5prep_kernel,
        grid=(NB,),
        in_specs=[row, row, row, mat, mat, mat, mat],
        out_specs=[pl.BlockSpec((1, tl, tl), lambda k: (k, 0, 0)),
                   pl.BlockSpec((1, tl, st), lambda k: (k, 0, 0)),
                   pl.BlockSpec((1, tl, st), lambda k: (k, 0, 0)),
                   pl.BlockSpec((1, S5_POW_ROWS, st), lambda k: (k, 0, 0))],
        out_shape=[jax.ShapeDtypeStruct((NB, tl, tl), BF16),
                   jax.ShapeDtypeStruct((NB, tl, st), BF16),
                   jax.ShapeDtypeStruct((NB, tl, st), BF16),
                   jax.ShapeDtypeStruct((NB, S5_POW_ROWS, st), F32)],
        compiler_params=_cparams("arbitrary"),
        name="s5prep",
    )(lanes(a_re), lanes(a_im), lanes(jnp.repeat(log_dt, S5_STATE)),
      tiled(jnp.swapaxes(b_re, 1, 2)), tiled(jnp.swapaxes(b_im, 1, 2)), tiled(c_re), tiled(c_im))


def _s5_kernel(u_ref, h0_ref, w_ref, p_ref, q_ref, at_ref, d_ref, y_ref, hT_ref,
               lhs_ref, x_ref, hp_ref, *, bt, n):
    T, half = S5_T, S5_BLK_STATE
    for b in range(bt):
        for t in range(T):
            lhs_ref[b * n:(b + 1) * n, t * LANES:(t + 1) * LANES] = (
                u_ref[b, pl.ds(t, n, stride=T), :].astype(BF16))
    lhs = lhs_ref[...]
    x_ref[...] = jnp.dot(lhs, p_ref[0], preferred_element_type=F32)
    cplx = lambda m: (at_ref[0, m:m + 1, :half], at_ref[0, m:m + 1, half:])
    rowi = lax.broadcasted_iota(jnp.int32, (SUBLANES, half), 0)
    shifts = []
    d = 1
    while d < SUBLANES:
        p_re, p_im = cplx(d)
        shifts.append((d, jnp.where(rowi >= d, p_re, 0.0), jnp.where(rowi >= d, p_im, 0.0)))
        d *= 2
    row_re = at_ref[0, 0:SUBLANES, :half]
    row_im = at_ref[0, 0:SUBLANES, half:]
    full_re, full_im = cplx(SUBLANES)
    y_in = jnp.concatenate(
        [jnp.dot(lhs[:, :c + MXU_TILE], w_ref[0, :c + MXU_TILE, c:c + MXU_TILE],
                 preferred_element_type=F32) for c in range(0, T * LANES, MXU_TILE)], axis=1)
    for b in range(bt):
        def tile(i, c):
            c_re, c_im = c
            rows = slice(b * n + i * SUBLANES, b * n + (i + 1) * SUBLANES)
            y_re = x_ref[rows, :half]
            y_im = x_ref[rows, half:]
            for d, m_re, m_im in shifts:
                s_re = pltpu.roll(y_re, d, axis=0)
                s_im = pltpu.roll(y_im, d, axis=0)
                y_re, y_im = y_re + (m_re * s_re - m_im * s_im), y_im + (m_re * s_im + m_im * s_re)
            e_re = jnp.where(rowi == 0, 0.0, pltpu.roll(y_re, 1, axis=0))
            e_im = jnp.where(rowi == 0, 0.0, pltpu.roll(y_im, 1, axis=0))
            hp_ref[rows, :half] = e_re + (row_re * c_re - row_im * c_im)
            hp_ref[rows, half:] = e_im + (row_re * c_im + row_im * c_re)
            last = SUBLANES - 1
            return (y_re[last:] + (full_re * c_re - full_im * c_im),
                    y_im[last:] + (full_re * c_im + full_im * c_re))
        c = (h0_ref[b, 0, :, :half], h0_ref[b, 0, :, half:])
        for i in range(n // SUBLANES):
            c = tile(i, c)
        hT_ref[b, 0, :, :half] = c[0]
        hT_ref[b, 0, :, half:] = c[1]
    y = (y_in
         + lax.dot_general(hp_ref[...].astype(BF16), q_ref[0], (((1,), (1,)), ((), ())),
                           preferred_element_type=F32))
    d = d_ref[...]
    for b in range(bt):
        for t in range(T):
            y_ref[b, pl.ds(t, n, stride=T), :] = (
                y[b * n:(b + 1) * n, t * LANES:(t + 1) * LANES]
                + d * u_ref[b, pl.ds(t, n, stride=T), :])


def _s5(u, h0, w, pmat, qmat, a_t, d_skip, bt):
    B, L, _ = u.shape
    n = L // S5_T
    tl, st = S5_T * LANES, 2 * S5_BLK_STATE
    assert n % SUBLANES == 0
    return pl.pallas_call(
        functools.partial(_s5_kernel, bt=bt, n=n),
        grid=(S5_NBLK, B // bt),
        in_specs=[pl.BlockSpec((bt, L, LANES), lambda k, b: (b, 0, k)),
                  pl.BlockSpec((bt, 1, 1, st), lambda k, b: (b, k, 0, 0)),
                  pl.BlockSpec((1, tl, tl), lambda k, b: (k, 0, 0)),
                  pl.BlockSpec((1, tl, st), lambda k, b: (k, 0, 0)),
                  pl.BlockSpec((1, tl, st), lambda k, b: (k, 0, 0)),
                  pl.BlockSpec((1, S5_POW_ROWS, st), lambda k, b: (k, 0, 0)),
                  pl.BlockSpec((1, LANES), lambda k, b: (0, k))],
        out_specs=[pl.BlockSpec((bt, L, LANES), lambda k, b: (b, 0, k)),
                   pl.BlockSpec((bt, 1, 1, st), lambda k, b: (b, k, 0, 0))],
        out_shape=[jax.ShapeDtypeStruct((B, L, S5_WIDTH), F32),
                   jax.ShapeDtypeStruct((B, S5_NBLK, 1, st), F32)],
        scratch_shapes=[pltpu.VMEM((bt * n, tl), BF16),
                        pltpu.VMEM((bt * n, st), F32),
                        pltpu.VMEM((bt * n, st), F32)],
        compiler_params=_cparams("arbitrary", "arbitrary"),
        name="s5",
    )(u, h0, w, pmat, qmat, a_t, d_skip.reshape(1, S5_WIDTH))


def _s5_state_in(h_re, h_im):
    B = h_re.shape[0]
    r = h_re.reshape(B, S5_NBLK, 1, S5_BLK_STATE)
    i = h_im.reshape(B, S5_NBLK, 1, S5_BLK_STATE)
    return jnp.concatenate([r, i], -1)


def _s5_state_out(h):
    B = h.shape[0]
    h = h.reshape(B, S5_NBLK, 2, S5_BLK_GROUPS, S5_STATE)
    return (h[:, :, 0].reshape(B, S5_GROUPS, S5_STATE), h[:, :, 1].reshape(B, S5_GROUPS, S5_STATE))


CONV_TAIL = SUBLANES


def _inproj_kernel(x_ref, sh_ref, sc_ref, w_ref, cb_ref, cw_ref,
                   u_ref, act_ref, z_ref, ba_ref, cbo_ref, *xbuf_refs):
    bt, lt, d = x_ref.shape
    keep = CONV_WIDTH - 1
    o_q, o_z, o_b = S5_WIDTH, S5_WIDTH + CONV_CH, S5_WIDTH + CONV_CH + GDN_WIDTH
    nparts = CONV_CH // GDN_WIDTH
    parts = [slice(p * GDN_WIDTH, (p + 1) * GDN_WIDTH) for p in range(nparts)]
    nun = len(xbuf_refs) // nparts
    ub = bt // nun
    units = [slice(i * ub, (i + 1) * ub) for i in range(nun)]
    xbufs = [xbuf_refs[i * nparts:(i + 1) * nparts] for i in range(nun)]

    @pl.when(pl.program_id(1) == 0)
    def _():
        for us, bufs in zip(units, xbufs):
            for cs, xbuf_ref in zip(parts, bufs):
                xbuf_ref[:, 0:CONV_TAIL, :] = jnp.zeros((ub, CONV_TAIL, GDN_WIDTH), F32)
                xbuf_ref[:, CONV_TAIL - keep:CONV_TAIL, :] = cb_ref[us, :, cs]

    hs = []
    for us in units:
        h = _ln(x_ref[us]) * (1.0 + sc_ref[us, 0]) + sh_ref[us, 0]
        hs.append(h.reshape(ub * lt, d).astype(BF16))
    for p, cs in enumerate(parts):
        for h, bufs in zip(hs, xbufs):
            bufs[p][:, CONV_TAIL:CONV_TAIL + lt, :] = (
                _bdot(h, w_ref[:, o_q + cs.start:o_q + cs.stop]).reshape(ub, lt, GDN_WIDTH))
    for us, h in zip(units, hs):
        u_ref[us] = _bdot(h, w_ref[:, :o_q]).reshape(ub, lt, S5_WIDTH)
    for us, h in zip(units, hs):
        z_ref[us] = _bdot(h, w_ref[:, o_z:o_b]).reshape(ub, lt, GDN_WIDTH)
        ba_ref[us] = _bdot(h, w_ref[:, o_b:]).reshape(ub, lt, LANES)
    rows = lt + CONV_TAIL
    for us, bufs in zip(units, xbufs):
        for cs, xbuf_ref in zip(parts, bufs):
            xfull = xbuf_ref[...]
            conv = xfull[:, CONV_TAIL:, :] * cw_ref[keep:keep + 1, cs]
            for j in range(keep):
                shifted = pltpu.roll(xfull, rows - (CONV_TAIL - keep + j), axis=1)
                conv = conv + shifted[:, :lt, :] * cw_ref[j:j + 1, cs]
            act_ref[us, :, cs] = _silu(conv)
            cbo_ref[us, :, cs] = xbuf_ref[:, lt + CONV_TAIL - keep:lt + CONV_TAIL, :]
            xbuf_ref[:, 0:CONV_TAIL, :] = xbuf_ref[:, lt:lt + CONV_TAIL, :]


def _inproj(x, mod, w_in_p, conv_buf, conv_w, bt, lt):
    B, L, _ = x.shape
    nun = 2 if bt % 2 == 0 else 1
    xs = lambda w: pl.BlockSpec((bt, lt, w), lambda b, i: (b, i, 0))
    ms = lambda idx: pl.BlockSpec((bt, 1, 1, D_MODEL), lambda b, i: (b, idx, 0, 0))
    cache = pl.BlockSpec((bt, CONV_WIDTH - 1, CONV_CH), lambda b, i: (b, 0, 0))
    return pl.pallas_call(
        _inproj_kernel,
        grid=(B // bt, L // lt),
        in_specs=[xs(D_MODEL), ms(0), ms(1),
                  pl.BlockSpec((D_MODEL, IN_PAD), lambda b, i: (0, 0)), cache,
                  pl.BlockSpec((CONV_WIDTH, CONV_CH), lambda b, i: (0, 0))],
        out_specs=[xs(S5_WIDTH), xs(CONV_CH), xs(GDN_WIDTH), xs(LANES), cache],
        out_shape=[jax.ShapeDtypeStruct((B, L, w), F32) for w in (S5_WIDTH, CONV_CH, GDN_WIDTH, LANES)]
                  + [jax.ShapeDtypeStruct((B, CONV_WIDTH - 1, CONV_CH), F32)],
        scratch_shapes=[pltpu.VMEM((bt // nun, lt + CONV_TAIL, GDN_WIDTH), F32)
                        for _ in range(nun * (CONV_CH // GDN_WIDTH))],
        compiler_params=_cparams("arbitrary", "arbitrary"),
        name="inproj",
    )(x, mod, mod, w_in_p, conv_buf, conv_w)


def _mm(a, b):
    return jnp.dot(a.astype(BF16), b.astype(BF16), preferred_element_type=F32)


GDN_INV_BASE = 2
GDN_INV_JOIN = 4


def _bdot(a, b):
    return jnp.dot(a, b, preferred_element_type=F32)


def _unit_lower_inverse(a_list, ri, ci, size):
    eye_f = (ri == ci).astype(F32)
    blk = lambda s: (ri // s) == (ci // s)
    prev = min(GDN_INV_BASE, size)
    base_mask = blk(prev)
    d_list = [(eye_f - jnp.where(base_mask, a, 0.0)).astype(BF16) for a in a_list]
    while prev < size:
        cur = min(prev * GDN_INV_JOIN, size)
        off_mask = blk(cur) & jnp.logical_not(blk(prev))
        ms = [_bdot(d, jnp.where(off_mask, a, 0.0).astype(BF16)) for a, d in zip(a_list, d_list)]
        rs = [eye_f - m for m in ms]
        mps = [m.astype(BF16) for m in ms]
        span = 2
        while span < cur // prev:
            mps = [_bdot(mp, mp).astype(BF16) for mp in mps]
            rs = [r + _bdot(r.astype(BF16), mp) for r, mp in zip(rs, mps)]
            span *= 2
        d_list = [_bdot(r.astype(BF16), d).astype(BF16) for r, d in zip(rs, d_list)]
        prev = cur
    return d_list


def _gdn_kernel(act_ref, z_ref, ba_ref, s0_ref, alog_ref, dtb_ref, ng_ref,
                y_ref, sT_ref, s_ref, *, chunk, bt):
    C, Dh, H = chunk, GDN_HEAD_DIM, GDN_HEADS

    @pl.when(pl.program_id(1) == 0)
    def _():
        s_ref[...] = s0_ref[...]

    act_all = act_ref[...]
    ba = ba_ref[...]
    beta_all = _sigmoid(ba)
    xa = ba + dtb_ref[...]
    softplus = jnp.maximum(xa, 0.0) + jnp.log1p(jnp.exp(-jnp.abs(xa)))
    g = -jnp.exp(alog_ref[...]) * softplus

    ri = lax.broadcasted_iota(jnp.int32, (C, C), 0)
    ci = lax.broadcasted_iota(jnp.int32, (C, C), 1)
    causal = ri >= ci
    strict = ri > ci
    eye = ri == ci
    ng = ng_ref[...]
    tril = causal.astype(BF16)
    decay_all = []
    for b in range(bt):
        g_hi = g[b].astype(BF16)
        g_rest = g[b] - g_hi.astype(F32)
        g_mid = g_rest.astype(BF16)
        g_lo = (g_rest - g_mid.astype(F32)).astype(BF16)
        d3 = _bdot(tril, jnp.concatenate([g_hi, g_mid, g_lo], axis=1))
        decay_all.append(d3[:, :LANES] + (d3[:, LANES:2 * LANES] + d3[:, 2 * LANES:]))
    units = [(b, h) for b in range(bt) for h in range(H)]

    e_list, kbq_list, kT_list, rhs_list, qd_list, kdT_list, gl_list = [], [], [], [], [], [], []
    for b, h in units:
        decay, beta, act = decay_all[b], beta_all[b], act_all[b]
        dcol = decay[:, H + h:H + h + 1]
        dmat = jnp.broadcast_to(dcol, (C, C))
        drow = jnp.sum(jnp.where(eye, dmat, 0.0), axis=0, keepdims=True)
        e = jnp.exp(dmat - drow)
        bcol = beta[:, h:h + 1]
        q = act[:, h * Dh:(h + 1) * Dh]
        k = act[:, GDN_WIDTH + h * Dh:GDN_WIDTH + (h + 1) * Dh]
        v = act[:, 2 * GDN_WIDTH + h * Dh:2 * GDN_WIDTH + (h + 1) * Dh]
        q = q * (lax.rsqrt(jnp.sum(q * q, -1, keepdims=True) + RMS_EPS) * (Dh ** -0.5))
        k = k * lax.rsqrt(jnp.sum(k * k, -1, keepdims=True) + RMS_EPS)
        kb = k * bcol
        kT = k.T
        kT_b = kT.astype(BF16)
        edec = jnp.exp(dcol)
        dlast = decay[C - 1:C, H + h:H + h + 1]
        e_list.append(e)
        kbq_list.append(jnp.concatenate([kb, q], axis=0).astype(BF16))
        kT_list.append(kT_b)
        rhs_list.append(jnp.concatenate([v * bcol, kb * edec], axis=1).astype(BF16))
        qd_list.append((q * edec).astype(BF16))
        kdT_list.append((kT * jnp.exp(dlast - drow)).astype(BF16))
        gl_list.append(jnp.exp(dlast))
    sc_list = [_bdot(kbq, kT_b) for kbq, kT_b in zip(kbq_list, kT_list)]
    a_list = [jnp.where(strict, sc[:C] * e, 0.0) for sc, e in zip(sc_list, e_list)]
    qk_list = [jnp.where(causal, sc[C:] * e, 0.0).astype(BF16) for sc, e in zip(sc_list, e_list)]

    tinv_list = _unit_lower_inverse(a_list, ri, ci, C)

    uw_list = [_bdot(t, rhs) for t, rhs in zip(tinv_list, rhs_list)]
    s_list = [s_ref[b, h] for b, h in units]
    ws_list = [_bdot(jnp.concatenate([uw[:, Dh:].astype(BF16), qd], axis=0), s.astype(BF16))
               for uw, qd, s in zip(uw_list, qd_list, s_list)]
    vn_list = [(uw[:, :Dh] - ws[:C]).astype(BF16) for uw, ws in zip(uw_list, ws_list)]
    o_list = [ws[C:] + _bdot(qk, vn) for ws, qk, vn in zip(ws_list, qk_list, vn_list)]
    for i, (b, h) in enumerate(units):
        s_ref[b, h] = s_list[i] * gl_list[i] + _bdot(kdT_list[i], vn_list[i])
    for i, (b, h) in enumerate(units):
        o = o_list[i]
        o = o * lax.rsqrt(jnp.mean(o * o, -1, keepdims=True) + RMS_EPS) * ng
        y_ref[b, :, h * Dh:(h + 1) * Dh] = o * _silu(z_ref[b, :, h * Dh:(h + 1) * Dh])

    sT_ref[...] = s_ref[...]


def _gdn(act, z, ba, s0, a_log, dt_bias, norm_g, chunk, bt):
    B, L, _ = act.shape
    H, Dh = GDN_HEADS, GDN_HEAD_DIM
    lt = chunk
    place = lambda t: jnp.zeros((1, LANES), F32).at[0, H:2 * H].set(t)
    xs = lambda w: pl.BlockSpec((bt, lt, w), lambda b, i: (b, i, 0))
    full = lambda *shape: pl.BlockSpec(shape, lambda b, i: (0,) * len(shape))
    state = pl.BlockSpec((bt, H, Dh, Dh), lambda b, i: (b, 0, 0, 0))
    return pl.pallas_call(
        functools.partial(_gdn_kernel, chunk=chunk, bt=bt),
        grid=(B // bt, L // lt),
        in_specs=[xs(CONV_CH), xs(GDN_WIDTH), xs(LANES), state,
                  full(1, LANES), full(1, LANES), full(1, Dh)],
        out_specs=[xs(GDN_WIDTH), state],
        out_shape=[jax.ShapeDtypeStruct((B, L, GDN_WIDTH), F32),
                   jax.ShapeDtypeStruct((B, H, Dh, Dh), F32)],
        scratch_shapes=[pltpu.VMEM((bt, H, Dh, Dh), F32)],
        compiler_params=_cparams("arbitrary", "arbitrary"),
        name="gdn",
    )(act, z, ba, s0, place(a_log), place(dt_bias), norm_g.reshape(1, Dh))


FF_BLK = 256


def _gelu_tanh(x):
    return 0.5 * x * (1.0 + jnp.tanh(0.7978845608028654 * (x + 0.044715 * x * x * x)))


def _outffn_kernel(x_ref, ya_ref, yb_ref, g1_ref, sh2_ref, sc2_ref, g2_ref,
                   wglu_ref, wout_ref, l1g_ref, l1b_ref, wup_ref, wdn_ref, l2g_ref, l2b_ref,
                   o_ref):
    bt, lt, d = x_ref.shape
    if bt % 2 == 0:
        subs = [(slice(i * bt // 2, (i + 1) * bt // 2), slice(0, lt)) for i in range(2)]
    else:
        subs = [(slice(0, bt), slice(i * lt // 2, (i + 1) * lt // 2)) for i in range(2)]
    rd = lambda ref, sub: ref[sub[0], sub[1], :]
    flat = lambda t: t.reshape(t.shape[0] * t.shape[1], t.shape[-1])
    unflat = lambda t, like: t.reshape(like.shape[0], like.shape[1], t.shape[-1])
    mod = lambda ref, sub: ref[sub[0], 0]
    nsub = range(len(subs))

    zz = [_mm(_gelu_tanh(flat(rd(ya_ref, s))), wglu_ref[...]) for s in subs]
    y_a = [z[:, :S5_WIDTH] * _sigmoid(z[:, S5_WIDTH:]) for z in zz]
    mix_b = [_mm(flat(rd(yb_ref, s)), wout_ref[S5_WIDTH:, :]) for s in subs]
    mix = [mb + _mm(ya, wout_ref[:S5_WIDTH, :]) for mb, ya in zip(mix_b, y_a)]
    xs = [rd(x_ref, s) for s in subs]
    x1 = [_ln(ALPHA * x + (1.0 + mod(g1_ref, s)) * unflat(mx, x)) * l1g_ref[...] + l1b_ref[...]
          for x, mx, s in zip(xs, mix, subs)]
    h = [flat(_ln(x) * (1.0 + mod(sc2_ref, s)) + mod(sh2_ref, s)).astype(BF16) for x, s in zip(x1, subs)]

    def gate_up(i, j):
        gate = _bdot(h[i], wup_ref[:, j * FF_BLK:(j + 1) * FF_BLK])
        up = _bdot(h[i], wup_ref[:, D_FF + j * FF_BLK:D_FF + (j + 1) * FF_BLK])
        return gate, up

    nblk = D_FF // FF_BLK
    acc = [None for _ in nsub]
    gu = [gate_up(i, 0) for i in nsub]
    for j in range(nblk):
        gu_next = [gate_up(i, j + 1) for i in nsub] if j + 1 < nblk else None
        for i in nsub:
            gate, up = gu[i]
            part = _mm(_silu(gate) * up, wdn_ref[j * FF_BLK:(j + 1) * FF_BLK, :])
            acc[i] = part if acc[i] is None else acc[i] + part
        gu = gu_next
    for i, s in enumerate(subs):
        o_ref[s[0], s[1], :] = (_ln(ALPHA * x1[i] + (1.0 + mod(g2_ref, s)) * unflat(acc[i], x1[i]))
                                * l2g_ref[...] + l2b_ref[...])


def _outffn(x, ya, yb, mod, w_glu, w_out, ln1_g, ln1_b, w_up, w_dn, ln2_g, ln2_b, bt, lt):
    B, L, _ = x.shape
    xs = lambda w: pl.BlockSpec((bt, lt, w), lambda b, i: (b, i, 0))
    ms = lambda idx: pl.BlockSpec((bt, 1, 1, D_MODEL), lambda b, i: (b, idx, 0, 0))
    const = lambda a: pl.BlockSpec(a.shape, lambda b, i: (0,) * a.ndim, pipeline_mode=pl.Buffered(1))
    vec = lambda t: t.reshape(1, D_MODEL)
    weights = (w_glu, w_out, vec(ln1_g), vec(ln1_b), w_up, w_dn, vec(ln2_g), vec(ln2_b))
    return pl.pallas_call(
        _outffn_kernel,
        grid=(B // bt, L // lt),
        in_specs=[xs(D_MODEL), xs(S5_WIDTH), xs(GDN_WIDTH), ms(2), ms(3), ms(4), ms(5)]
                 + [const(a) for a in weights],
        out_specs=xs(D_MODEL),
        out_shape=jax.ShapeDtypeStruct((B, L, D_MODEL), F32),
        compiler_params=_cparams("arbitrary", "arbitrary"),
        name="outffn",
    )(x, ya, yb, mod, mod, mod, mod, *weights)


def _layer(x, mod, h_re, h_im, s_gdn, conv_buf, wts, bt, lt, in_bt, gdn_chunk, gdn_bt):
    (w_in_p, s5_mats, s5_d, w_glu, conv_w, a_log, dt_bias, norm_g, w_out,
     ln1_g, ln1_b, w_up, w_dn, ln2_g, ln2_b) = wts
    u, act, z, ba, cbo = _inproj(x, mod, w_in_p, conv_buf, conv_w, in_bt, lt)
    w, pmat, qmat, a_t = s5_mats
    ys5, hT = _s5(u, _s5_state_in(h_re, h_im), w, pmat, qmat, a_t, s5_d, bt)
    yb, sT = _gdn(act, z, ba, s_gdn, a_log, dt_bias, norm_g, gdn_chunk, gdn_bt)
    y = _outffn(x, ys5, yb, mod, w_glu, w_out, ln1_g, ln1_b, w_up, w_dn, ln2_g, ln2_b, bt, lt)
    o_re, o_im = _s5_state_out(hT)
    return y, o_re, o_im, sT, cbo


def kernel(x_prompt, x_sample, state_s5_re, state_s5_im, state_gdn, cache_gdn_conv, c_prompt, c_sample, w_ada, b_ada, w_in, s5_a_re, s5_a_im, s5_log_dt, s5_b_re, s5_b_im, s5_c_re, s5_c_im, s5_d, w_s5_glu, gdn_conv_w, gdn_a_log, gdn_dt_bias, gdn_norm_g, w_out, ln1_g, ln1_b, w_ffn_up, w_ffn_down, ln2_g, ln2_b):
    bp, bs = x_prompt.shape[0], x_sample.shape[0]
    assert w_ada.shape[0] == DEPTH == 1
    l = 0
    c_all = jnp.concatenate([c_prompt, c_sample], 0)
    c_all = jnp.pad(c_all, ((0, -(bp + bs) % (2 * SUBLANES)), (0, 0)))
    mod = _ada(c_all, w_ada[l], b_ada[l])[:bp + bs]
    mod = mod.reshape(bp + bs, 6, 1, D_MODEL)
    w_in_p = jnp.pad(w_in[l], ((0, 0), (0, IN_PAD - w_in.shape[-1]))).astype(BF16)
    s5_mats = _s5_prep(s5_a_re[l], s5_a_im[l], s5_log_dt[l], s5_b_re[l], s5_b_im[l],
                       s5_c_re[l], s5_c_im[l])
    wts = (w_in_p, s5_mats, s5_d[l], w_s5_glu[l].astype(BF16), gdn_conv_w[l], gdn_a_log[l],
           gdn_dt_bias[l], gdn_norm_g[l], w_out[l].astype(BF16), ln1_g[l], ln1_b[l],
           w_ffn_up[l].astype(BF16), w_ffn_down[l].astype(BF16), ln2_g[l], ln2_b[l])
    zeros = lambda *s: jnp.zeros(s, F32)
    yp, p_re, p_im, p_gdn, p_conv = _layer(
        x_prompt, mod[:bp],
        zeros(bp, S5_GROUPS, S5_STATE), zeros(bp, S5_GROUPS, S5_STATE),
        zeros(bp, GDN_HEADS, GDN_HEAD_DIM, GDN_HEAD_DIM), zeros(bp, CONV_WIDTH - 1, CONV_CH),
        wts, bt=1, lt=512, in_bt=1, gdn_chunk=128, gdn_bt=4)
    ys, s_re, s_im, s_gdn, s_conv = _layer(
        x_sample, mod[bp:], state_s5_re[l], state_s5_im[l], state_gdn[l], cache_gdn_conv[l],
        wts, bt=bs, lt=x_sample.shape[1], in_bt=bs, gdn_chunk=x_sample.shape[1], gdn_bt=4)
    st = lambda t: t[None]
    return (yp, ys, st(p_re), st(p_im), st(p_gdn), st(p_conv),
            st(s_re), st(s_im), st(s_gdn), st(s_conv))
```

```python
import functools

import jax
import jax.numpy as jnp
from jax import lax
from jax.experimental import pallas as pl
from jax.experimental.pallas import tpu as pltpu

F32 = jnp.float32
BF16 = jnp.bfloat16

D_MODEL = 1024
S5_WIDTH = 512
S5_GROUP = 16
S5_GROUPS = 32
S5_STATE = 64
GDN_WIDTH = 512
GDN_HEAD_DIM = 128
GDN_HEADS = 4
CONV_WIDTH = 4
CONV_CH = 3 * GDN_WIDTH
D_FF = 2816
LN_EPS = 1e-5
RMS_EPS = 1e-6
DEPTH = 1
ALPHA = (2 * DEPTH) ** 0.25

LANES = 128
MXU_TILE = 256
S5_T = 8
S5_BLK_GROUPS = LANES // S5_GROUP
S5_NBLK = S5_GROUPS // S5_BLK_GROUPS
S5_BLK_STATE = S5_BLK_GROUPS * S5_STATE
SUBLANES = 8
S5_POW_ROWS = 2 * SUBLANES
IN_PAD = S5_WIDTH + 4 * GDN_WIDTH + LANES
VMEM_LIMIT = 56 * 1024 * 1024


def _cparams(*sem):
    return pltpu.CompilerParams(dimension_semantics=sem, vmem_limit_bytes=VMEM_LIMIT)


def _ln(x):
    mu = jnp.mean(x, -1, keepdims=True)
    xc = x - mu
    var = jnp.mean(xc * xc, -1, keepdims=True)
    return xc * lax.rsqrt(var + LN_EPS)


def _sigmoid(x):
    return 1.0 / (1.0 + jnp.exp(-x))


def _silu(x):
    return x * _sigmoid(x)


def _split_bf16(x):
    hi = x.astype(BF16)
    return hi, (x - hi.astype(F32)).astype(BF16)


def _ada_kernel(c_ref, w_ref, b_ref, o_ref):
    c_hi, c_lo = _split_bf16(_silu(c_ref[...]))
    w_hi, w_lo = _split_bf16(w_ref[...])
    dot = functools.partial(jnp.dot, preferred_element_type=F32)
    o_ref[...] = dot(c_hi, w_hi) + (dot(c_hi, w_lo) + dot(c_lo, w_hi)) + b_ref[...]


def _ada(c, w_ada, b_ada):
    rows = c.shape[0]
    ncol = w_ada.shape[1]
    blk = D_MODEL
    return pl.pallas_call(
        _ada_kernel,
        grid=(ncol // blk,),
        in_specs=[pl.BlockSpec((rows, D_MODEL), lambda j: (0, 0)),
                  pl.BlockSpec((D_MODEL, blk), lambda j: (0, j)),
                  pl.BlockSpec((1, blk), lambda j: (0, j))],
        out_specs=pl.BlockSpec((rows, blk), lambda j: (0, j)),
        out_shape=jax.ShapeDtypeStruct((rows, ncol), F32),
        compiler_params=_cparams("arbitrary"),
        name="ada",
    )(c, w_ada, b_ada.reshape(1, ncol))


def _s5prep_kernel(are_ref, aim_ref, ldt_ref, btre_ref, btim_ref, ctre_ref, ctim_ref,
                   w_ref, p_ref, qt_ref, at_ref):
    T, half = S5_T, S5_BLK_STATE
    a_re = are_ref[0]
    a_im = aim_ref[0]
    dt = jnp.exp(ldt_ref[0])
    den = a_re * a_re + a_im * a_im

    def lam_pow(tau):
        mag = jnp.exp(a_re * dt * tau)
        return mag * jnp.cos(a_im * dt * tau), mag * jnp.sin(a_im * dt * tau)

    l_re, l_im = lam_pow(1.0)
    n_re, n_im = l_re - 1.0, l_im
    f_re = (n_re * a_re + n_im * a_im) / den
    f_im = (n_im * a_re - n_re * a_im) / den
    same_group = (lax.broadcasted_iota(jnp.int32, (LANES, half), 0) // S5_GROUP
                  == lax.broadcasted_iota(jnp.int32, (LANES, half), 1) // S5_STATE)
    bm_re = jnp.where(same_group, btre_ref[0], 0.0)
    bm_im = jnp.where(same_group, btim_ref[0], 0.0)
    cm_re = jnp.where(same_group, ctre_ref[0], 0.0)
    cm_im = jnp.where(same_group, ctim_ref[0], 0.0)
    nt = lambda a, b: lax.dot_general(a, b, (((1,), (1,)), ((), ())), preferred_element_type=F32)
    cm_re_s, cm_im_s = _split_bf16(cm_re), _split_bf16(cm_im)

    def dot_nt(a, b_split):
        a_hi, a_lo = _split_bf16(a)
        return nt(a_hi, b_split[0]) + (nt(a_hi, b_split[1]) + nt(a_lo, b_split[0]))
    blk = lambda i: slice(i * LANES, (i + 1) * LANES)
    for s in range(T):
        for t in range(s):
            w_ref[0, blk(s), blk(t)] = jnp.zeros((LANES, LANES), BF16)
    m_t = lax.broadcasted_iota(jnp.int32, (S5_POW_ROWS, half), 0).astype(F32) * float(T)
    mag = jnp.exp(a_re * dt * m_t)
    at_ref[0, :, :half] = mag * jnp.cos(a_im * dt * m_t)
    at_ref[0, :, half:] = mag * jnp.sin(a_im * dt * m_t)
    for tau in range(T + 1):
        p_re, p_im = (jnp.ones_like(a_re), jnp.zeros_like(a_re)) if tau == 0 else lam_pow(float(tau))
        if tau < T:
            g_re = p_re * f_re - p_im * f_im
            g_im = p_re * f_im + p_im * f_re
            inj_re = g_re * bm_re - g_im * bm_im
            inj_im = g_re * bm_im + g_im * bm_re
            s = T - 1 - tau
            p_ref[0, blk(s), :half] = inj_re.astype(BF16)
            p_ref[0, blk(s), half:] = inj_im.astype(BF16)
            k_tau = (dot_nt(inj_re, cm_re_s) - dot_nt(inj_im, cm_im_s)).astype(BF16)
            for s in range(T - tau):
                w_ref[0, blk(s), blk(s + tau)] = k_tau
        if tau >= 1:
            qt_ref[0, blk(tau - 1), :half] = (cm_re * p_re - cm_im * p_im).astype(BF16)
            qt_ref[0, blk(tau - 1), half:] = (-(cm_re * p_im + cm_im * p_re)).astype(BF16)


def _s5_prep(a_re, a_im, log_dt, b_re, b_im, c_re, c_im):
    NB, T, half = S5_NBLK, S5_T, S5_BLK_STATE
    tl, st = T * LANES, 2 * half
    lanes = lambda t: t.reshape(NB, 1, half)
    tiled = lambda t: jnp.tile(t.reshape(NB, LANES, S5_STATE), (1, 1, S5_BLK_GROUPS))
    row = pl.BlockSpec((1, 1, half), lambda k: (k, 0, 0))
    mat = pl.BlockSpec((1, LANES, half), lambda k: (k, 0, 0))
    return pl.pallas_call(
        _s5prep_kernel,
        grid=(NB,),
        in_specs=[row, row, row, mat, mat, mat, mat],
        out_specs=[pl.BlockSpec((1, tl, tl), lambda k: (k, 0, 0)),
                   pl.BlockSpec((1, tl, st), lambda k: (k, 0, 0)),
                   pl.BlockSpec((1, tl, st), lambda k: (k, 0, 0)),
                   pl.BlockSpec((1, S5_POW_ROWS, st), lambda k: (k, 0, 0))],
        out_shape=[jax.ShapeDtypeStruct((NB, tl, tl), BF16),
                   jax.ShapeDtypeStruct((NB, tl, st), BF16),
                   jax.ShapeDtypeStruct((NB, tl, st), BF16),
                   jax.ShapeDtypeStruct((NB, S5_POW_ROWS, st), F32)],
        compiler_params=_cparams("arbitrary"),
        name="s5prep",
    )(lanes(a_re), lanes(a_im), lanes(jnp.repeat(log_dt, S5_STATE)),
      tiled(jnp.swapaxes(b_re, 1, 2)), tiled(jnp.swapaxes(b_im, 1, 2)), tiled(c_re), tiled(c_im))


def _s5_kernel(u_ref, h0_ref, w_ref, p_ref, q_ref, at_ref, d_ref, y_ref, hT_ref,
               lhs_ref, x_ref, hp_ref, *, bt, n):
    T, half = S5_T, S5_BLK_STATE
    for b in range(bt):
        for t in range(T):
            lhs_ref[b * n:(b + 1) * n, t * LANES:(t + 1) * LANES] = (
                u_ref[b, pl.ds(t, n, stride=T), :].astype(BF16))
    lhs = lhs_ref[...]
    x_ref[...] = jnp.dot(lhs, p_ref[0], preferred_element_type=F32)
    cplx = lambda m: (at_ref[0, m:m + 1, :half], at_ref[0, m:m + 1, half:])
    rowi = lax.broadcasted_iota(jnp.int32, (SUBLANES, half), 0)
    shifts = []
    d = 1
    while d < SUBLANES:
        p_re, p_im = cplx(d)
        shifts.append((d, jnp.where(rowi >= d, p_re, 0.0), jnp.where(rowi >= d, p_im, 0.0)))
        d *= 2
    row_re = at_ref[0, 0:SUBLANES, :half]
    row_im = at_ref[0, 0:SUBLANES, half:]
    full_re, full_im = cplx(SUBLANES)
    y_in = jnp.concatenate(
        [jnp.dot(lhs[:, :c + MXU_TILE], w_ref[0, :c + MXU_TILE, c:c + MXU_TILE],
                 preferred_element_type=F32) for c in range(0, T * LANES, MXU_TILE)], axis=1)
    for b in range(bt):
        def tile(i, c):
            c_re, c_im = c
            rows = slice(b * n + i * SUBLANES, b * n + (i + 1) * SUBLANES)
            y_re = x_ref[rows, :half]
            y_im = x_ref[rows, half:]
            for d, m_re, m_im in shifts:
                s_re = pltpu.roll(y_re, d, axis=0)
                s_im = pltpu.roll(y_im, d, axis=0)
                y_re, y_im = y_re + (m_re * s_re - m_im * s_im), y_im + (m_re * s_im + m_im * s_re)
            e_re = jnp.where(rowi == 0, 0.0, pltpu.roll(y_re, 1, axis=0))
            e_im = jnp.where(rowi == 0, 0.0, pltpu.roll(y_im, 1, axis=0))
            hp_ref[rows, :half] = e_re + (row_re * c_re - row_im * c_im)
            hp_ref[rows, half:] = e_im + (row_re * c_im + row_im * c_re)
            last = SUBLANES - 1
            return (y_re[last:] + (full_re * c_re - full_im * c_im),
                    y_im[last:] + (full_re * c_im + full_im * c_re))
        c = (h0_ref[b, 0, :, :half], h0_ref[b, 0, :, half:])
        for i in range(n // SUBLANES):
            c = tile(i, c)
        hT_ref[b, 0, :, :half] = c[0]
        hT_ref[b, 0, :, half:] = c[1]
    y = (y_in
         + lax.dot_general(hp_ref[...].astype(BF16), q_ref[0], (((1,), (1,)), ((), ())),
                           preferred_element_type=F32))
    d = d_ref[...]
    for b in range(bt):
        for t in range(T):
            y_ref[b, pl.ds(t, n, stride=T), :] = (
                y[b * n:(b + 1) * n, t * LANES:(t + 1) * LANES]
                + d * u_ref[b, pl.ds(t, n, stride=T), :])


def _s5(u, h0, w, pmat, qmat, a_t, d_skip, bt):
    B, L, _ = u.shape
    n = L // S5_T
    tl, st = S5_T * LANES, 2 * S5_BLK_STATE
    assert n % SUBLANES == 0
    return pl.pallas_call(
        functools.partial(_s5_kernel, bt=bt, n=n),
        grid=(S5_NBLK, B // bt),
        in_specs=[pl.BlockSpec((bt, L, LANES), lambda k, b: (b, 0, k)),
                  pl.BlockSpec((bt, 1, 1, st), lambda k, b: (b, k, 0, 0)),
                  pl.BlockSpec((1, tl, tl), lambda k, b: (k, 0, 0)),
                  pl.BlockSpec((1, tl, st), lambda k, b: (k, 0, 0)),
                  pl.BlockSpec((1, tl, st), lambda k, b: (k, 0, 0)),
                  pl.BlockSpec((1, S5_POW_ROWS, st), lambda k, b: (k, 0, 0)),
                  pl.BlockSpec((1, LANES), lambda k, b: (0, k))],
        out_specs=[pl.BlockSpec((bt, L, LANES), lambda k, b: (b, 0, k)),
                   pl.BlockSpec((bt, 1, 1, st), lambda k, b: (b, k, 0, 0))],
        out_shape=[jax.ShapeDtypeStruct((B, L, S5_WIDTH), F32),
                   jax.ShapeDtypeStruct((B, S5_NBLK, 1, st), F32)],
        scratch_shapes=[pltpu.VMEM((bt * n, tl), BF16),
                        pltpu.VMEM((bt * n, st), F32),
                        pltpu.VMEM((bt * n, st), F32)],
        compiler_params=_cparams("arbitrary", "arbitrary"),
        name="s5",
    )(u, h0, w, pmat, qmat, a_t, d_skip.reshape(1, S5_WIDTH))


def _s5_state_in(h_re, h_im):
    B = h_re.shape[0]
    r = h_re.reshape(B, S5_NBLK, 1, S5_BLK_STATE)
    i = h_im.reshape(B, S5_NBLK, 1, S5_BLK_STATE)
    return jnp.concatenate([r, i], -1)


def _s5_state_out(h):
    B = h.shape[0]
    h = h.reshape(B, S5_NBLK, 2, S5_BLK_GROUPS, S5_STATE)
    return (h[:, :, 0].reshape(B, S5_GROUPS, S5_STATE), h[:, :, 1].reshape(B, S5_GROUPS, S5_STATE))


CONV_TAIL = SUBLANES


def _inproj_kernel(x_ref, sh_ref, sc_ref, w_ref, cb_ref, cw_ref,
                   u_ref, act_ref, z_ref, ba_ref, cbo_ref, *xbuf_refs):
    bt, lt, d = x_ref.shape
    keep = CONV_WIDTH - 1
    o_q, o_z, o_b = S5_WIDTH, S5_WIDTH + CONV_CH, S5_WIDTH + CONV_CH + GDN_WIDTH
    nparts = CONV_CH // GDN_WIDTH
    parts = [slice(p * GDN_WIDTH, (p + 1) * GDN_WIDTH) for p in range(nparts)]
    nun = len(xbuf_refs) // nparts
    ub = bt // nun
    units = [slice(i * ub, (i + 1) * ub) for i in range(nun)]
    xbufs = [xbuf_refs[i * nparts:(i + 1) * nparts] for i in range(nun)]

    @pl.when(pl.program_id(1) == 0)
    def _():
        for us, bufs in zip(units, xbufs):
            for cs, xbuf_ref in zip(parts, bufs):
                xbuf_ref[:, 0:CONV_TAIL, :] = jnp.zeros((ub, CONV_TAIL, GDN_WIDTH), F32)
                xbuf_ref[:, CONV_TAIL - keep:CONV_TAIL, :] = cb_ref[us, :, cs]

    hs = []
    for us in units:
        h = _ln(x_ref[us]) * (1.0 + sc_ref[us, 0]) + sh_ref[us, 0]
        hs.append(h.reshape(ub * lt, d).astype(BF16))
    for p, cs in enumerate(parts):
        for h, bufs in zip(hs, xbufs):
            bufs[p][:, CONV_TAIL:CONV_TAIL + lt, :] = (
                _bdot(h, w_ref[:, o_q + cs.start:o_q + cs.stop]).reshape(ub, lt, GDN_WIDTH))
    for us, h in zip(units, hs):
        u_ref[us] = _bdot(h, w_ref[:, :o_q]).reshape(ub, lt, S5_WIDTH)
    for us, h in zip(units, hs):
        z_ref[us] = _bdot(h, w_ref[:, o_z:o_b]).reshape(ub, lt, GDN_WIDTH)
        ba_ref[us] = _bdot(h, w_ref[:, o_b:]).reshape(ub, lt, LANES)
    rows = lt + CONV_TAIL
    for us, bufs in zip(units, xbufs):
        for cs, xbuf_ref in zip(parts, bufs):
            xfull = xbuf_ref[...]
            conv = xfull[:, CONV_TAIL:, :] * cw_ref[keep:keep + 1, cs]
            for j in range(keep):
                shifted = pltpu.roll(xfull, rows - (CONV_TAIL - keep + j), axis=1)
                conv = conv + shifted[:, :lt, :] * cw_ref[j:j + 1, cs]
            act_ref[us, :, cs] = _silu(conv)
            cbo_ref[us, :, cs] = xbuf_ref[:, lt + CONV_TAIL - keep:lt + CONV_TAIL, :]
            xbuf_ref[:, 0:CONV_TAIL, :] = xbuf_ref[:, lt:lt + CONV_TAIL, :]


def _inproj(x, mod, w_in_p, conv_buf, conv_w, bt, lt):
    B, L, _ = x.shape
    nun = 2 if bt % 2 == 0 else 1
    xs = lambda w: pl.BlockSpec((bt, lt, w), lambda b, i: (b, i, 0))
    ms = lambda idx: pl.BlockSpec((bt, 1, 1, D_MODEL), lambda b, i: (b, idx, 0, 0))
    cache = pl.BlockSpec((bt, CONV_WIDTH - 1, CONV_CH), lambda b, i: (b, 0, 0))
    return pl.pallas_call(
        _inproj_kernel,
        grid=(B // bt, L // lt),
        in_specs=[xs(D_MODEL), ms(0), ms(1),
                  pl.BlockSpec((D_MODEL, IN_PAD), lambda b, i: (0, 0)), cache,
                  pl.BlockSpec((CONV_WIDTH, CONV_CH), lambda b, i: (0, 0))],
        out_specs=[xs(S5_WIDTH), xs(CONV_CH), xs(GDN_WIDTH), xs(LANES), cache],
        out_shape=[jax.ShapeDtypeStruct((B, L, w), F32) for w in (S5_WIDTH, CONV_CH, GDN_WIDTH, LANES)]
                  + [jax.ShapeDtypeStruct((B, CONV_WIDTH - 1, CONV_CH), F32)],
        scratch_shapes=[pltpu.VMEM((bt // nun, lt + CONV_TAIL, GDN_WIDTH), F32)
                        for _ in range(nun * (CONV_CH // GDN_WIDTH))],
        compiler_params=_cparams("arbitrary", "arbitrary"),
        name="inproj",
    )(x, mod, mod, w_in_p, conv_buf, conv_w)


def _mm(a, b):
    return jnp.dot(a.astype(BF16), b.astype(BF16), preferred_element_type=F32)


GDN_INV_BASE = 2
GDN_INV_JOIN = 4


def _bdot(a, b):
    return jnp.dot(a, b, preferred_element_type=F32)


def _unit_lower_inverse(a_list, ri, ci, size):
    eye_f = (ri == ci).astype(F32)
    blk = lambda s: (ri // s) == (ci // s)
    prev = min(GDN_INV_BASE, size)
    base_mask = blk(prev)
    d_list = [(eye_f - jnp.where(base_mask, a, 0.0)).astype(BF16) for a in a_list]
    while prev < size:
        cur = min(prev * GDN_INV_JOIN, size)
        off_mask = blk(cur) & jnp.logical_not(blk(prev))
        ms = [_bdot(d, jnp.where(off_mask, a, 0.0).astype(BF16)) for a, d in zip(a_list, d_list)]
        rs = [eye_f - m for m in ms]
        mps = [m.astype(BF16) for m in ms]
        span = 2
        while span < cur // prev:
            mps = [_bdot(mp, mp).astype(BF16) for mp in mps]
            rs = [r + _bdot(r.astype(BF16), mp) for r, mp in zip(rs, mps)]
            span *= 2
        d_list = [_bdot(r.astype(BF16), d).astype(BF16) for r, d in zip(rs, d_list)]
        prev = cur
    return d_list


def _gdn_kernel(act_ref, z_ref, ba_ref, s0_ref, alog_ref, dtb_ref, ng_ref,
                y_ref, sT_ref, s_ref, *, chunk, bt):
    C, Dh, H = chunk, GDN_HEAD_DIM, GDN_HEADS

    @pl.when(pl.program_id(1) == 0)
    def _():
        s_ref[...] = s0_ref[...]

    act_all = act_ref[...]
    ba = ba_ref[...]
    beta_all = _sigmoid(ba)
    xa = ba + dtb_ref[...]
    softplus = jnp.maximum(xa, 0.0) + jnp.log1p(jnp.exp(-jnp.abs(xa)))
    g = -jnp.exp(alog_ref[...]) * softplus

    ri = lax.broadcasted_iota(jnp.int32, (C, C), 0)
    ci = lax.broadcasted_iota(jnp.int32, (C, C), 1)
    causal = ri >= ci
    strict = ri > ci
    eye = ri == ci
    ng = ng_ref[...]
    tril = causal.astype(BF16)
    decay_all = []
    for b in range(bt):
        g_hi = g[b].astype(BF16)
        g_rest = g[b] - g_hi.astype(F32)
        g_mid = g_rest.astype(BF16)
        g_lo = (g_rest - g_mid.astype(F32)).astype(BF16)
        d3 = _bdot(tril, jnp.concatenate([g_hi, g_mid, g_lo], axis=1))
        decay_all.append(d3[:, :LANES] + (d3[:, LANES:2 * LANES] + d3[:, 2 * LANES:]))
    units = [(b, h) for b in range(bt) for h in range(H)]

    e_list, kbq_list, kT_list, rhs_list, qd_list, kdT_list, gl_list = [], [], [], [], [], [], []
    for b, h in units:
        decay, beta, act = decay_all[b], beta_all[b], act_all[b]
        dcol = decay[:, H + h:H + h + 1]
        dmat = jnp.broadcast_to(dcol, (C, C))
        drow = jnp.sum(jnp.where(eye, dmat, 0.0), axis=0, keepdims=True)
        e = jnp.exp(dmat - drow)
        bcol = beta[:, h:h + 1]
        q = act[:, h * Dh:(h + 1) * Dh]
        k = act[:, GDN_WIDTH + h * Dh:GDN_WIDTH + (h + 1) * Dh]
        v = act[:, 2 * GDN_WIDTH + h * Dh:2 * GDN_WIDTH + (h + 1) * Dh]
        q = q * (lax.rsqrt(jnp.sum(q * q, -1, keepdims=True) + RMS_EPS) * (Dh ** -0.5))
        k = k * lax.rsqrt(jnp.sum(k * k, -1, keepdims=True) + RMS_EPS)
        kb = k * bcol
        kT = k.T
        kT_b = kT.astype(BF16)
        edec = jnp.exp(dcol)
        dlast = decay[C - 1:C, H + h:H + h + 1]
        e_list.append(e)
        kbq_list.append(jnp.concatenate([kb, q], axis=0).astype(BF16))
        kT_list.append(kT_b)
        rhs_list.append(jnp.concatenate([v * bcol, kb * edec], axis=1).astype(BF16))
        qd_list.append((q * edec).astype(BF16))
        kdT_list.append((kT * jnp.exp(dlast - drow)).astype(BF16))
        gl_list.append(jnp.exp(dlast))
    sc_list = [_bdot(kbq, kT_b) for kbq, kT_b in zip(kbq_list, kT_list)]
    a_list = [jnp.where(strict, sc[:C] * e, 0.0) for sc, e in zip(sc_list, e_list)]
    qk_list = [jnp.where(causal, sc[C:] * e, 0.0).astype(BF16) for sc, e in zip(sc_list, e_list)]

    tinv_list = _unit_lower_inverse(a_list, ri, ci, C)

    uw_list = [_bdot(t, rhs) for t, rhs in zip(tinv_list, rhs_list)]
    s_list = [s_ref[b, h] for b, h in units]
    ws_list = [_bdot(jnp.concatenate([uw[:, Dh:].astype(BF16), qd], axis=0), s.astype(BF16))
               for uw, qd, s in zip(uw_list, qd_list, s_list)]
    vn_list = [(uw[:, :Dh] - ws[:C]).astype(BF16) for uw, ws in zip(uw_list, ws_list)]
    o_list = [ws[C:] + _bdot(qk, vn) for ws, qk, vn in zip(ws_list, qk_list, vn_list)]
    for i, (b, h) in enumerate(units):
        s_ref[b, h] = s_list[i] * gl_list[i] + _bdot(kdT_list[i], vn_list[i])
    for i, (b, h) in enumerate(units):
        o = o_list[i]
        o = o * lax.rsqrt(jnp.mean(o * o, -1, keepdims=True) + RMS_EPS) * ng
        y_ref[b, :, h * Dh:(h + 1) * Dh] = o * _silu(z_ref[b, :, h * Dh:(h + 1) * Dh])

    sT_ref[...] = s_ref[...]


def _gdn(act, z, ba, s0, a_log, dt_bias, norm_g, chunk, bt):
    B, L, _ = act.shape
    H, Dh = GDN_HEADS, GDN_HEAD_DIM
    lt = chunk
    place = lambda t: jnp.zeros((1, LANES), F32).at[0, H:2 * H].set(t)
    xs = lambda w: pl.BlockSpec((bt, lt, w), lambda b, i: (b, i, 0))
    full = lambda *shape: pl.BlockSpec(shape, lambda b, i: (0,) * len(shape))
    state = pl.BlockSpec((bt, H, Dh, Dh), lambda b, i: (b, 0, 0, 0))
    return pl.pallas_call(
        functools.partial(_gdn_kernel, chunk=chunk, bt=bt),
        grid=(B // bt, L // lt),
        in_specs=[xs(CONV_CH), xs(GDN_WIDTH), xs(LANES), state,
                  full(1, LANES), full(1, LANES), full(1, Dh)],
        out_specs=[xs(GDN_WIDTH), state],
        out_shape=[jax.ShapeDtypeStruct((B, L, GDN_WIDTH), F32),
                   jax.ShapeDtypeStruct((B, H, Dh, Dh), F32)],
        scratch_shapes=[pltpu.VMEM((bt, H, Dh, Dh), F32)],
        compiler_params=_cparams("arbitrary", "arbitrary"),
        name="gdn",
    )(act, z, ba, s0, place(a_log), place(dt_bias), norm_g.reshape(1, Dh))


FF_BLK = 256


def _gelu_tanh(x):
    return 0.5 * x * (1.0 + jnp.tanh(0.7978845608028654 * (x + 0.044715 * x * x * x)))


def _outffn_kernel(x_ref, ya_ref, yb_ref, g1_ref, sh2_ref, sc2_ref, g2_ref,
                   wglu_ref, wout_ref, l1g_ref, l1b_ref, wup_ref, wdn_ref, l2g_ref, l2b_ref,
                   o_ref):
    bt, lt, d = x_ref.shape
    if bt % 2 == 0:
        subs = [(slice(i * bt // 2, (i + 1) * bt // 2), slice(0, lt)) for i in range(2)]
    else:
        subs = [(slice(0, bt), slice(i * lt // 2, (i + 1) * lt // 2)) for i in range(2)]
    rd = lambda ref, sub: ref[sub[0], sub[1], :]
    flat = lambda t: t.reshape(t.shape[0] * t.shape[1], t.shape[-1])
    unflat = lambda t, like: t.reshape(like.shape[0], like.shape[1], t.shape[-1])
    mod = lambda ref, sub: ref[sub[0], 0]
    nsub = range(len(subs))

    zz = [_mm(_gelu_tanh(flat(rd(ya_ref, s))), wglu_ref[...]) for s in subs]
    y_a = [z[:, :S5_WIDTH] * _sigmoid(z[:, S5_WIDTH:]) for z in zz]
    mix_b = [_mm(flat(rd(yb_ref, s)), wout_ref[S5_WIDTH:, :]) for s in subs]
    mix = [mb + _mm(ya, wout_ref[:S5_WIDTH, :]) for mb, ya in zip(mix_b, y_a)]
    xs = [rd(x_ref, s) for s in subs]
    x1 = [_ln(ALPHA * x + (1.0 + mod(g1_ref, s)) * unflat(mx, x)) * l1g_ref[...] + l1b_ref[...]
          for x, mx, s in zip(xs, mix, subs)]
    h = [flat(_ln(x) * (1.0 + mod(sc2_ref, s)) + mod(sh2_ref, s)).astype(BF16) for x, s in zip(x1, subs)]

    def gate_up(i, j):
        gate = _bdot(h[i], wup_ref[:, j * FF_BLK:(j + 1) * FF_BLK])
        up = _bdot(h[i], wup_ref[:, D_FF + j * FF_BLK:D_FF + (j + 1) * FF_BLK])
        return gate, up

    nblk = D_FF // FF_BLK
    acc = [None for _ in nsub]
    gu = [gate_up(i, 0) for i in nsub]
    for j in range(nblk):
        gu_next = [gate_up(i, j + 1) for i in nsub] if j + 1 < nblk else None
        for i in nsub:
            gate, up = gu[i]
            part = _mm(_silu(gate) * up, wdn_ref[j * FF_BLK:(j + 1) * FF_BLK, :])
            acc[i] = part if acc[i] is None else acc[i] + part
        gu = gu_next
    for i, s in enumerate(subs):
        o_ref[s[0], s[1], :] = (_ln(ALPHA * x1[i] + (1.0 + mod(g2_ref, s)) * unflat(acc[i], x1[i]))
                                * l2g_ref[...] + l2b_ref[...])


def _outffn(x, ya, yb, mod, w_glu, w_out, ln1_g, ln1_b, w_up, w_dn, ln2_g, ln2_b, bt, lt):
    B, L, _ = x.shape
    xs = lambda w: pl.BlockSpec((bt, lt, w), lambda b, i: (b, i, 0))
    ms = lambda idx: pl.BlockSpec((bt, 1, 1, D_MODEL), lambda b, i: (b, idx, 0, 0))
    const = lambda a: pl.BlockSpec(a.shape, lambda b, i: (0,) * a.ndim, pipeline_mode=pl.Buffered(1))
    vec = lambda t: t.reshape(1, D_MODEL)
    weights = (w_glu, w_out, vec(ln1_g), vec(ln1_b), w_up, w_dn, vec(ln2_g), vec(ln2_b))
    return pl.pallas_call(
        _outffn_kernel,
        grid=(B // bt, L // lt),
        in_specs=[xs(D_MODEL), xs(S5_WIDTH), xs(GDN_WIDTH), ms(2), ms(3), ms(4), ms(5)]
                 + [const(a) for a in weights],
        out_specs=xs(D_MODEL),
        out_shape=jax.ShapeDtypeStruct((B, L, D_MODEL), F32),
        compiler_params=_cparams("arbitrary", "arbitrary"),
        name="outffn",
    )(x, ya, yb, mod, mod, mod, mod, *weights)


TOKEN_TILE = 2 * MXU_TILE
GDN_CHUNK = LANES
GDN_CHAINS_SEQ = 4


def _plan(B, L):
    if L >= TOKEN_TILE:
        bt, lt = 1, TOKEN_TILE
    else:
        bt, lt = min(B, TOKEN_TILE // L), L
    assert B % bt == 0 and L % lt == 0
    gdn_chunk = min(L, GDN_CHUNK)
    gdn_bt = min(B, GDN_CHAINS_SEQ)
    assert L % gdn_chunk == 0 and B % gdn_bt == 0
    return bt, lt, gdn_chunk, gdn_bt


def _layer(x, mod, h_re, h_im, s_gdn, conv_buf, wts):
    (w_in_p, s5_mats, s5_d, w_glu, conv_w, a_log, dt_bias, norm_g, w_out,
     ln1_g, ln1_b, w_up, w_dn, ln2_g, ln2_b) = wts
    bt, lt, gdn_chunk, gdn_bt = _plan(x.shape[0], x.shape[1])
    in_bt = bt
    u, act, z, ba, cbo = _inproj(x, mod, w_in_p, conv_buf, conv_w, in_bt, lt)
    w, pmat, qmat, a_t = s5_mats
    ys5, hT = _s5(u, _s5_state_in(h_re, h_im), w, pmat, qmat, a_t, s5_d, bt)
    yb, sT = _gdn(act, z, ba, s_gdn, a_log, dt_bias, norm_g, gdn_chunk, gdn_bt)
    y = _outffn(x, ys5, yb, mod, w_glu, w_out, ln1_g, ln1_b, w_up, w_dn, ln2_g, ln2_b, bt, lt)
    o_re, o_im = _s5_state_out(hT)
    return y, o_re, o_im, sT, cbo


def kernel(x_prompt, x_sample, state_s5_re, state_s5_im, state_gdn, cache_gdn_conv, c_prompt, c_sample, w_ada, b_ada, w_in, s5_a_re, s5_a_im, s5_log_dt, s5_b_re, s5_b_im, s5_c_re, s5_c_im, s5_d, w_s5_glu, gdn_conv_w, gdn_a_log, gdn_dt_bias, gdn_norm_g, w_out, ln1_g, ln1_b, w_ffn_up, w_ffn_down, ln2_g, ln2_b):
    bp, bs = x_prompt.shape[0], x_sample.shape[0]
    assert w_ada.shape[0] == DEPTH == 1
    l = 0
    c_all = jnp.concatenate([c_prompt, c_sample], 0)
    c_all = jnp.pad(c_all, ((0, -(bp + bs) % (2 * SUBLANES)), (0, 0)))
    mod = _ada(c_all, w_ada[l], b_ada[l])[:bp + bs]
    mod = mod.reshape(bp + bs, 6, 1, D_MODEL)
    w_in_p = jnp.pad(w_in[l], ((0, 0), (0, IN_PAD - w_in.shape[-1]))).astype(BF16)
    s5_mats = _s5_prep(s5_a_re[l], s5_a_im[l], s5_log_dt[l], s5_b_re[l], s5_b_im[l],
                       s5_c_re[l], s5_c_im[l])
    wts = (w_in_p, s5_mats, s5_d[l], w_s5_glu[l].astype(BF16), gdn_conv_w[l], gdn_a_log[l],
           gdn_dt_bias[l], gdn_norm_g[l], w_out[l].astype(BF16), ln1_g[l], ln1_b[l],
           w_ffn_up[l].astype(BF16), w_ffn_down[l].astype(BF16), ln2_g[l], ln2_b[l])
    zeros = lambda *s: jnp.zeros(s, F32)
    yp, p_re, p_im, p_gdn, p_conv = _layer(
        x_prompt, mod[:bp],
        zeros(bp, S5_GROUPS, S5_STATE), zeros(bp, S5_GROUPS, S5_STATE),
        zeros(bp, GDN_HEADS, GDN_HEAD_DIM, GDN_HEAD_DIM), zeros(bp, CONV_WIDTH - 1, CONV_CH),
        wts)
    ys, s_re, s_im, s_gdn, s_conv = _layer(
        x_sample, mod[bp:], state_s5_re[l], state_s5_im[l], state_gdn[l], cache_gdn_conv[l],
        wts)
    st = lambda t: t[None]
    return (yp, ys, st(p_re), st(p_im), st(p_gdn), st(p_conv),
            st(s_re), st(s_im), st(s_gdn), st(s_conv))
```

```python
import functools

import jax
import jax.numpy as jnp
from jax import lax
from jax.experimental import pallas as pl
from jax.experimental.pallas import tpu as pltpu

F32 = jnp.float32
BF16 = jnp.bfloat16

D_MODEL = 1024
S5_WIDTH = 512
S5_GROUP = 16
S5_GROUPS = 32
S5_STATE = 64
GDN_WIDTH = 512
GDN_HEAD_DIM = 128
GDN_HEADS = 4
CONV_WIDTH = 4
CONV_CH = 3 * GDN_WIDTH
D_FF = 2816
LN_EPS = 1e-5
RMS_EPS = 1e-6
DEPTH = 1
ALPHA = (2 * DEPTH) ** 0.25

LANES = 128
MXU_TILE = 256
S5_T = 8
S5_BLK_GROUPS = LANES // S5_GROUP
S5_NBLK = S5_GROUPS // S5_BLK_GROUPS
S5_BLK_STATE = S5_BLK_GROUPS * S5_STATE
SUBLANES = 8
S5_POW_ROWS = 2 * SUBLANES
IN_PAD = S5_WIDTH + 4 * GDN_WIDTH + LANES
VMEM_LIMIT = 56 * 1024 * 1024


def _cparams(*sem):
    return pltpu.CompilerParams(dimension_semantics=sem, vmem_limit_bytes=VMEM_LIMIT)


def _ln(x):
    mu = jnp.mean(x, -1, keepdims=True)
    xc = x - mu
    var = jnp.mean(xc * xc, -1, keepdims=True)
    return xc * lax.rsqrt(var + LN_EPS)


def _sigmoid(x):
    return 1.0 / (1.0 + jnp.exp(-x))


def _silu(x):
    return x * _sigmoid(x)


def _split_bf16(x):
    hi = x.astype(BF16)
    return hi, (x - hi.astype(F32)).astype(BF16)


def _ada_kernel(c_ref, w_ref, b_ref, o_ref):
    c_hi, c_lo = _split_bf16(_silu(c_ref[...]))
    w_hi, w_lo = _split_bf16(w_ref[...])
    dot = functools.partial(jnp.dot, preferred_element_type=F32)
    o_ref[...] = dot(c_hi, w_hi) + (dot(c_hi, w_lo) + dot(c_lo, w_hi)) + b_ref[...]


def _ada(c, w_ada, b_ada):
    rows = c.shape[0]
    ncol = w_ada.shape[1]
    blk = D_MODEL
    return pl.pallas_call(
        _ada_kernel,
        grid=(ncol // blk,),
        in_specs=[pl.BlockSpec((rows, D_MODEL), lambda j: (0, 0)),
                  pl.BlockSpec((D_MODEL, blk), lambda j: (0, j)),
                  pl.BlockSpec((1, blk), lambda j: (0, j))],
        out_specs=pl.BlockSpec((rows, blk), lambda j: (0, j)),
        out_shape=jax.ShapeDtypeStruct((rows, ncol), F32),
        compiler_params=_cparams("arbitrary"),
        name="ada",
    )(c, w_ada, b_ada.reshape(1, ncol))


def _s5prep_kernel(are_ref, aim_ref, ldt_ref, btre_ref, btim_ref, ctre_ref, ctim_ref,
                   w_ref, p_ref, qt_ref, at_ref):
    T, half = S5_T, S5_BLK_STATE
    a_re = are_ref[0]
    a_im = aim_ref[0]
    dt = jnp.exp(ldt_ref[0])
    den = a_re * a_re + a_im * a_im

    def lam_pow(tau):
        mag = jnp.exp(a_re * dt * tau)
        return mag * jnp.cos(a_im * dt * tau), mag * jnp.sin(a_im * dt * tau)

    l_re, l_im = lam_pow(1.0)
    n_re, n_im = l_re - 1.0, l_im
    f_re = (n_re * a_re + n_im * a_im) / den
    f_im = (n_im * a_re - n_re * a_im) / den
    same_group = (lax.broadcasted_iota(jnp.int32, (LANES, half), 0) // S5_GROUP
                  == lax.broadcasted_iota(jnp.int32, (LANES, half), 1) // S5_STATE)
    bm_re = jnp.where(same_group, btre_ref[0], 0.0)
    bm_im = jnp.where(same_group, btim_ref[0], 0.0)
    cm_re = jnp.where(same_group, ctre_ref[0], 0.0)
    cm_im = jnp.where(same_group, ctim_ref[0], 0.0)
    nt = lambda a, b: lax.dot_general(a, b, (((1,), (1,)), ((), ())), preferred_element_type=F32)
    cm_re_s, cm_im_s = _split_bf16(cm_re), _split_bf16(cm_im)

    def dot_nt(a, b_split):
        a_hi, a_lo = _split_bf16(a)
        return nt(a_hi, b_split[0]) + (nt(a_hi, b_split[1]) + nt(a_lo, b_split[0]))
    blk = lambda i: slice(i * LANES, (i + 1) * LANES)
    for s in range(T):
        for t in range(s):
            w_ref[0, blk(s), blk(t)] = jnp.zeros((LANES, LANES), BF16)
    m_t = lax.broadcasted_iota(jnp.int32, (S5_POW_ROWS, half), 0).astype(F32) * float(T)
    mag = jnp.exp(a_re * dt * m_t)
    at_ref[0, :, :half] = mag * jnp.cos(a_im * dt * m_t)
    at_ref[0, :, half:] = mag * jnp.sin(a_im * dt * m_t)
    for tau in range(T + 1):
        p_re, p_im = (jnp.ones_like(a_re), jnp.zeros_like(a_re)) if tau == 0 else lam_pow(float(tau))
        if tau < T:
            g_re = p_re * f_re - p_im * f_im
            g_im = p_re * f_im + p_im * f_re
            inj_re = g_re * bm_re - g_im * bm_im
            inj_im = g_re * bm_im + g_im * bm_re
            s = T - 1 - tau
            p_ref[0, blk(s), :half] = inj_re.astype(BF16)
            p_ref[0, blk(s), half:] = inj_im.astype(BF16)
            k_tau = (dot_nt(inj_re, cm_re_s) - dot_nt(inj_im, cm_im_s)).astype(BF16)
            for s in range(T - tau):
                w_ref[0, blk(s), blk(s + tau)] = k_tau
        if tau >= 1:
            qt_ref[0, blk(tau - 1), :half] = (cm_re * p_re - cm_im * p_im).astype(BF16)
            qt_ref[0, blk(tau - 1), half:] = (-(cm_re * p_im + cm_im * p_re)).astype(BF16)


def _s5_prep(a_re, a_im, log_dt, b_re, b_im, c_re, c_im):
    NB, T, half = S5_NBLK, S5_T, S5_BLK_STATE
    tl, st = T * LANES, 2 * half
    lanes = lambda t: t.reshape(NB, 1, half)
    tiled = lambda t: jnp.tile(t.reshape(NB, LANES, S5_STATE), (1, 1, S5_BLK_GROUPS))
    row = pl.BlockSpec((1, 1, half), lambda k: (k, 0, 0))
    mat = pl.BlockSpec((1, LANES, half), lambda k: (k, 0, 0))
    return pl.pallas_call(
        _s5prep_kernel,
        grid=(NB,),
        in_specs=[row, row, row, mat, mat, mat, mat],
        out_specs=[pl.BlockSpec((1, tl, tl), lambda k: (k, 0, 0)),
                   pl.BlockSpec((1, tl, st), lambda k: (k, 0, 0)),
                   pl.BlockSpec((1, tl, st), lambda k: (k, 0, 0)),
                   pl.BlockSpec((1, S5_POW_ROWS, st), lambda k: (k, 0, 0))],
        out_shape=[jax.ShapeDtypeStruct((NB, tl, tl), BF16),
                   jax.ShapeDtypeStruct((NB, tl, st), BF16),
                   jax.ShapeDtypeStruct((NB, tl, st), BF16),
                   jax.ShapeDtypeStruct((NB, S5_POW_ROWS, st), F32)],
        compiler_params=_cparams("arbitrary"),
        name="s5prep",
    )(lanes(a_re), lanes(a_im), lanes(jnp.repeat(log_dt, S5_STATE)),
      tiled(jnp.swapaxes(b_re, 1, 2)), tiled(jnp.swapaxes(b_im, 1, 2)), tiled(c_re), tiled(c_im))


def _s5_kernel(u_ref, h0_ref, w_ref, p_ref, q_ref, at_ref, d_ref, y_ref, hT_ref,
               lhs_ref, x_ref, hp_ref, *, bt, n):
    T, half = S5_T, S5_BLK_STATE
    for b in range(bt):
        for t in range(T):
            lhs_ref[b * n:(b + 1) * n, t * LANES:(t + 1) * LANES] = (
                u_ref[b, pl.ds(t, n, stride=T), :].astype(BF16))
    lhs = lhs_ref[...]
    x_ref[...] = jnp.dot(lhs, p_ref[0], preferred_element_type=F32)
    cplx = lambda m: (at_ref[0, m:m + 1, :half], at_ref[0, m:m + 1, half:])
    rowi = lax.broadcasted_iota(jnp.int32, (SUBLANES, half), 0)
    shifts = []
    d = 1
    while d < SUBLANES:
        p_re, p_im = cplx(d)
        shifts.append((d, jnp.where(rowi >= d, p_re, 0.0), jnp.where(rowi >= d, p_im, 0.0)))
        d *= 2
    row_re = at_ref[0, 0:SUBLANES, :half]
    row_im = at_ref[0, 0:SUBLANES, half:]
    full_re, full_im = cplx(SUBLANES)
    y_in = jnp.concatenate(
        [jnp.dot(lhs[:, :c + MXU_TILE], w_ref[0, :c + MXU_TILE, c:c + MXU_TILE],
                 preferred_element_type=F32) for c in range(0, T * LANES, MXU_TILE)], axis=1)
    for b in range(bt):
        def tile(i, c):
            c_re, c_im = c
            rows = slice(b * n + i * SUBLANES, b * n + (i + 1) * SUBLANES)
            y_re = x_ref[rows, :half]
            y_im = x_ref[rows, half:]
            for d, m_re, m_im in shifts:
                s_re = pltpu.roll(y_re, d, axis=0)
                s_im = pltpu.roll(y_im, d, axis=0)
                y_re, y_im = y_re + (m_re * s_re - m_im * s_im), y_im + (m_re * s_im + m_im * s_re)
            e_re = jnp.where(rowi == 0, 0.0, pltpu.roll(y_re, 1, axis=0))
            e_im = jnp.where(rowi == 0, 0.0, pltpu.roll(y_im, 1, axis=0))
            hp_ref[rows, :half] = e_re + (row_re * c_re - row_im * c_im)
            hp_ref[rows, half:] = e_im + (row_re * c_im + row_im * c_re)
            last = SUBLANES - 1
            return (y_re[last:] + (full_re * c_re - full_im * c_im),
                    y_im[last:] + (full_re * c_im + full_im * c_re))
        c = (h0_ref[b, 0, :, :half], h0_ref[b, 0, :, half:])
        for i in range(n // SUBLANES):
            c = tile(i, c)
        hT_ref[b, 0, :, :half] = c[0]
        hT_ref[b, 0, :, half:] = c[1]
    y = (y_in
         + lax.dot_general(hp_ref[...].astype(BF16), q_ref[0], (((1,), (1,)), ((), ())),
                           preferred_element_type=F32))
    d = d_ref[...]
    for b in range(bt):
        for t in range(T):
            y_ref[b, pl.ds(t, n, stride=T), :] = (
                y[b * n:(b + 1) * n, t * LANES:(t + 1) * LANES]
                + d * u_ref[b, pl.ds(t, n, stride=T), :])


def _s5(u, h0, w, pmat, qmat, a_t, d_skip, bt):
    B, L, _ = u.shape
    n = L // S5_T
    tl, st = S5_T * LANES, 2 * S5_BLK_STATE
    assert n % SUBLANES == 0
    return pl.pallas_call(
        functools.partial(_s5_kernel, bt=bt, n=n),
        grid=(S5_NBLK, B // bt),
        in_specs=[pl.BlockSpec((bt, L, LANES), lambda k, b: (b, 0, k)),
                  pl.BlockSpec((bt, 1, 1, st), lambda k, b: (b, k, 0, 0)),
                  pl.BlockSpec((1, tl, tl), lambda k, b: (k, 0, 0)),
                  pl.BlockSpec((1, tl, st), lambda k, b: (k, 0, 0)),
                  pl.BlockSpec((1, tl, st), lambda k, b: (k, 0, 0)),
                  pl.BlockSpec((1, S5_POW_ROWS, st), lambda k, b: (k, 0, 0)),
                  pl.BlockSpec((1, LANES), lambda k, b: (0, k))],
        out_specs=[pl.BlockSpec((bt, L, LANES), lambda k, b: (b, 0, k)),
                   pl.BlockSpec((bt, 1, 1, st), lambda k, b: (b, k, 0, 0))],
        out_shape=[jax.ShapeDtypeStruct((B, L, S5_WIDTH), F32),
                   jax.ShapeDtypeStruct((B, S5_NBLK, 1, st), F32)],
        scratch_shapes=[pltpu.VMEM((bt * n, tl), BF16),
                        pltpu.VMEM((bt * n, st), F32),
                        pltpu.VMEM((bt * n, st), F32)],
        compiler_params=_cparams("arbitrary", "arbitrary"),
        name="s5",
    )(u, h0, w, pmat, qmat, a_t, d_skip.reshape(1, S5_WIDTH))


def _s5_state_in(h_re, h_im):
    B = h_re.shape[0]
    r = h_re.reshape(B, S5_NBLK, 1, S5_BLK_STATE)
    i = h_im.reshape(B, S5_NBLK, 1, S5_BLK_STATE)
    return jnp.concatenate([r, i], -1)


def _s5_state_out(h):
    B = h.shape[0]
    h = h.reshape(B, S5_NBLK, 2, S5_BLK_GROUPS, S5_STATE)
    return (h[:, :, 0].reshape(B, S5_GROUPS, S5_STATE), h[:, :, 1].reshape(B, S5_GROUPS, S5_STATE))


CONV_TAIL = SUBLANES


def _inproj_kernel(x_ref, sh_ref, sc_ref, w_ref, cb_ref, cw_ref,
                   u_ref, conv_ref, z_ref, ba_ref, cbo_ref, *xbuf_refs):
    bt, lt, d = x_ref.shape
    keep = CONV_WIDTH - 1
    o_q, o_z, o_b = S5_WIDTH, S5_WIDTH + CONV_CH, S5_WIDTH + CONV_CH + GDN_WIDTH
    nparts = CONV_CH // GDN_WIDTH
    parts = [slice(p * GDN_WIDTH, (p + 1) * GDN_WIDTH) for p in range(nparts)]
    nun = len(xbuf_refs) // nparts
    ub = bt // nun
    units = [slice(i * ub, (i + 1) * ub) for i in range(nun)]
    xbufs = [xbuf_refs[i * nparts:(i + 1) * nparts] for i in range(nun)]

    @pl.when(pl.program_id(1) == 0)
    def _():
        for us, bufs in zip(units, xbufs):
            for cs, xbuf_ref in zip(parts, bufs):
                xbuf_ref[:, 0:CONV_TAIL, :] = jnp.zeros((ub, CONV_TAIL, GDN_WIDTH), F32)
                xbuf_ref[:, CONV_TAIL - keep:CONV_TAIL, :] = cb_ref[us, :, cs]

    hs = []
    for us in units:
        h = _ln(x_ref[us]) * (1.0 + sc_ref[us, 0]) + sh_ref[us, 0]
        hs.append(h.reshape(ub * lt, d).astype(BF16))
    for p, cs in enumerate(parts):
        for h, bufs in zip(hs, xbufs):
            bufs[p][:, CONV_TAIL:CONV_TAIL + lt, :] = (
                _bdot(h, w_ref[:, o_q + cs.start:o_q + cs.stop]).reshape(ub, lt, GDN_WIDTH))
    for us, h in zip(units, hs):
        u_ref[us] = _bdot(h, w_ref[:, :o_q]).reshape(ub, lt, S5_WIDTH)
    for us, h in zip(units, hs):
        z_ref[us] = _bdot(h, w_ref[:, o_z:o_b]).reshape(ub, lt, GDN_WIDTH)
        ba_ref[us] = _bdot(h, w_ref[:, o_b:]).reshape(ub, lt, LANES)
    rows = lt + CONV_TAIL
    for us, bufs in zip(units, xbufs):
        for cs, xbuf_ref in zip(parts, bufs):
            xfull = xbuf_ref[...]
            conv = xfull[:, CONV_TAIL:, :] * cw_ref[keep:keep + 1, cs]
            for j in range(keep):
                shifted = pltpu.roll(xfull, rows - (CONV_TAIL - keep + j), axis=1)
                conv = conv + shifted[:, :lt, :] * cw_ref[j:j + 1, cs]
            conv_ref[us, :, cs] = conv
            cbo_ref[us, :, cs] = xbuf_ref[:, lt + CONV_TAIL - keep:lt + CONV_TAIL, :]
            xbuf_ref[:, 0:CONV_TAIL, :] = xbuf_ref[:, lt:lt + CONV_TAIL, :]


def _inproj(x, mod, w_in_p, conv_buf, conv_w, bt, lt):
    B, L, _ = x.shape
    nun = 2 if bt % 2 == 0 else 1
    xs = lambda w: pl.BlockSpec((bt, lt, w), lambda b, i: (b, i, 0))
    ms = lambda idx: pl.BlockSpec((bt, 1, 1, D_MODEL), lambda b, i: (b, idx, 0, 0))
    cache = pl.BlockSpec((bt, CONV_WIDTH - 1, CONV_CH), lambda b, i: (b, 0, 0))
    return pl.pallas_call(
        _inproj_kernel,
        grid=(B // bt, L // lt),
        in_specs=[xs(D_MODEL), ms(0), ms(1),
                  pl.BlockSpec((D_MODEL, IN_PAD), lambda b, i: (0, 0)), cache,
                  pl.BlockSpec((CONV_WIDTH, CONV_CH), lambda b, i: (0, 0))],
        out_specs=[xs(S5_WIDTH), xs(CONV_CH), xs(GDN_WIDTH), xs(LANES), cache],
        out_shape=[jax.ShapeDtypeStruct((B, L, w), F32) for w in (S5_WIDTH, CONV_CH, GDN_WIDTH, LANES)]
                  + [jax.ShapeDtypeStruct((B, CONV_WIDTH - 1, CONV_CH), F32)],
        scratch_shapes=[pltpu.VMEM((bt // nun, lt + CONV_TAIL, GDN_WIDTH), F32)
                        for _ in range(nun * (CONV_CH // GDN_WIDTH))],
        compiler_params=_cparams("arbitrary", "arbitrary"),
        name="inproj",
    )(x, mod, mod, w_in_p, conv_buf, conv_w)


def _mm(a, b):
    return jnp.dot(a.astype(BF16), b.astype(BF16), preferred_element_type=F32)


GDN_INV_BASE = 2
GDN_INV_JOIN = 4


def _bdot(a, b):
    return jnp.dot(a, b, preferred_element_type=F32)


def _unit_lower_inverse(a_list, ri, ci, size):
    eye_f = (ri == ci).astype(F32)
    blk = lambda s: (ri // s) == (ci // s)
    prev = min(GDN_INV_BASE, size)
    base_mask = blk(prev)
    d_list = [(eye_f - jnp.where(base_mask, a, 0.0)).astype(BF16) for a in a_list]
    while prev < size:
        cur = min(prev * GDN_INV_JOIN, size)
        off_mask = blk(cur) & jnp.logical_not(blk(prev))
        ms = [_bdot(d, jnp.where(off_mask, a, 0.0).astype(BF16)) for a, d in zip(a_list, d_list)]
        rs = [eye_f - m for m in ms]
        mps = [m.astype(BF16) for m in ms]
        span = 2
        while span < cur // prev:
            mps = [_bdot(mp, mp).astype(BF16) for mp in mps]
            rs = [r + _bdot(r.astype(BF16), mp) for r, mp in zip(rs, mps)]
            span *= 2
        d_list = [_bdot(r.astype(BF16), d).astype(BF16) for r, d in zip(rs, d_list)]
        prev = cur
    return d_list


def _gdn_kernel(conv_ref, z_ref, ba_ref, s0_ref, alog_ref, dtb_ref, ng_ref,
                y_ref, sT_ref, s_ref, *, chunk, bt):
    C, Dh, H = chunk, GDN_HEAD_DIM, GDN_HEADS

    @pl.when(pl.program_id(1) == 0)
    def _():
        s_ref[...] = s0_ref[...]

    act_all = _silu(conv_ref[...])
    ba = ba_ref[...]
    beta_all = _sigmoid(ba)
    xa = ba + dtb_ref[...]
    softplus = jnp.maximum(xa, 0.0) + jnp.log1p(jnp.exp(-jnp.abs(xa)))
    g = -jnp.exp(alog_ref[...]) * softplus

    ri = lax.broadcasted_iota(jnp.int32, (C, C), 0)
    ci = lax.broadcasted_iota(jnp.int32, (C, C), 1)
    causal = ri >= ci
    strict = ri > ci
    eye = ri == ci
    ng = ng_ref[...]
    tril = causal.astype(BF16)
    decay_all = []
    for b in range(bt):
        g_hi = g[b].astype(BF16)
        g_rest = g[b] - g_hi.astype(F32)
        g_mid = g_rest.astype(BF16)
        g_lo = (g_rest - g_mid.astype(F32)).astype(BF16)
        d3 = _bdot(tril, jnp.concatenate([g_hi, g_mid, g_lo], axis=1))
        decay_all.append(d3[:, :LANES] + (d3[:, LANES:2 * LANES] + d3[:, 2 * LANES:]))
    units = [(b, h) for b in range(bt) for h in range(H)]

    e_list, kbq_list, kT_list, rhs_list, qd_list, kdT_list, gl_list = [], [], [], [], [], [], []
    for b, h in units:
        decay, beta, act = decay_all[b], beta_all[b], act_all[b]
        dcol = decay[:, H + h:H + h + 1]
        dmat = jnp.broadcast_to(dcol, (C, C))
        drow = jnp.sum(jnp.where(eye, dmat, 0.0), axis=0, keepdims=True)
        e = jnp.exp(dmat - drow)
        bcol = beta[:, h:h + 1]
        q = act[:, h * Dh:(h + 1) * Dh]
        k = act[:, GDN_WIDTH + h * Dh:GDN_WIDTH + (h + 1) * Dh]
        v = act[:, 2 * GDN_WIDTH + h * Dh:2 * GDN_WIDTH + (h + 1) * Dh]
        q = q * (lax.rsqrt(jnp.sum(q * q, -1, keepdims=True) + RMS_EPS) * (Dh ** -0.5))
        k = k * lax.rsqrt(jnp.sum(k * k, -1, keepdims=True) + RMS_EPS)
        kb = k * bcol
        kT = k.T
        kT_b = kT.astype(BF16)
        edec = jnp.exp(dcol)
        dlast = decay[C - 1:C, H + h:H + h + 1]
        e_list.append(e)
        kbq_list.append(jnp.concatenate([kb, q], axis=0).astype(BF16))
        kT_list.append(kT_b)
        rhs_list.append(jnp.concatenate([v * bcol, kb * edec], axis=1).astype(BF16))
        qd_list.append((q * edec).astype(BF16))
        kdT_list.append((kT * jnp.exp(dlast - drow)).astype(BF16))
        gl_list.append(jnp.exp(dlast))
    sc_list = [_bdot(kbq, kT_b) for kbq, kT_b in zip(kbq_list, kT_list)]
    a_list = [jnp.where(strict, sc[:C] * e, 0.0) for sc, e in zip(sc_list, e_list)]
    qk_list = [jnp.where(causal, sc[C:] * e, 0.0).astype(BF16) for sc, e in zip(sc_list, e_list)]

    tinv_list = _unit_lower_inverse(a_list, ri, ci, C)

    uw_list = [_bdot(t, rhs) for t, rhs in zip(tinv_list, rhs_list)]
    s_list = [s_ref[b, h] for b, h in units]
    ws_list = [_bdot(jnp.concatenate([uw[:, Dh:].astype(BF16), qd], axis=0), s.astype(BF16))
               for uw, qd, s in zip(uw_list, qd_list, s_list)]
    vn_list = [(uw[:, :Dh] - ws[:C]).astype(BF16) for uw, ws in zip(uw_list, ws_list)]
    o_list = [ws[C:] + _bdot(qk, vn) for ws, qk, vn in zip(ws_list, qk_list, vn_list)]
    for i, (b, h) in enumerate(units):
        s_ref[b, h] = s_list[i] * gl_list[i] + _bdot(kdT_list[i], vn_list[i])
    for i, (b, h) in enumerate(units):
        o = o_list[i]
        o = o * lax.rsqrt(jnp.mean(o * o, -1, keepdims=True) + RMS_EPS) * ng
        y_ref[b, :, h * Dh:(h + 1) * Dh] = o * _silu(z_ref[b, :, h * Dh:(h + 1) * Dh])

    sT_ref[...] = s_ref[...]


def _gdn(conv, z, ba, s0, a_log, dt_bias, norm_g, chunk, bt):
    B, L, _ = conv.shape
    H, Dh = GDN_HEADS, GDN_HEAD_DIM
    lt = chunk
    place = lambda t: jnp.zeros((1, LANES), F32).at[0, H:2 * H].set(t)
    xs = lambda w: pl.BlockSpec((bt, lt, w), lambda b, i: (b, i, 0))
    full = lambda *shape: pl.BlockSpec(shape, lambda b, i: (0,) * len(shape))
    state = pl.BlockSpec((bt, H, Dh, Dh), lambda b, i: (b, 0, 0, 0))
    return pl.pallas_call(
        functools.partial(_gdn_kernel, chunk=chunk, bt=bt),
        grid=(B // bt, L // lt),
        in_specs=[xs(CONV_CH), xs(GDN_WIDTH), xs(LANES), state,
                  full(1, LANES), full(1, LANES), full(1, Dh)],
        out_specs=[xs(GDN_WIDTH), state],
        out_shape=[jax.ShapeDtypeStruct((B, L, GDN_WIDTH), F32),
                   jax.ShapeDtypeStruct((B, H, Dh, Dh), F32)],
        scratch_shapes=[pltpu.VMEM((bt, H, Dh, Dh), F32)],
        compiler_params=_cparams("arbitrary", "arbitrary"),
        name="gdn",
    )(conv, z, ba, s0, place(a_log), place(dt_bias), norm_g.reshape(1, Dh))


FF_BLK = 256


def _gelu_tanh(x):
    return 0.5 * x * (1.0 + jnp.tanh(0.7978845608028654 * (x + 0.044715 * x * x * x)))


def _outffn_kernel(x_ref, ya_ref, yb_ref, g1_ref, sh2_ref, sc2_ref, g2_ref,
                   wglu_ref, wout_ref, l1g_ref, l1b_ref, wup_ref, wdn_ref, l2g_ref, l2b_ref,
                   o_ref):
    bt, lt, d = x_ref.shape
    if bt % 2 == 0:
        subs = [(slice(i * bt // 2, (i + 1) * bt // 2), slice(0, lt)) for i in range(2)]
    else:
        subs = [(slice(0, bt), slice(i * lt // 2, (i + 1) * lt // 2)) for i in range(2)]
    rd = lambda ref, sub: ref[sub[0], sub[1], :]
    flat = lambda t: t.reshape(t.shape[0] * t.shape[1], t.shape[-1])
    unflat = lambda t, like: t.reshape(like.shape[0], like.shape[1], t.shape[-1])
    mod = lambda ref, sub: ref[sub[0], 0]
    nsub = range(len(subs))

    zz = [_mm(_gelu_tanh(flat(rd(ya_ref, s))), wglu_ref[...]) for s in subs]
    y_a = [z[:, :S5_WIDTH] * _sigmoid(z[:, S5_WIDTH:]) for z in zz]
    mix_b = [_mm(flat(rd(yb_ref, s)), wout_ref[S5_WIDTH:, :]) for s in subs]
    mix = [mb + _mm(ya, wout_ref[:S5_WIDTH, :]) for mb, ya in zip(mix_b, y_a)]
    xs = [rd(x_ref, s) for s in subs]
    x1 = [_ln(ALPHA * x + (1.0 + mod(g1_ref, s)) * unflat(mx, x)) * l1g_ref[...] + l1b_ref[...]
          for x, mx, s in zip(xs, mix, subs)]
    h = [flat(_ln(x) * (1.0 + mod(sc2_ref, s)) + mod(sh2_ref, s)).astype(BF16) for x, s in zip(x1, subs)]

    def gate_up(i, j):
        gate = _bdot(h[i], wup_ref[:, j * FF_BLK:(j + 1) * FF_BLK])
        up = _bdot(h[i], wup_ref[:, D_FF + j * FF_BLK:D_FF + (j + 1) * FF_BLK])
        return gate, up

    nblk = D_FF // FF_BLK
    acc = [None for _ in nsub]
    gu = [gate_up(i, 0) for i in nsub]
    for j in range(nblk):
        gu_next = [gate_up(i, j + 1) for i in nsub] if j + 1 < nblk else None
        for i in nsub:
            gate, up = gu[i]
            part = _mm(_silu(gate) * up, wdn_ref[j * FF_BLK:(j + 1) * FF_BLK, :])
            acc[i] = part if acc[i] is None else acc[i] + part
        gu = gu_next
    for i, s in enumerate(subs):
        o_ref[s[0], s[1], :] = (_ln(ALPHA * x1[i] + (1.0 + mod(g2_ref, s)) * unflat(acc[i], x1[i]))
                                * l2g_ref[...] + l2b_ref[...])


def _outffn(x, ya, yb, mod, w_glu, w_out, ln1_g, ln1_b, w_up, w_dn, ln2_g, ln2_b, bt, lt):
    B, L, _ = x.shape
    xs = lambda w: pl.BlockSpec((bt, lt, w), lambda b, i: (b, i, 0))
    ms = lambda idx: pl.BlockSpec((bt, 1, 1, D_MODEL), lambda b, i: (b, idx, 0, 0))
    const = lambda a: pl.BlockSpec(a.shape, lambda b, i: (0,) * a.ndim, pipeline_mode=pl.Buffered(1))
    vec = lambda t: t.reshape(1, D_MODEL)
    weights = (w_glu, w_out, vec(ln1_g), vec(ln1_b), w_up, w_dn, vec(ln2_g), vec(ln2_b))
    return pl.pallas_call(
        _outffn_kernel,
        grid=(B // bt, L // lt),
        in_specs=[xs(D_MODEL), xs(S5_WIDTH), xs(GDN_WIDTH), ms(2), ms(3), ms(4), ms(5)]
                 + [const(a) for a in weights],
        out_specs=xs(D_MODEL),
        out_shape=jax.ShapeDtypeStruct((B, L, D_MODEL), F32),
        compiler_params=_cparams("arbitrary", "arbitrary"),
        name="outffn",
    )(x, ya, yb, mod, mod, mod, mod, *weights)


TOKEN_TILE = 2 * MXU_TILE
GDN_CHUNK = LANES
GDN_CHAINS_SEQ = 4


def _plan(B, L):
    if L >= TOKEN_TILE:
        bt, lt = 1, TOKEN_TILE
    else:
        bt, lt = min(B, TOKEN_TILE // L), L
    assert B % bt == 0 and L % lt == 0
    gdn_chunk = min(L, GDN_CHUNK)
    gdn_bt = min(B, GDN_CHAINS_SEQ)
    assert L % gdn_chunk == 0 and B % gdn_bt == 0
    return bt, lt, gdn_chunk, gdn_bt


def _layer(x, mod, h_re, h_im, s_gdn, conv_buf, wts):
    (w_in_p, s5_mats, s5_d, w_glu, conv_w, a_log, dt_bias, norm_g, w_out,
     ln1_g, ln1_b, w_up, w_dn, ln2_g, ln2_b) = wts
    bt, lt, gdn_chunk, gdn_bt = _plan(x.shape[0], x.shape[1])
    u, conv, z, ba, cbo = _inproj(x, mod, w_in_p, conv_buf, conv_w, bt, lt)
    w, pmat, qmat, a_t = s5_mats
    ys5, hT = _s5(u, _s5_state_in(h_re, h_im), w, pmat, qmat, a_t, s5_d, bt)
    yb, sT = _gdn(conv, z, ba, s_gdn, a_log, dt_bias, norm_g, gdn_chunk, gdn_bt)
    y = _outffn(x, ys5, yb, mod, w_glu, w_out, ln1_g, ln1_b, w_up, w_dn, ln2_g, ln2_b, bt, lt)
    o_re, o_im = _s5_state_out(hT)
    return y, o_re, o_im, sT, cbo


def kernel(x_prompt, x_sample, state_s5_re, state_s5_im, state_gdn, cache_gdn_conv, c_prompt, c_sample, w_ada, b_ada, w_in, s5_a_re, s5_a_im, s5_log_dt, s5_b_re, s5_b_im, s5_c_re, s5_c_im, s5_d, w_s5_glu, gdn_conv_w, gdn_a_log, gdn_dt_bias, gdn_norm_g, w_out, ln1_g, ln1_b, w_ffn_up, w_ffn_down, ln2_g, ln2_b):
    bp, bs = x_prompt.shape[0], x_sample.shape[0]
    assert w_ada.shape[0] == DEPTH == 1
    l = 0
    c_all = jnp.concatenate([c_prompt, c_sample], 0)
    c_all = jnp.pad(c_all, ((0, -(bp + bs) % (2 * SUBLANES)), (0, 0)))
    mod = _ada(c_all, w_ada[l], b_ada[l])[:bp + bs]
    mod = mod.reshape(bp + bs, 6, 1, D_MODEL)
    w_in_p = jnp.pad(w_in[l], ((0, 0), (0, IN_PAD - w_in.shape[-1]))).astype(BF16)
    s5_mats = _s5_prep(s5_a_re[l], s5_a_im[l], s5_log_dt[l], s5_b_re[l], s5_b_im[l],
                       s5_c_re[l], s5_c_im[l])
    wts = (w_in_p, s5_mats, s5_d[l], w_s5_glu[l].astype(BF16), gdn_conv_w[l], gdn_a_log[l],
           gdn_dt_bias[l], gdn_norm_g[l], w_out[l].astype(BF16), ln1_g[l], ln1_b[l],
           w_ffn_up[l].astype(BF16), w_ffn_down[l].astype(BF16), ln2_g[l], ln2_b[l])
    zeros = lambda *s: jnp.zeros(s, F32)
    yp, p_re, p_im, p_gdn, p_conv = _layer(
        x_prompt, mod[:bp],
        zeros(bp, S5_GROUPS, S5_STATE), zeros(bp, S5_GROUPS, S5_STATE),
        zeros(bp, GDN_HEADS, GDN_HEAD_DIM, GDN_HEAD_DIM), zeros(bp, CONV_WIDTH - 1, CONV_CH),
        wts)
    ys, s_re, s_im, s_gdn, s_conv = _layer(
        x_sample, mod[bp:], state_s5_re[l], state_s5_im[l], state_gdn[l], cache_gdn_conv[l],
        wts)
    st = lambda t: t[None]
    return (yp, ys, st(p_re), st(p_im), st(p_gdn), st(p_conv),
            st(s_re), st(s_im), st(s_gdn), st(s_conv))
```

```python
import functools

import jax
import jax.numpy as jnp
from jax import lax
from jax.experimental import pallas as pl
from jax.experimental.pallas import tpu as pltpu

F32 = jnp.float32
BF16 = jnp.bfloat16

D_MODEL = 1024
S5_WIDTH = 512
S5_GROUP = 16
S5_GROUPS = 32
S5_STATE = 64
GDN_WIDTH = 512
GDN_HEAD_DIM = 128
GDN_HEADS = 4
CONV_WIDTH = 4
CONV_CH = 3 * GDN_WIDTH
D_FF = 2816
LN_EPS = 1e-5
RMS_EPS = 1e-6
DEPTH = 1
ALPHA = (2 * DEPTH) ** 0.25

LANES = 128
MXU_TILE = 256
S5_T = 8
S5_BLK_GROUPS = LANES // S5_GROUP
S5_NBLK = S5_GROUPS // S5_BLK_GROUPS
S5_BLK_STATE = S5_BLK_GROUPS * S5_STATE
SUBLANES = 8
S5_POW_ROWS = 2 * SUBLANES
IN_PAD = S5_WIDTH + 4 * GDN_WIDTH + LANES
VMEM_LIMIT = 56 * 1024 * 1024


def _cparams(*sem):
    return pltpu.CompilerParams(dimension_semantics=sem, vmem_limit_bytes=VMEM_LIMIT)


def _ln(x):
    mu = jnp.mean(x, -1, keepdims=True)
    xc = x - mu
    var = jnp.mean(xc * xc, -1, keepdims=True)
    return xc * lax.rsqrt(var + LN_EPS)


def _sigmoid(x):
    return 1.0 / (1.0 + jnp.exp(-x))


def _silu(x):
    return x * _sigmoid(x)


def _split_bf16(x):
    hi = x.astype(BF16)
    return hi, (x - hi.astype(F32)).astype(BF16)


def _ada_kernel(c_ref, w_ref, b_ref, o_ref):
    c_hi, c_lo = _split_bf16(_silu(c_ref[...]))
    w_hi, w_lo = _split_bf16(w_ref[...])
    dot = functools.partial(jnp.dot, preferred_element_type=F32)
    o_ref[...] = dot(c_hi, w_hi) + (dot(c_hi, w_lo) + dot(c_lo, w_hi)) + b_ref[...]


def _ada(c, w_ada, b_ada):
    rows = c.shape[0]
    ncol = w_ada.shape[1]
    blk = D_MODEL
    return pl.pallas_call(
        _ada_kernel,
        grid=(ncol // blk,),
        in_specs=[pl.BlockSpec((rows, D_MODEL), lambda j: (0, 0)),
                  pl.BlockSpec((D_MODEL, blk), lambda j: (0, j)),
                  pl.BlockSpec((1, blk), lambda j: (0, j))],
        out_specs=pl.BlockSpec((rows, blk), lambda j: (0, j)),
        out_shape=jax.ShapeDtypeStruct((rows, ncol), F32),
        compiler_params=_cparams("arbitrary"),
        name="ada",
    )(c, w_ada, b_ada.reshape(1, ncol))


def _s5prep_kernel(are_ref, aim_ref, ldt_ref, btre_ref, btim_ref, ctre_ref, ctim_ref,
                   w_ref, p_ref, qt_ref, at_ref):
    T, half = S5_T, S5_BLK_STATE
    a_re = are_ref[0]
    a_im = aim_ref[0]
    dt = jnp.exp(ldt_ref[0])
    den = a_re * a_re + a_im * a_im

    def lam_pow(tau):
        mag = jnp.exp(a_re * dt * tau)
        return mag * jnp.cos(a_im * dt * tau), mag * jnp.sin(a_im * dt * tau)

    l_re, l_im = lam_pow(1.0)
    n_re, n_im = l_re - 1.0, l_im
    f_re = (n_re * a_re + n_im * a_im) / den
    f_im = (n_im * a_re - n_re * a_im) / den
    same_group = (lax.broadcasted_iota(jnp.int32, (LANES, half), 0) // S5_GROUP
                  == lax.broadcasted_iota(jnp.int32, (LANES, half), 1) // S5_STATE)
    bm_re = jnp.where(same_group, btre_ref[0], 0.0)
    bm_im = jnp.where(same_group, btim_ref[0], 0.0)
    cm_re = jnp.where(same_group, ctre_ref[0], 0.0)
    cm_im = jnp.where(same_group, ctim_ref[0], 0.0)
    nt = lambda a, b: lax.dot_general(a, b, (((1,), (1,)), ((), ())), preferred_element_type=F32)
    cm_re_s, cm_im_s = _split_bf16(cm_re), _split_bf16(cm_im)

    def dot_nt(a, b_split):
        a_hi, a_lo = _split_bf16(a)
        return nt(a_hi, b_split[0]) + (nt(a_hi, b_split[1]) + nt(a_lo, b_split[0]))
    blk = lambda i: slice(i * LANES, (i + 1) * LANES)
    for s in range(T):
        for t in range(s):
            w_ref[0, blk(s), blk(t)] = jnp.zeros((LANES, LANES), BF16)
    m_t = lax.broadcasted_iota(jnp.int32, (S5_POW_ROWS, half), 0).astype(F32) * float(T)
    mag = jnp.exp(a_re * dt * m_t)
    at_ref[0, :, :half] = mag * jnp.cos(a_im * dt * m_t)
    at_ref[0, :, half:] = mag * jnp.sin(a_im * dt * m_t)
    for tau in range(T + 1):
        p_re, p_im = (jnp.ones_like(a_re), jnp.zeros_like(a_re)) if tau == 0 else lam_pow(float(tau))
        if tau < T:
            g_re = p_re * f_re - p_im * f_im
            g_im = p_re * f_im + p_im * f_re
            inj_re = g_re * bm_re - g_im * bm_im
            inj_im = g_re * bm_im + g_im * bm_re
            s = T - 1 - tau
            p_ref[0, blk(s), :half] = inj_re.astype(BF16)
            p_ref[0, blk(s), half:] = inj_im.astype(BF16)
            k_tau = (dot_nt(inj_re, cm_re_s) - dot_nt(inj_im, cm_im_s)).astype(BF16)
            for s in range(T - tau):
                w_ref[0, blk(s), blk(s + tau)] = k_tau
        if tau >= 1:
            qt_ref[0, blk(tau - 1), :half] = (cm_re * p_re - cm_im * p_im).astype(BF16)
            qt_ref[0, blk(tau - 1), half:] = (-(cm_re * p_im + cm_im * p_re)).astype(BF16)


def _s5_prep(a_re, a_im, log_dt, b_re, b_im, c_re, c_im):
    NB, T, half = S5_NBLK, S5_T, S5_BLK_STATE
    tl, st = T * LANES, 2 * half
    lanes = lambda t: t.reshape(NB, 1, half)
    tiled = lambda t: jnp.tile(t.reshape(NB, LANES, S5_STATE), (1, 1, S5_BLK_GROUPS))
    row = pl.BlockSpec((1, 1, half), lambda k: (k, 0, 0))
    mat = pl.BlockSpec((1, LANES, half), lambda k: (k, 0, 0))
    return pl.pallas_call(
        _s5prep_kernel,
        grid=(NB,),
        in_specs=[row, row, row, mat, mat, mat, mat],
        out_specs=[pl.BlockSpec((1, tl, tl), lambda k: (k, 0, 0)),
                   pl.BlockSpec((1, tl, st), lambda k: (k, 0, 0)),
                   pl.BlockSpec((1, tl, st), lambda k: (k, 0, 0)),
                   pl.BlockSpec((1, S5_POW_ROWS, st), lambda k: (k, 0, 0))],
        out_shape=[jax.ShapeDtypeStruct((NB, tl, tl), BF16),
                   jax.ShapeDtypeStruct((NB, tl, st), BF16),
                   jax.ShapeDtypeStruct((NB, tl, st), BF16),
                   jax.ShapeDtypeStruct((NB, S5_POW_ROWS, st), F32)],
        compiler_params=_cparams("arbitrary"),
        name="s5prep",
    )(lanes(a_re), lanes(a_im), lanes(jnp.repeat(log_dt, S5_STATE)),
      tiled(jnp.swapaxes(b_re, 1, 2)), tiled(jnp.swapaxes(b_im, 1, 2)), tiled(c_re), tiled(c_im))


def _s5_kernel(u_ref, h0_ref, w_ref, p_ref, q_ref, at_ref, d_ref, y_ref, hT_ref,
               lhs_ref, x_ref, hp_ref, *, bt, n):
    T, half = S5_T, S5_BLK_STATE
    for b in range(bt):
        for t in range(T):
            lhs_ref[b * n:(b + 1) * n, t * LANES:(t + 1) * LANES] = (
                u_ref[b, pl.ds(t, n, stride=T), :].astype(BF16))
    lhs = lhs_ref[...]
    x_ref[...] = jnp.dot(lhs, p_ref[0], preferred_element_type=F32)
    cplx = lambda m: (at_ref[0, m:m + 1, :half], at_ref[0, m:m + 1, half:])
    rowi = lax.broadcasted_iota(jnp.int32, (SUBLANES, half), 0)
    shifts = []
    d = 1
    while d < SUBLANES:
        p_re, p_im = cplx(d)
        shifts.append((d, jnp.where(rowi >= d, p_re, 0.0), jnp.where(rowi >= d, p_im, 0.0)))
        d *= 2
    row_re = at_ref[0, 0:SUBLANES, :half]
    row_im = at_ref[0, 0:SUBLANES, half:]
    full_re, full_im = cplx(SUBLANES)
    y_in = jnp.concatenate(
        [jnp.dot(lhs[:, :c + MXU_TILE], w_ref[0, :c + MXU_TILE, c:c + MXU_TILE],
                 preferred_element_type=F32) for c in range(0, T * LANES, MXU_TILE)], axis=1)
    for b in range(bt):
        def tile(i, c):
            c_re, c_im = c
            rows = slice(b * n + i * SUBLANES, b * n + (i + 1) * SUBLANES)
            y_re = x_ref[rows, :half]
            y_im = x_ref[rows, half:]
            for d, m_re, m_im in shifts:
                s_re = pltpu.roll(y_re, d, axis=0)
                s_im = pltpu.roll(y_im, d, axis=0)
                y_re, y_im = y_re + (m_re * s_re - m_im * s_im), y_im + (m_re * s_im + m_im * s_re)
            e_re = jnp.where(rowi == 0, 0.0, pltpu.roll(y_re, 1, axis=0))
            e_im = jnp.where(rowi == 0, 0.0, pltpu.roll(y_im, 1, axis=0))
            hp_ref[rows, :half] = e_re + (row_re * c_re - row_im * c_im)
            hp_ref[rows, half:] = e_im + (row_re * c_im + row_im * c_re)
            last = SUBLANES - 1
            return (y_re[last:] + (full_re * c_re - full_im * c_im),
                    y_im[last:] + (full_re * c_im + full_im * c_re))
        c = (h0_ref[b, 0, :, :half], h0_ref[b, 0, :, half:])
        for i in range(n // SUBLANES):
            c = tile(i, c)
        hT_ref[b, 0, :, :half] = c[0]
        hT_ref[b, 0, :, half:] = c[1]
    y = (y_in
         + lax.dot_general(hp_ref[...].astype(BF16), q_ref[0], (((1,), (1,)), ((), ())),
                           preferred_element_type=F32))
    d = d_ref[...]
    for b in range(bt):
        for t in range(T):
            y_ref[b, pl.ds(t, n, stride=T), :] = (
                y[b * n:(b + 1) * n, t * LANES:(t + 1) * LANES]
                + d * u_ref[b, pl.ds(t, n, stride=T), :])


def _s5(u, h0, w, pmat, qmat, a_t, d_skip, bt):
    B, L, _ = u.shape
    n = L // S5_T
    tl, st = S5_T * LANES, 2 * S5_BLK_STATE
    assert n % SUBLANES == 0
    return pl.pallas_call(
        functools.partial(_s5_kernel, bt=bt, n=n),
        grid=(S5_NBLK, B // bt),
        in_specs=[pl.BlockSpec((bt, L, LANES), lambda k, b: (b, 0, k)),
                  pl.BlockSpec((bt, 1, 1, st), lambda k, b: (b, k, 0, 0)),
                  pl.BlockSpec((1, tl, tl), lambda k, b: (k, 0, 0)),
                  pl.BlockSpec((1, tl, st), lambda k, b: (k, 0, 0)),
                  pl.BlockSpec((1, tl, st), lambda k, b: (k, 0, 0)),
                  pl.BlockSpec((1, S5_POW_ROWS, st), lambda k, b: (k, 0, 0)),
                  pl.BlockSpec((1, LANES), lambda k, b: (0, k))],
        out_specs=[pl.BlockSpec((bt, L, LANES), lambda k, b: (b, 0, k)),
                   pl.BlockSpec((bt, 1, 1, st), lambda k, b: (b, k, 0, 0))],
        out_shape=[jax.ShapeDtypeStruct((B, L, S5_WIDTH), F32),
                   jax.ShapeDtypeStruct((B, S5_NBLK, 1, st), F32)],
        scratch_shapes=[pltpu.VMEM((bt * n, tl), BF16),
                        pltpu.VMEM((bt * n, st), F32),
                        pltpu.VMEM((bt * n, st), F32)],
        compiler_params=_cparams("arbitrary", "arbitrary"),
        name="s5",
    )(u, h0, w, pmat, qmat, a_t, d_skip.reshape(1, S5_WIDTH))


def _s5_state_in(h_re, h_im):
    B = h_re.shape[0]
    r = h_re.reshape(B, S5_NBLK, 1, S5_BLK_STATE)
    i = h_im.reshape(B, S5_NBLK, 1, S5_BLK_STATE)
    return jnp.concatenate([r, i], -1)


def _s5_state_out(h):
    B = h.shape[0]
    h = h.reshape(B, S5_NBLK, 2, S5_BLK_GROUPS, S5_STATE)
    return (h[:, :, 0].reshape(B, S5_GROUPS, S5_STATE), h[:, :, 1].reshape(B, S5_GROUPS, S5_STATE))


CONV_TAIL = SUBLANES


def _inproj_kernel(x_ref, sh_ref, sc_ref, w_ref, cb_ref, cw_ref,
                   u_ref, conv_ref, z_ref, ba_ref, cbo_ref, *xbuf_refs):
    bt, lt, d = x_ref.shape
    keep = CONV_WIDTH - 1
    o_q, o_z, o_b = S5_WIDTH, S5_WIDTH + CONV_CH, S5_WIDTH + CONV_CH + GDN_WIDTH
    nparts = CONV_CH // GDN_WIDTH
    parts = [slice(p * GDN_WIDTH, (p + 1) * GDN_WIDTH) for p in range(nparts)]
    nun = len(xbuf_refs) // nparts
    ub = bt // nun
    units = [slice(i * ub, (i + 1) * ub) for i in range(nun)]
    xbufs = [xbuf_refs[i * nparts:(i + 1) * nparts] for i in range(nun)]

    @pl.when(pl.program_id(1) == 0)
    def _():
        for us, bufs in zip(units, xbufs):
            for cs, xbuf_ref in zip(parts, bufs):
                xbuf_ref[:, 0:CONV_TAIL, :] = jnp.zeros((ub, CONV_TAIL, GDN_WIDTH), F32)
                xbuf_ref[:, CONV_TAIL - keep:CONV_TAIL, :] = cb_ref[us, :, cs]

    hs = []
    for us in units:
        h = _ln(x_ref[us]) * (1.0 + sc_ref[us, 0]) + sh_ref[us, 0]
        hs.append(h.reshape(ub * lt, d).astype(BF16))
    for p, cs in enumerate(parts):
        for h, bufs in zip(hs, xbufs):
            bufs[p][:, CONV_TAIL:CONV_TAIL + lt, :] = (
                _bdot(h, w_ref[:, o_q + cs.start:o_q + cs.stop]).reshape(ub, lt, GDN_WIDTH))
    for us, h in zip(units, hs):
        u_ref[us] = _bdot(h, w_ref[:, :o_q]).reshape(ub, lt, S5_WIDTH)
    for us, h in zip(units, hs):
        z_ref[us] = _bdot(h, w_ref[:, o_z:o_b]).reshape(ub, lt, GDN_WIDTH)
        ba_ref[us] = _bdot(h, w_ref[:, o_b:]).reshape(ub, lt, LANES)
    rows = lt + CONV_TAIL
    for us, bufs in zip(units, xbufs):
        for cs, xbuf_ref in zip(parts, bufs):
            xfull = xbuf_ref[...]
            conv = xfull[:, CONV_TAIL:, :] * cw_ref[keep:keep + 1, cs]
            for j in range(keep):
                shifted = pltpu.roll(xfull, rows - (CONV_TAIL - keep + j), axis=1)
                conv = conv + shifted[:, :lt, :] * cw_ref[j:j + 1, cs]
            conv_ref[us, :, cs] = conv
            cbo_ref[us, :, cs] = xbuf_ref[:, lt + CONV_TAIL - keep:lt + CONV_TAIL, :]
            xbuf_ref[:, 0:CONV_TAIL, :] = xbuf_ref[:, lt:lt + CONV_TAIL, :]


def _inproj(x, mod, w_in_p, conv_buf, conv_w, bt, lt):
    B, L, _ = x.shape
    nun = 2 if bt % 2 == 0 else 1
    xs = lambda w: pl.BlockSpec((bt, lt, w), lambda b, i: (b, i, 0))
    ms = lambda idx: pl.BlockSpec((bt, 1, 1, D_MODEL), lambda b, i: (b, idx, 0, 0))
    cache = pl.BlockSpec((bt, CONV_WIDTH - 1, CONV_CH), lambda b, i: (b, 0, 0))
    return pl.pallas_call(
        _inproj_kernel,
        grid=(B // bt, L // lt),
        in_specs=[xs(D_MODEL), ms(0), ms(1),
                  pl.BlockSpec((D_MODEL, IN_PAD), lambda b, i: (0, 0)), cache,
                  pl.BlockSpec((CONV_WIDTH, CONV_CH), lambda b, i: (0, 0))],
        out_specs=[xs(S5_WIDTH), xs(CONV_CH), xs(GDN_WIDTH), xs(LANES), cache],
        out_shape=[jax.ShapeDtypeStruct((B, L, w), F32) for w in (S5_WIDTH, CONV_CH, GDN_WIDTH, LANES)]
                  + [jax.ShapeDtypeStruct((B, CONV_WIDTH - 1, CONV_CH), F32)],
        scratch_shapes=[pltpu.VMEM((bt // nun, lt + CONV_TAIL, GDN_WIDTH), F32)
                        for _ in range(nun * (CONV_CH // GDN_WIDTH))],
        compiler_params=_cparams("arbitrary", "arbitrary"),
        name="inproj",
    )(x, mod, mod, w_in_p, conv_buf, conv_w)


def _mm(a, b):
    return jnp.dot(a.astype(BF16), b.astype(BF16), preferred_element_type=F32)


GDN_INV_BASE = 2
GDN_INV_JOIN = 4


def _bdot(a, b):
    return jnp.dot(a, b, preferred_element_type=F32)


def _unit_lower_inverse(a_list, ri, ci, size):
    eye_f = (ri == ci).astype(F32)
    blk = lambda s: (ri // s) == (ci // s)
    prev = min(GDN_INV_BASE, size)
    base_mask = blk(prev)
    d_list = [(eye_f - jnp.where(base_mask, a, 0.0)).astype(BF16) for a in a_list]
    while prev < size:
        cur = min(prev * GDN_INV_JOIN, size)
        off_mask = blk(cur) & jnp.logical_not(blk(prev))
        ms = [_bdot(d, jnp.where(off_mask, a, 0.0).astype(BF16)) for a, d in zip(a_list, d_list)]
        rs = [eye_f - m for m in ms]
        mps = [m.astype(BF16) for m in ms]
        span = 2
        while span < cur // prev:
            mps = [_bdot(mp, mp).astype(BF16) for mp in mps]
            rs = [r + _bdot(r.astype(BF16), mp) for r, mp in zip(rs, mps)]
            span *= 2
        d_list = [_bdot(r.astype(BF16), d).astype(BF16) for r, d in zip(rs, d_list)]
        prev = cur
    return d_list


def _gdn_kernel(conv_ref, z_ref, ba_ref, s0_ref, alog_ref, dtb_ref, ng_ref,
                y_ref, sT_ref, s_ref, *, chunk, bt):
    C, Dh, H = chunk, GDN_HEAD_DIM, GDN_HEADS

    @pl.when(pl.program_id(1) == 0)
    def _():
        s_ref[...] = s0_ref[...]

    act_all = _silu(conv_ref[...])
    ba = ba_ref[...]
    beta_all = _sigmoid(ba)
    xa = ba + dtb_ref[...]
    softplus = jnp.maximum(xa, 0.0) + jnp.log1p(jnp.exp(-jnp.abs(xa)))
    g = -jnp.exp(alog_ref[...]) * softplus

    ri = lax.broadcasted_iota(jnp.int32, (C, C), 0)
    ci = lax.broadcasted_iota(jnp.int32, (C, C), 1)
    causal = ri >= ci
    strict = ri > ci
    eye = ri == ci
    ng = ng_ref[...]
    tril = causal.astype(BF16)
    decay_all = []
    for b in range(bt):
        g_hi = g[b].astype(BF16)
        g_rest = g[b] - g_hi.astype(F32)
        g_mid = g_rest.astype(BF16)
        g_lo = (g_rest - g_mid.astype(F32)).astype(BF16)
        d3 = _bdot(tril, jnp.concatenate([g_hi, g_mid, g_lo], axis=1))
        decay_all.append(d3[:, :LANES] + (d3[:, LANES:2 * LANES] + d3[:, 2 * LANES:]))
    units = [(b, h) for b in range(bt) for h in range(H)]

    e_list, kbq_list, kT_list, rhs_list, qd_list, kdT_list, gl_list = [], [], [], [], [], [], []
    for b, h in units:
        decay, beta, act = decay_all[b], beta_all[b], act_all[b]
        dcol = decay[:, H + h:H + h + 1]
        dmat = jnp.broadcast_to(dcol, (C, C))
        drow = jnp.sum(jnp.where(eye, dmat, 0.0), axis=0, keepdims=True)
        e = jnp.exp(dmat - drow)
        bcol = beta[:, h:h + 1]
        q = act[:, h * Dh:(h + 1) * Dh]
        k = act[:, GDN_WIDTH + h * Dh:GDN_WIDTH + (h + 1) * Dh]
        v = act[:, 2 * GDN_WIDTH + h * Dh:2 * GDN_WIDTH + (h + 1) * Dh]
        q = q * (lax.rsqrt(jnp.sum(q * q, -1, keepdims=True) + RMS_EPS) * (Dh ** -0.5))
        k = k * lax.rsqrt(jnp.sum(k * k, -1, keepdims=True) + RMS_EPS)
        kb = k * bcol
        kT = k.T
        kT_b = kT.astype(BF16)
        edec = jnp.exp(dcol)
        dlast = decay[C - 1:C, H + h:H + h + 1]
        e_list.append(e)
        kbq_list.append(jnp.concatenate([kb, q], axis=0).astype(BF16))
        kT_list.append(kT_b)
        rhs_list.append(jnp.concatenate([v * bcol, kb * edec], axis=1).astype(BF16))
        qd_list.append((q * edec).astype(BF16))
        kdT_list.append((kT * jnp.exp(dlast - drow)).astype(BF16))
        gl_list.append(jnp.exp(dlast))
    sc_list = [_bdot(kbq, kT_b) for kbq, kT_b in zip(kbq_list, kT_list)]
    a_list = [jnp.where(strict, sc[:C] * e, 0.0) for sc, e in zip(sc_list, e_list)]
    qk_list = [jnp.where(causal, sc[C:] * e, 0.0).astype(BF16) for sc, e in zip(sc_list, e_list)]

    tinv_list = _unit_lower_inverse(a_list, ri, ci, C)

    uw_list = [_bdot(t, rhs) for t, rhs in zip(tinv_list, rhs_list)]
    s_list = [s_ref[b, h] for b, h in units]
    ws_list = [_bdot(jnp.concatenate([uw[:, Dh:].astype(BF16), qd], axis=0), s.astype(BF16))
               for uw, qd, s in zip(uw_list, qd_list, s_list)]
    vn_list = [(uw[:, :Dh] - ws[:C]).astype(BF16) for uw, ws in zip(uw_list, ws_list)]
    ov_list = [_bdot(jnp.concatenate([qk, kdT], axis=0), vn)
               for qk, kdT, vn in zip(qk_list, kdT_list, vn_list)]
    for i, (b, h) in enumerate(units):
        s_ref[b, h] = s_list[i] * gl_list[i] + ov_list[i][C:]
    for i, (b, h) in enumerate(units):
        o = ws_list[i][C:] + ov_list[i][:C]
        o = o * lax.rsqrt(jnp.mean(o * o, -1, keepdims=True) + RMS_EPS) * ng
        y_ref[b, :, h * Dh:(h + 1) * Dh] = o * _silu(z_ref[b, :, h * Dh:(h + 1) * Dh])

    sT_ref[...] = s_ref[...]


def _gdn(conv, z, ba, s0, a_log, dt_bias, norm_g, chunk, bt):
    B, L, _ = conv.shape
    H, Dh = GDN_HEADS, GDN_HEAD_DIM
    lt = chunk
    place = lambda t: jnp.zeros((1, LANES), F32).at[0, H:2 * H].set(t)
    xs = lambda w: pl.BlockSpec((bt, lt, w), lambda b, i: (b, i, 0))
    full = lambda *shape: pl.BlockSpec(shape, lambda b, i: (0,) * len(shape))
    state = pl.BlockSpec((bt, H, Dh, Dh), lambda b, i: (b, 0, 0, 0))
    return pl.pallas_call(
        functools.partial(_gdn_kernel, chunk=chunk, bt=bt),
        grid=(B // bt, L // lt),
        in_specs=[xs(CONV_CH), xs(GDN_WIDTH), xs(LANES), state,
                  full(1, LANES), full(1, LANES), full(1, Dh)],
        out_specs=[xs(GDN_WIDTH), state],
        out_shape=[jax.ShapeDtypeStruct((B, L, GDN_WIDTH), F32),
                   jax.ShapeDtypeStruct((B, H, Dh, Dh), F32)],
        scratch_shapes=[pltpu.VMEM((bt, H, Dh, Dh), F32)],
        compiler_params=_cparams("arbitrary", "arbitrary"),
        name="gdn",
    )(conv, z, ba, s0, place(a_log), place(dt_bias), norm_g.reshape(1, Dh))


FF_BLK = 256


def _gelu_tanh(x):
    return 0.5 * x * (1.0 + jnp.tanh(0.7978845608028654 * (x + 0.044715 * x * x * x)))


def _outffn_kernel(x_ref, ya_ref, yb_ref, g1_ref, sh2_ref, sc2_ref, g2_ref,
                   wglu_ref, wout_ref, l1g_ref, l1b_ref, wup_ref, wdn_ref, l2g_ref, l2b_ref,
                   o_ref):
    bt, lt, d = x_ref.shape
    if bt % 2 == 0:
        subs = [(slice(i * bt // 2, (i + 1) * bt // 2), slice(0, lt)) for i in range(2)]
    else:
        subs = [(slice(0, bt), slice(i * lt // 2, (i + 1) * lt // 2)) for i in range(2)]
    rd = lambda ref, sub: ref[sub[0], sub[1], :]
    flat = lambda t: t.reshape(t.shape[0] * t.shape[1], t.shape[-1])
    unflat = lambda t, like: t.reshape(like.shape[0], like.shape[1], t.shape[-1])
    mod = lambda ref, sub: ref[sub[0], 0]
    nsub = range(len(subs))

    zz = [_mm(_gelu_tanh(flat(rd(ya_ref, s))), wglu_ref[...]) for s in subs]
    y_a = [z[:, :S5_WIDTH] * _sigmoid(z[:, S5_WIDTH:]) for z in zz]
    mix_b = [_mm(flat(rd(yb_ref, s)), wout_ref[S5_WIDTH:, :]) for s in subs]
    mix = [mb + _mm(ya, wout_ref[:S5_WIDTH, :]) for mb, ya in zip(mix_b, y_a)]
    xs = [rd(x_ref, s) for s in subs]
    x1 = [_ln(ALPHA * x + (1.0 + mod(g1_ref, s)) * unflat(mx, x)) * l1g_ref[...] + l1b_ref[...]
          for x, mx, s in zip(xs, mix, subs)]
    h = [flat(_ln(x) * (1.0 + mod(sc2_ref, s)) + mod(sh2_ref, s)).astype(BF16) for x, s in zip(x1, subs)]

    def gate_up(i, j):
        gate = _bdot(h[i], wup_ref[:, j * FF_BLK:(j + 1) * FF_BLK])
        up = _bdot(h[i], wup_ref[:, D_FF + j * FF_BLK:D_FF + (j + 1) * FF_BLK])
        return gate, up

    nblk = D_FF // FF_BLK
    acc = [None for _ in nsub]
    gu = [gate_up(i, 0) for i in nsub]
    for j in range(nblk):
        gu_next = [gate_up(i, j + 1) for i in nsub] if j + 1 < nblk else None
        for i in nsub:
            gate, up = gu[i]
            part = _mm(_silu(gate) * up, wdn_ref[j * FF_BLK:(j + 1) * FF_BLK, :])
            acc[i] = part if acc[i] is None else acc[i] + part
        gu = gu_next
    for i, s in enumerate(subs):
        o_ref[s[0], s[1], :] = (_ln(ALPHA * x1[i] + (1.0 + mod(g2_ref, s)) * unflat(acc[i], x1[i]))
                                * l2g_ref[...] + l2b_ref[...])


def _outffn(x, ya, yb, mod, w_glu, w_out, ln1_g, ln1_b, w_up, w_dn, ln2_g, ln2_b, bt, lt):
    B, L, _ = x.shape
    xs = lambda w: pl.BlockSpec((bt, lt, w), lambda b, i: (b, i, 0))
    ms = lambda idx: pl.BlockSpec((bt, 1, 1, D_MODEL), lambda b, i: (b, idx, 0, 0))
    const = lambda a: pl.BlockSpec(a.shape, lambda b, i: (0,) * a.ndim, pipeline_mode=pl.Buffered(1))
    vec = lambda t: t.reshape(1, D_MODEL)
    weights = (w_glu, w_out, vec(ln1_g), vec(ln1_b), w_up, w_dn, vec(ln2_g), vec(ln2_b))
    return pl.pallas_call(
        _outffn_kernel,
        grid=(B // bt, L // lt),
        in_specs=[xs(D_MODEL), xs(S5_WIDTH), xs(GDN_WIDTH), ms(2), ms(3), ms(4), ms(5)]
                 + [const(a) for a in weights],
        out_specs=xs(D_MODEL),
        out_shape=jax.ShapeDtypeStruct((B, L, D_MODEL), F32),
        compiler_params=_cparams("arbitrary", "arbitrary"),
        name="outffn",
    )(x, ya, yb, mod, mod, mod, mod, *weights)


TOKEN_TILE = 2 * MXU_TILE
GDN_CHUNK = LANES
GDN_CHAINS_SEQ = 4


def _plan(B, L):
    if L >= TOKEN_TILE:
        bt, lt = 1, TOKEN_TILE
    else:
        bt, lt = min(B, TOKEN_TILE // L), L
    assert B % bt == 0 and L % lt == 0
    gdn_chunk = min(L, GDN_CHUNK)
    gdn_bt = min(B, GDN_CHAINS_SEQ)
    assert L % gdn_chunk == 0 and B % gdn_bt == 0
    return bt, lt, gdn_chunk, gdn_bt


def _layer(x, mod, h_re, h_im, s_gdn, conv_buf, wts):
    (w_in_p, s5_mats, s5_d, w_glu, conv_w, a_log, dt_bias, norm_g, w_out,
     ln1_g, ln1_b, w_up, w_dn, ln2_g, ln2_b) = wts
    bt, lt, gdn_chunk, gdn_bt = _plan(x.shape[0], x.shape[1])
    u, conv, z, ba, cbo = _inproj(x, mod, w_in_p, conv_buf, conv_w, bt, lt)
    w, pmat, qmat, a_t = s5_mats
    ys5, hT = _s5(u, _s5_state_in(h_re, h_im), w, pmat, qmat, a_t, s5_d, bt)
    yb, sT = _gdn(conv, z, ba, s_gdn, a_log, dt_bias, norm_g, gdn_chunk, gdn_bt)
    y = _outffn(x, ys5, yb, mod, w_glu, w_out, ln1_g, ln1_b, w_up, w_dn, ln2_g, ln2_b, bt, lt)
    o_re, o_im = _s5_state_out(hT)
    return y, o_re, o_im, sT, cbo


def kernel(x_prompt, x_sample, state_s5_re, state_s5_im, state_gdn, cache_gdn_conv, c_prompt, c_sample, w_ada, b_ada, w_in, s5_a_re, s5_a_im, s5_log_dt, s5_b_re, s5_b_im, s5_c_re, s5_c_im, s5_d, w_s5_glu, gdn_conv_w, gdn_a_log, gdn_dt_bias, gdn_norm_g, w_out, ln1_g, ln1_b, w_ffn_up, w_ffn_down, ln2_g, ln2_b):
    bp, bs = x_prompt.shape[0], x_sample.shape[0]
    assert w_ada.shape[0] == DEPTH == 1
    l = 0
    c_all = jnp.concatenate([c_prompt, c_sample], 0)
    c_all = jnp.pad(c_all, ((0, -(bp + bs) % (2 * SUBLANES)), (0, 0)))
    mod = _ada(c_all, w_ada[l], b_ada[l])[:bp + bs]
    mod = mod.reshape(bp + bs, 6, 1, D_MODEL)
    w_in_p = jnp.pad(w_in[l], ((0, 0), (0, IN_PAD - w_in.shape[-1]))).astype(BF16)
    s5_mats = _s5_prep(s5_a_re[l], s5_a_im[l], s5_log_dt[l], s5_b_re[l], s5_b_im[l],
                       s5_c_re[l], s5_c_im[l])
    wts = (w_in_p, s5_mats, s5_d[l], w_s5_glu[l].astype(BF16), gdn_conv_w[l], gdn_a_log[l],
           gdn_dt_bias[l], gdn_norm_g[l], w_out[l].astype(BF16), ln1_g[l], ln1_b[l],
           w_ffn_up[l].astype(BF16), w_ffn_down[l].astype(BF16), ln2_g[l], ln2_b[l])
    zeros = lambda *s: jnp.zeros(s, F32)
    yp, p_re, p_im, p_gdn, p_conv = _layer(
        x_prompt, mod[:bp],
        zeros(bp, S5_GROUPS, S5_STATE), zeros(bp, S5_GROUPS, S5_STATE),
        zeros(bp, GDN_HEADS, GDN_HEAD_DIM, GDN_HEAD_DIM), zeros(bp, CONV_WIDTH - 1, CONV_CH),
        wts)
    ys, s_re, s_im, s_gdn, s_conv = _layer(
        x_sample, mod[bp:], state_s5_re[l], state_s5_im[l], state_gdn[l], cache_gdn_conv[l],
        wts)
    st = lambda t: t[None]
    return (yp, ys, st(p_re), st(p_im), st(p_gdn), st(p_conv),
            st(s_re), st(s_im), st(s_gdn), st(s_conv))
```

```python
import functools

import jax
import jax.numpy as jnp
from jax import lax
from jax.experimental import pallas as pl
from jax.experimental.pallas import tpu as pltpu

F32 = jnp.float32
BF16 = jnp.bfloat16

D_MODEL = 1024
S5_WIDTH = 512
S5_GROUP = 16
S5_GROUPS = 32
S5_STATE = 64
GDN_WIDTH = 512
GDN_HEAD_DIM = 128
GDN_HEADS = 4
CONV_WIDTH = 4
CONV_CH = 3 * GDN_WIDTH
D_FF = 2816
LN_EPS = 1e-5
RMS_EPS = 1e-6
DEPTH = 1
ALPHA = (2 * DEPTH) ** 0.25

LANES = 128
MXU_TILE = 256
S5_T = 8
S5_BLK_GROUPS = LANES // S5_GROUP
S5_NBLK = S5_GROUPS // S5_BLK_GROUPS
S5_BLK_STATE = S5_BLK_GROUPS * S5_STATE
SUBLANES = 8
S5_POW_ROWS = 2 * SUBLANES
IN_PAD = S5_WIDTH + 4 * GDN_WIDTH + LANES
VMEM_LIMIT = 56 * 1024 * 1024


def _cparams(*sem):
    return pltpu.CompilerParams(dimension_semantics=sem, vmem_limit_bytes=VMEM_LIMIT)


def _ln(x):
    mu = jnp.mean(x, -1, keepdims=True)
    xc = x - mu
    var = jnp.mean(xc * xc, -1, keepdims=True)
    return xc * lax.rsqrt(var + LN_EPS)


def _sigmoid(x):
    return 1.0 / (1.0 + jnp.exp(-x))


def _silu(x):
    return x * _sigmoid(x)


def _split_bf16(x):
    hi = x.astype(BF16)
    return hi, (x - hi.astype(F32)).astype(BF16)


def _ada_kernel(c_ref, w_ref, b_ref, o_ref):
    c_hi, c_lo = _split_bf16(_silu(c_ref[...]))
    w_hi, w_lo = _split_bf16(w_ref[...])
    dot = functools.partial(jnp.dot, preferred_element_type=F32)
    o_ref[...] = dot(c_hi, w_hi) + (dot(c_hi, w_lo) + dot(c_lo, w_hi)) + b_ref[...]


def _ada(c, w_ada, b_ada):
    rows = c.shape[0]
    ncol = w_ada.shape[1]
    blk = D_MODEL
    return pl.pallas_call(
        _ada_kernel,
        grid=(ncol // blk,),
        in_specs=[pl.BlockSpec((rows, D_MODEL), lambda j: (0, 0)),
                  pl.BlockSpec((D_MODEL, blk), lambda j: (0, j)),
                  pl.BlockSpec((1, blk), lambda j: (0, j))],
        out_specs=pl.BlockSpec((rows, blk), lambda j: (0, j)),
        out_shape=jax.ShapeDtypeStruct((rows, ncol), F32),
        compiler_params=_cparams("arbitrary"),
        name="ada",
    )(c, w_ada, b_ada.reshape(1, ncol))


def _s5prep_kernel(are_ref, aim_ref, ldt_ref, btre_ref, btim_ref, ctre_ref, ctim_ref,
                   w_ref, p_ref, qt_ref, at_ref):
    T, half = S5_T, S5_BLK_STATE
    a_re = are_ref[0]
    a_im = aim_ref[0]
    dt = jnp.exp(ldt_ref[0])
    den = a_re * a_re + a_im * a_im

    def lam_pow(tau):
        mag = jnp.exp(a_re * dt * tau)
        return mag * jnp.cos(a_im * dt * tau), mag * jnp.sin(a_im * dt * tau)

    l_re, l_im = lam_pow(1.0)
    n_re, n_im = l_re - 1.0, l_im
    f_re = (n_re * a_re + n_im * a_im) / den
    f_im = (n_im * a_re - n_re * a_im) / den
    same_group = (lax.broadcasted_iota(jnp.int32, (LANES, half), 0) // S5_GROUP
                  == lax.broadcasted_iota(jnp.int32, (LANES, half), 1) // S5_STATE)
    bm_re = jnp.where(same_group, btre_ref[0], 0.0)
    bm_im = jnp.where(same_group, btim_ref[0], 0.0)
    cm_re = jnp.where(same_group, ctre_ref[0], 0.0)
    cm_im = jnp.where(same_group, ctim_ref[0], 0.0)
    nt = lambda a, b: lax.dot_general(a, b, (((1,), (1,)), ((), ())), preferred_element_type=F32)
    cm_re_s, cm_im_s = _split_bf16(cm_re), _split_bf16(cm_im)

    def dot_nt(a, b_split):
        a_hi, a_lo = _split_bf16(a)
        return nt(a_hi, b_split[0]) + (nt(a_hi, b_split[1]) + nt(a_lo, b_split[0]))
    blk = lambda i: slice(i * LANES, (i + 1) * LANES)
    for s in range(T):
        for t in range(s):
            w_ref[0, blk(s), blk(t)] = jnp.zeros((LANES, LANES), BF16)
    m_t = lax.broadcasted_iota(jnp.int32, (S5_POW_ROWS, half), 0).astype(F32) * float(T)
    mag = jnp.exp(a_re * dt * m_t)
    at_ref[0, :, :half] = mag * jnp.cos(a_im * dt * m_t)
    at_ref[0, :, half:] = mag * jnp.sin(a_im * dt * m_t)
    for tau in range(T + 1):
        p_re, p_im = (jnp.ones_like(a_re), jnp.zeros_like(a_re)) if tau == 0 else lam_pow(float(tau))
        if tau < T:
            g_re = p_re * f_re - p_im * f_im
            g_im = p_re * f_im + p_im * f_re
            inj_re = g_re * bm_re - g_im * bm_im
            inj_im = g_re * bm_im + g_im * bm_re
            s = T - 1 - tau
            p_ref[0, blk(s), :half] = inj_re.astype(BF16)
            p_ref[0, blk(s), half:] = inj_im.astype(BF16)
            k_tau = (dot_nt(inj_re, cm_re_s) - dot_nt(inj_im, cm_im_s)).astype(BF16)
            for s in range(T - tau):
                w_ref[0, blk(s), blk(s + tau)] = k_tau
        if tau >= 1:
            qt_ref[0, blk(tau - 1), :half] = (cm_re * p_re - cm_im * p_im).astype(BF16)
            qt_ref[0, blk(tau - 1), half:] = (-(cm_re * p_im + cm_im * p_re)).astype(BF16)


def _s5_prep(a_re, a_im, log_dt, b_re, b_im, c_re, c_im):
    NB, T, half = S5_NBLK, S5_T, S5_BLK_STATE
    tl, st = T * LANES, 2 * half
    lanes = lambda t: t.reshape(NB, 1, half)
    tiled = lambda t: jnp.tile(t.reshape(NB, LANES, S5_STATE), (1, 1, S5_BLK_GROUPS))
    row = pl.BlockSpec((1, 1, half), lambda k: (k, 0, 0))
    mat = pl.BlockSpec((1, LANES, half), lambda k: (k, 0, 0))
    return pl.pallas_call(
        _s5prep_kernel,
        grid=(NB,),
        in_specs=[row, row, row, mat, mat, mat, mat],
        out_specs=[pl.BlockSpec((1, tl, tl), lambda k: (k, 0, 0)),
                   pl.BlockSpec((1, tl, st), lambda k: (k, 0, 0)),
                   pl.BlockSpec((1, tl, st), lambda k: (k, 0, 0)),
                   pl.BlockSpec((1, S5_POW_ROWS, st), lambda k: (k, 0, 0))],
        out_shape=[jax.ShapeDtypeStruct((NB, tl, tl), BF16),
                   jax.ShapeDtypeStruct((NB, tl, st), BF16),
                   jax.ShapeDtypeStruct((NB, tl, st), BF16),
                   jax.ShapeDtypeStruct((NB, S5_POW_ROWS, st), F32)],
        compiler_params=_cparams("arbitrary"),
        name="s5prep",
    )(lanes(a_re), lanes(a_im), lanes(jnp.repeat(log_dt, S5_STATE)),
      tiled(jnp.swapaxes(b_re, 1, 2)), tiled(jnp.swapaxes(b_im, 1, 2)), tiled(c_re), tiled(c_im))


def _s5_kernel(u_ref, h0_ref, w_ref, p_ref, q_ref, at_ref, d_ref, y_ref, hT_ref,
               lhs_ref, x_ref, hp_ref, *, bt, n):
    T, half = S5_T, S5_BLK_STATE
    for b in range(bt):
        for t in range(T):
            lhs_ref[b * n:(b + 1) * n, t * LANES:(t + 1) * LANES] = (
                u_ref[b, pl.ds(t, n, stride=T), :].astype(BF16))
    lhs = lhs_ref[...]
    x_ref[...] = jnp.dot(lhs, p_ref[0], preferred_element_type=F32)
    cplx = lambda m: (at_ref[0, m:m + 1, :half], at_ref[0, m:m + 1, half:])
    rowi = lax.broadcasted_iota(jnp.int32, (SUBLANES, half), 0)
    shifts = []
    d = 1
    while d < SUBLANES:
        p_re, p_im = cplx(d)
        shifts.append((d, jnp.where(rowi >= d, p_re, 0.0), jnp.where(rowi >= d, p_im, 0.0)))
        d *= 2
    row_re = at_ref[0, 0:SUBLANES, :half]
    row_im = at_ref[0, 0:SUBLANES, half:]
    full_re, full_im = cplx(SUBLANES)
    y_in = jnp.concatenate(
        [jnp.dot(lhs[:, :c + MXU_TILE], w_ref[0, :c + MXU_TILE, c:c + MXU_TILE],
                 preferred_element_type=F32) for c in range(0, T * LANES, MXU_TILE)], axis=1)
    for b in range(bt):
        def tile(i, c):
            c_re, c_im = c
            rows = slice(b * n + i * SUBLANES, b * n + (i + 1) * SUBLANES)
            y_re = x_ref[rows, :half]
            y_im = x_ref[rows, half:]
            for d, m_re, m_im in shifts:
                s_re = pltpu.roll(y_re, d, axis=0)
                s_im = pltpu.roll(y_im, d, axis=0)
                y_re, y_im = y_re + (m_re * s_re - m_im * s_im), y_im + (m_re * s_im + m_im * s_re)
            e_re = jnp.where(rowi == 0, 0.0, pltpu.roll(y_re, 1, axis=0))
            e_im = jnp.where(rowi == 0, 0.0, pltpu.roll(y_im, 1, axis=0))
            hp_ref[rows, :half] = e_re + (row_re * c_re - row_im * c_im)
            hp_ref[rows, half:] = e_im + (row_re * c_im + row_im * c_re)
            last = SUBLANES - 1
            return (y_re[last:] + (full_re * c_re - full_im * c_im),
                    y_im[last:] + (full_re * c_im + full_im * c_re))
        c = (h0_ref[b, 0, :, :half], h0_ref[b, 0, :, half:])
        for i in range(n // SUBLANES):
            c = tile(i, c)
        hT_ref[b, 0, :, :half] = c[0]
        hT_ref[b, 0, :, half:] = c[1]
    y = (y_in
         + lax.dot_general(hp_ref[...].astype(BF16), q_ref[0], (((1,), (1,)), ((), ())),
                           preferred_element_type=F32))
    d = d_ref[...]
    for b in range(bt):
        for t in range(T):
            y_ref[b, pl.ds(t, n, stride=T), :] = (
                y[b * n:(b + 1) * n, t * LANES:(t + 1) * LANES]
                + d * u_ref[b, pl.ds(t, n, stride=T), :])


def _s5(u, h0, w, pmat, qmat, a_t, d_skip, bt):
    B, L, _ = u.shape
    n = L // S5_T
    tl, st = S5_T * LANES, 2 * S5_BLK_STATE
    assert n % SUBLANES == 0
    return pl.pallas_call(
        functools.partial(_s5_kernel, bt=bt, n=n),
        grid=(S5_NBLK, B // bt),
        in_specs=[pl.BlockSpec((bt, L, LANES), lambda k, b: (b, 0, k)),
                  pl.BlockSpec((bt, 1, 1, st), lambda k, b: (b, k, 0, 0)),
                  pl.BlockSpec((1, tl, tl), lambda k, b: (k, 0, 0)),
                  pl.BlockSpec((1, tl, st), lambda k, b: (k, 0, 0)),
                  pl.BlockSpec((1, tl, st), lambda k, b: (k, 0, 0)),
                  pl.BlockSpec((1, S5_POW_ROWS, st), lambda k, b: (k, 0, 0)),
                  pl.BlockSpec((1, LANES), lambda k, b: (0, k))],
        out_specs=[pl.BlockSpec((bt, L, LANES), lambda k, b: (b, 0, k)),
                   pl.BlockSpec((bt, 1, 1, st), lambda k, b: (b, k, 0, 0))],
        out_shape=[jax.ShapeDtypeStruct((B, L, S5_WIDTH), F32),
                   jax.ShapeDtypeStruct((B, S5_NBLK, 1, st), F32)],
        scratch_shapes=[pltpu.VMEM((bt * n, tl), BF16),
                        pltpu.VMEM((bt * n, st), F32),
                        pltpu.VMEM((bt * n, st), F32)],
        compiler_params=_cparams("arbitrary", "arbitrary"),
        name="s5",
    )(u, h0, w, pmat, qmat, a_t, d_skip.reshape(1, S5_WIDTH))


def _s5_state_in(h_re, h_im):
    B = h_re.shape[0]
    r = h_re.reshape(B, S5_NBLK, 1, S5_BLK_STATE)
    i = h_im.reshape(B, S5_NBLK, 1, S5_BLK_STATE)
    return jnp.concatenate([r, i], -1)


def _s5_state_out(h):
    B = h.shape[0]
    h = h.reshape(B, S5_NBLK, 2, S5_BLK_GROUPS, S5_STATE)
    return (h[:, :, 0].reshape(B, S5_GROUPS, S5_STATE), h[:, :, 1].reshape(B, S5_GROUPS, S5_STATE))


CONV_TAIL = SUBLANES


def _inproj_kernel(x_ref, sh_ref, sc_ref, w_ref, cb_ref, cw_ref,
                   u_ref, conv_ref, z_ref, ba_ref, cbo_ref, *xbuf_refs):
    bt, lt, d = x_ref.shape
    keep = CONV_WIDTH - 1
    o_q, o_z, o_b = S5_WIDTH, S5_WIDTH + CONV_CH, S5_WIDTH + CONV_CH + GDN_WIDTH
    nparts = CONV_CH // GDN_WIDTH
    parts = [slice(p * GDN_WIDTH, (p + 1) * GDN_WIDTH) for p in range(nparts)]
    nun = len(xbuf_refs) // nparts
    ub = bt // nun
    units = [slice(i * ub, (i + 1) * ub) for i in range(nun)]
    xbufs = [xbuf_refs[i * nparts:(i + 1) * nparts] for i in range(nun)]

    @pl.when(pl.program_id(1) == 0)
    def _():
        for us, bufs in zip(units, xbufs):
            for cs, xbuf_ref in zip(parts, bufs):
                xbuf_ref[:, 0:CONV_TAIL, :] = jnp.zeros((ub, CONV_TAIL, GDN_WIDTH), F32)
                xbuf_ref[:, CONV_TAIL - keep:CONV_TAIL, :] = cb_ref[us, :, cs]

    hs = []
    for us in units:
        h = _ln(x_ref[us]) * (1.0 + sc_ref[us, 0]) + sh_ref[us, 0]
        hs.append(h.reshape(ub * lt, d).astype(BF16))
    for p, cs in enumerate(parts):
        for h, bufs in zip(hs, xbufs):
            bufs[p][:, CONV_TAIL:CONV_TAIL + lt, :] = (
                _bdot(h, w_ref[:, o_q + cs.start:o_q + cs.stop]).reshape(ub, lt, GDN_WIDTH))
    for us, h in zip(units, hs):
        u_ref[us] = _bdot(h, w_ref[:, :o_q]).reshape(ub, lt, S5_WIDTH)
    for us, h in zip(units, hs):
        z_ref[us] = _bdot(h, w_ref[:, o_z:o_b]).reshape(ub, lt, GDN_WIDTH)
        ba_ref[us] = _bdot(h, w_ref[:, o_b:]).reshape(ub, lt, LANES)
    rows = lt + CONV_TAIL
    for us, bufs in zip(units, xbufs):
        for cs, xbuf_ref in zip(parts, bufs):
            xfull = xbuf_ref[...]
            conv = xfull[:, CONV_TAIL:, :] * cw_ref[keep:keep + 1, cs]
            for j in range(keep):
                shifted = pltpu.roll(xfull, rows - (CONV_TAIL - keep + j), axis=1)
                conv = conv + shifted[:, :lt, :] * cw_ref[j:j + 1, cs]
            conv_ref[us, :, cs] = conv
            cbo_ref[us, :, cs] = xbuf_ref[:, lt + CONV_TAIL - keep:lt + CONV_TAIL, :]
            xbuf_ref[:, 0:CONV_TAIL, :] = xbuf_ref[:, lt:lt + CONV_TAIL, :]


def _inproj(x, mod, w_in_p, conv_buf, conv_w, bt, lt):
    B, L, _ = x.shape
    nun = 2 if bt % 2 == 0 else 1
    xs = lambda w: pl.BlockSpec((bt, lt, w), lambda b, i: (b, i, 0))
    ms = lambda idx: pl.BlockSpec((bt, 1, 1, D_MODEL), lambda b, i: (b, idx, 0, 0))
    cache = pl.BlockSpec((bt, CONV_WIDTH - 1, CONV_CH), lambda b, i: (b, 0, 0))
    return pl.pallas_call(
        _inproj_kernel,
        grid=(B // bt, L // lt),
        in_specs=[xs(D_MODEL), ms(0), ms(1),
                  pl.BlockSpec((D_MODEL, IN_PAD), lambda b, i: (0, 0)), cache,
                  pl.BlockSpec((CONV_WIDTH, CONV_CH), lambda b, i: (0, 0))],
        out_specs=[xs(S5_WIDTH), xs(CONV_CH), xs(GDN_WIDTH), xs(LANES), cache],
        out_shape=[jax.ShapeDtypeStruct((B, L, w), F32) for w in (S5_WIDTH, CONV_CH, GDN_WIDTH, LANES)]
                  + [jax.ShapeDtypeStruct((B, CONV_WIDTH - 1, CONV_CH), F32)],
        scratch_shapes=[pltpu.VMEM((bt // nun, lt + CONV_TAIL, GDN_WIDTH), F32)
                        for _ in range(nun * (CONV_CH // GDN_WIDTH))],
        compiler_params=_cparams("arbitrary", "arbitrary"),
        name="inproj",
    )(x, mod, mod, w_in_p, conv_buf, conv_w)


def _mm(a, b):
    return jnp.dot(a.astype(BF16), b.astype(BF16), preferred_element_type=F32)


GDN_INV_BASE = 2
GDN_INV_JOIN = 4


def _bdot(a, b):
    return jnp.dot(a, b, preferred_element_type=F32)


def _unit_lower_inverse(a_list, ri, ci, size):
    eye_f = (ri == ci).astype(F32)
    blk = lambda s: (ri // s) == (ci // s)
    prev = min(GDN_INV_BASE, size)
    base_mask = blk(prev)
    d_list = [(eye_f - jnp.where(base_mask, a, 0.0)).astype(BF16) for a in a_list]
    while prev < size:
        cur = min(prev * GDN_INV_JOIN, size)
        off_mask = blk(cur) & jnp.logical_not(blk(prev))
        ms = [_bdot(d, jnp.where(off_mask, a, 0.0).astype(BF16)) for a, d in zip(a_list, d_list)]
        rs = [eye_f - m for m in ms]
        mps = [m.astype(BF16) for m in ms]
        span = 2
        while span < cur // prev:
            mps = [_bdot(mp, mp).astype(BF16) for mp in mps]
            rs = [r + _bdot(r.astype(BF16), mp) for r, mp in zip(rs, mps)]
            span *= 2
        d_list = [_bdot(r.astype(BF16), d).astype(BF16) for r, d in zip(rs, d_list)]
        prev = cur
    return d_list


def _gdn_kernel(conv_ref, z_ref, ba_ref, s0_ref, alog_ref, dtb_ref, ng_ref,
                y_ref, sT_ref, s_ref, *, chunk, nchunk, bt):
    C, Dh, H = chunk, GDN_HEAD_DIM, GDN_HEADS

    @pl.when(pl.program_id(1) == 0)
    def _():
        s_ref[...] = s0_ref[...]

    act_all = _silu(conv_ref[...])
    ba = ba_ref[...]
    beta_all = _sigmoid(ba)
    xa = ba + dtb_ref[...]
    softplus = jnp.maximum(xa, 0.0) + jnp.log1p(jnp.exp(-jnp.abs(xa)))
    g = -jnp.exp(alog_ref[...]) * softplus

    ri = lax.broadcasted_iota(jnp.int32, (C, C), 0)
    ci = lax.broadcasted_iota(jnp.int32, (C, C), 1)
    causal = ri >= ci
    strict = ri > ci
    eye = ri == ci
    ng = ng_ref[...]
    tril = causal.astype(BF16)
    g_hi = g.astype(BF16)
    g_rest = g - g_hi.astype(F32)
    g_mid = g_rest.astype(BF16)
    g_lo = (g_rest - g_mid.astype(F32)).astype(BF16)
    g3 = jnp.concatenate([g_hi, g_mid, g_lo], axis=2)
    units = [(c, b, h) for c in range(nchunk) for b in range(bt) for h in range(H)]
    rows = lambda c: slice(c * C, (c + 1) * C)
    decay_all = {}
    for c in range(nchunk):
        for b in range(bt):
            d3 = _bdot(tril, g3[b, rows(c)])
            decay_all[c, b] = d3[:, :LANES] + (d3[:, LANES:2 * LANES] + d3[:, 2 * LANES:])

    e_list, kbq_list, kT_list, rhs_list, qd_list, kdT_list, gl_list = [], [], [], [], [], [], []
    for c, b, h in units:
        decay, beta, act = decay_all[c, b], beta_all[b, rows(c)], act_all[b, rows(c)]
        dcol = decay[:, H + h:H + h + 1]
        dmat = jnp.broadcast_to(dcol, (C, C))
        drow = jnp.sum(jnp.where(eye, dmat, 0.0), axis=0, keepdims=True)
        e = jnp.exp(dmat - drow)
        bcol = beta[:, h:h + 1]
        q = act[:, h * Dh:(h + 1) * Dh]
        k = act[:, GDN_WIDTH + h * Dh:GDN_WIDTH + (h + 1) * Dh]
        v = act[:, 2 * GDN_WIDTH + h * Dh:2 * GDN_WIDTH + (h + 1) * Dh]
        q = q * (lax.rsqrt(jnp.sum(q * q, -1, keepdims=True) + RMS_EPS) * (Dh ** -0.5))
        k = k * lax.rsqrt(jnp.sum(k * k, -1, keepdims=True) + RMS_EPS)
        kb = k * bcol
        kT = k.T
        kT_b = kT.astype(BF16)
        edec = jnp.exp(dcol)
        dlast = decay[C - 1:C, H + h:H + h + 1]
        e_list.append(e)
        kbq_list.append(jnp.concatenate([kb, q], axis=0).astype(BF16))
        kT_list.append(kT_b)
        rhs_list.append(jnp.concatenate([v * bcol, kb * edec], axis=1).astype(BF16))
        qd_list.append((q * edec).astype(BF16))
        kdT_list.append((kT * jnp.exp(dlast - drow)).astype(BF16))
        gl_list.append(jnp.exp(dlast))
    sc_list = [_bdot(kbq, kT_b) for kbq, kT_b in zip(kbq_list, kT_list)]
    a_list = [jnp.where(strict, sc[:C] * e, 0.0) for sc, e in zip(sc_list, e_list)]
    qk_list = [jnp.where(causal, sc[C:] * e, 0.0).astype(BF16) for sc, e in zip(sc_list, e_list)]

    tinv_list = _unit_lower_inverse(a_list, ri, ci, C)

    uw_list = [_bdot(t, rhs) for t, rhs in zip(tinv_list, rhs_list)]
    state = {(b, h): s_ref[b, h] for b in range(bt) for h in range(H)}
    for c in range(nchunk):
        idx = [i for i, u in enumerate(units) if u[0] == c]
        ws = {i: _bdot(jnp.concatenate([uw_list[i][:, Dh:].astype(BF16), qd_list[i]], axis=0),
                       state[units[i][1:]].astype(BF16)) for i in idx}
        vn = {i: (uw_list[i][:, :Dh] - ws[i][:C]).astype(BF16) for i in idx}
        ov = {i: _bdot(jnp.concatenate([qk_list[i], kdT_list[i]], axis=0), vn[i]) for i in idx}
        for i in idx:
            state[units[i][1:]] = state[units[i][1:]] * gl_list[i] + ov[i][C:]
        for i in idx:
            _, b, h = units[i]
            o = ws[i][C:] + ov[i][:C]
            o = o * lax.rsqrt(jnp.mean(o * o, -1, keepdims=True) + RMS_EPS) * ng
            y_ref[b, rows(c), h * Dh:(h + 1) * Dh] = o * _silu(z_ref[b, rows(c), h * Dh:(h + 1) * Dh])
    for (b, h), s in state.items():
        s_ref[b, h] = s

    sT_ref[...] = s_ref[...]


def _gdn(conv, z, ba, s0, a_log, dt_bias, norm_g, chunk, nchunk, bt):
    B, L, _ = conv.shape
    H, Dh = GDN_HEADS, GDN_HEAD_DIM
    lt = chunk * nchunk
    place = lambda t: jnp.zeros((1, LANES), F32).at[0, H:2 * H].set(t)
    xs = lambda w: pl.BlockSpec((bt, lt, w), lambda b, i: (b, i, 0))
    full = lambda *shape: pl.BlockSpec(shape, lambda b, i: (0,) * len(shape))
    state = pl.BlockSpec((bt, H, Dh, Dh), lambda b, i: (b, 0, 0, 0))
    return pl.pallas_call(
        functools.partial(_gdn_kernel, chunk=chunk, nchunk=nchunk, bt=bt),
        grid=(B // bt, L // lt),
        in_specs=[xs(CONV_CH), xs(GDN_WIDTH), xs(LANES), state,
                  full(1, LANES), full(1, LANES), full(1, Dh)],
        out_specs=[xs(GDN_WIDTH), state],
        out_shape=[jax.ShapeDtypeStruct((B, L, GDN_WIDTH), F32),
                   jax.ShapeDtypeStruct((B, H, Dh, Dh), F32)],
        scratch_shapes=[pltpu.VMEM((bt, H, Dh, Dh), F32)],
        compiler_params=_cparams("arbitrary", "arbitrary"),
        name="gdn",
    )(conv, z, ba, s0, place(a_log), place(dt_bias), norm_g.reshape(1, Dh))


FF_BLK = 256


def _gelu_tanh(x):
    return 0.5 * x * (1.0 + jnp.tanh(0.7978845608028654 * (x + 0.044715 * x * x * x)))


def _outffn_kernel(x_ref, ya_ref, yb_ref, g1_ref, sh2_ref, sc2_ref, g2_ref,
                   wglu_ref, wout_ref, l1g_ref, l1b_ref, wup_ref, wdn_ref, l2g_ref, l2b_ref,
                   o_ref):
    bt, lt, d = x_ref.shape
    if bt % 2 == 0:
        subs = [(slice(i * bt // 2, (i + 1) * bt // 2), slice(0, lt)) for i in range(2)]
    else:
        subs = [(slice(0, bt), slice(i * lt // 2, (i + 1) * lt // 2)) for i in range(2)]
    rd = lambda ref, sub: ref[sub[0], sub[1], :]
    flat = lambda t: t.reshape(t.shape[0] * t.shape[1], t.shape[-1])
    unflat = lambda t, like: t.reshape(like.shape[0], like.shape[1], t.shape[-1])
    mod = lambda ref, sub: ref[sub[0], 0]
    nsub = range(len(subs))

    zz = [_mm(_gelu_tanh(flat(rd(ya_ref, s))), wglu_ref[...]) for s in subs]
    y_a = [z[:, :S5_WIDTH] * _sigmoid(z[:, S5_WIDTH:]) for z in zz]
    mix_b = [_mm(flat(rd(yb_ref, s)), wout_ref[S5_WIDTH:, :]) for s in subs]
    mix = [mb + _mm(ya, wout_ref[:S5_WIDTH, :]) for mb, ya in zip(mix_b, y_a)]
    xs = [rd(x_ref, s) for s in subs]
    x1 = [_ln(ALPHA * x + (1.0 + mod(g1_ref, s)) * unflat(mx, x)) * l1g_ref[...] + l1b_ref[...]
          for x, mx, s in zip(xs, mix, subs)]
    h = [flat(_ln(x) * (1.0 + mod(sc2_ref, s)) + mod(sh2_ref, s)).astype(BF16) for x, s in zip(x1, subs)]

    def gate_up(i, j):
        gate = _bdot(h[i], wup_ref[:, j * FF_BLK:(j + 1) * FF_BLK])
        up = _bdot(h[i], wup_ref[:, D_FF + j * FF_BLK:D_FF + (j + 1) * FF_BLK])
        return gate, up

    nblk = D_FF // FF_BLK
    acc = [None for _ in nsub]
    gu = [gate_up(i, 0) for i in nsub]
    for j in range(nblk):
        gu_next = [gate_up(i, j + 1) for i in nsub] if j + 1 < nblk else None
        for i in nsub:
            gate, up = gu[i]
            part = _mm(_silu(gate) * up, wdn_ref[j * FF_BLK:(j + 1) * FF_BLK, :])
            acc[i] = part if acc[i] is None else acc[i] + part
        gu = gu_next
    for i, s in enumerate(subs):
        o_ref[s[0], s[1], :] = (_ln(ALPHA * x1[i] + (1.0 + mod(g2_ref, s)) * unflat(acc[i], x1[i]))
                                * l2g_ref[...] + l2b_ref[...])


def _outffn(x, ya, yb, mod, w_glu, w_out, ln1_g, ln1_b, w_up, w_dn, ln2_g, ln2_b, bt, lt):
    B, L, _ = x.shape
    xs = lambda w: pl.BlockSpec((bt, lt, w), lambda b, i: (b, i, 0))
    ms = lambda idx: pl.BlockSpec((bt, 1, 1, D_MODEL), lambda b, i: (b, idx, 0, 0))
    const = lambda a: pl.BlockSpec(a.shape, lambda b, i: (0,) * a.ndim, pipeline_mode=pl.Buffered(1))
    vec = lambda t: t.reshape(1, D_MODEL)
    weights = (w_glu, w_out, vec(ln1_g), vec(ln1_b), w_up, w_dn, vec(ln2_g), vec(ln2_b))
    return pl.pallas_call(
        _outffn_kernel,
        grid=(B // bt, L // lt),
        in_specs=[xs(D_MODEL), xs(S5_WIDTH), xs(GDN_WIDTH), ms(2), ms(3), ms(4), ms(5)]
                 + [const(a) for a in weights],
        out_specs=xs(D_MODEL),
        out_shape=jax.ShapeDtypeStruct((B, L, D_MODEL), F32),
        compiler_params=_cparams("arbitrary", "arbitrary"),
        name="outffn",
    )(x, ya, yb, mod, mod, mod, mod, *weights)


TOKEN_TILE = 2 * MXU_TILE
GDN_CHUNK = LANES
GDN_CHAINS_SEQ = 4
GDN_CHUNKS_STEP = 2


def _plan(B, L):
    if L >= TOKEN_TILE:
        bt, lt = 1, TOKEN_TILE
    else:
        bt, lt = min(B, TOKEN_TILE // L), L
    assert B % bt == 0 and L % lt == 0
    gdn_chunk = min(L, GDN_CHUNK)
    gdn_nchunk = min(L // gdn_chunk, GDN_CHUNKS_STEP)
    gdn_bt = min(B, GDN_CHAINS_SEQ)
    assert L % (gdn_chunk * gdn_nchunk) == 0 and B % gdn_bt == 0
    return bt, lt, gdn_chunk, gdn_nchunk, gdn_bt


def _layer(x, mod, h_re, h_im, s_gdn, conv_buf, wts):
    (w_in_p, s5_mats, s5_d, w_glu, conv_w, a_log, dt_bias, norm_g, w_out,
     ln1_g, ln1_b, w_up, w_dn, ln2_g, ln2_b) = wts
    bt, lt, gdn_chunk, gdn_nchunk, gdn_bt = _plan(x.shape[0], x.shape[1])
    u, conv, z, ba, cbo = _inproj(x, mod, w_in_p, conv_buf, conv_w, bt, lt)
    w, pmat, qmat, a_t = s5_mats
    ys5, hT = _s5(u, _s5_state_in(h_re, h_im), w, pmat, qmat, a_t, s5_d, bt)
    yb, sT = _gdn(conv, z, ba, s_gdn, a_log, dt_bias, norm_g, gdn_chunk, gdn_nchunk, gdn_bt)
    y = _outffn(x, ys5, yb, mod, w_glu, w_out, ln1_g, ln1_b, w_up, w_dn, ln2_g, ln2_b, bt, lt)
    o_re, o_im = _s5_state_out(hT)
    return y, o_re, o_im, sT, cbo


def kernel(x_prompt, x_sample, state_s5_re, state_s5_im, state_gdn, cache_gdn_conv, c_prompt, c_sample, w_ada, b_ada, w_in, s5_a_re, s5_a_im, s5_log_dt, s5_b_re, s5_b_im, s5_c_re, s5_c_im, s5_d, w_s5_glu, gdn_conv_w, gdn_a_log, gdn_dt_bias, gdn_norm_g, w_out, ln1_g, ln1_b, w_ffn_up, w_ffn_down, ln2_g, ln2_b):
    bp, bs = x_prompt.shape[0], x_sample.shape[0]
    assert w_ada.shape[0] == DEPTH == 1
    l = 0
    c_all = jnp.concatenate([c_prompt, c_sample], 0)
    c_all = jnp.pad(c_all, ((0, -(bp + bs) % (2 * SUBLANES)), (0, 0)))
    mod = _ada(c_all, w_ada[l], b_ada[l])[:bp + bs]
    mod = mod.reshape(bp + bs, 6, 1, D_MODEL)
    w_in_p = jnp.pad(w_in[l], ((0, 0), (0, IN_PAD - w_in.shape[-1]))).astype(BF16)
    s5_mats = _s5_prep(s5_a_re[l], s5_a_im[l], s5_log_dt[l], s5_b_re[l], s5_b_im[l],
                       s5_c_re[l], s5_c_im[l])
    wts = (w_in_p, s5_mats, s5_d[l], w_s5_glu[l].astype(BF16), gdn_conv_w[l], gdn_a_log[l],
           gdn_dt_bias[l], gdn_norm_g[l], w_out[l].astype(BF16), ln1_g[l], ln1_b[l],
           w_ffn_up[l].astype(BF16), w_ffn_down[l].astype(BF16), ln2_g[l], ln2_b[l])
    zeros = lambda *s: jnp.zeros(s, F32)
    yp, p_re, p_im, p_gdn, p_conv = _layer(
        x_prompt, mod[:bp],
        zeros(bp, S5_GROUPS, S5_STATE), zeros(bp, S5_GROUPS, S5_STATE),
        zeros(bp, GDN_HEADS, GDN_HEAD_DIM, GDN_HEAD_DIM), zeros(bp, CONV_WIDTH - 1, CONV_CH),
        wts)
    ys, s_re, s_im, s_gdn, s_conv = _layer(
        x_sample, mod[bp:], state_s5_re[l], state_s5_im[l], state_gdn[l], cache_gdn_conv[l],
        wts)
    st = lambda t: t[None]
    return (yp, ys, st(p_re), st(p_im), st(p_gdn), st(p_conv),
            st(s_re), st(s_im), st(s_gdn), st(s_conv))
```

```python
import functools

import jax
import jax.numpy as jnp
from jax import lax
from jax.experimental import pallas as pl
from jax.experimental.pallas import tpu as pltpu

F32 = jnp.float32
BF16 = jnp.bfloat16

D_MODEL = 1024
S5_WIDTH = 512
S5_GROUP = 16
S5_GROUPS = 32
S5_STATE = 64
GDN_WIDTH = 512
GDN_HEAD_DIM = 128
GDN_HEADS = 4
CONV_WIDTH = 4
CONV_CH = 3 * GDN_WIDTH
D_FF = 2816
LN_EPS = 1e-5
RMS_EPS = 1e-6
DEPTH = 1
ALPHA = (2 * DEPTH) ** 0.25

LANES = 128
MXU_TILE = 256
S5_T = 8
S5_BLK_GROUPS = LANES // S5_GROUP
S5_NBLK = S5_GROUPS // S5_BLK_GROUPS
S5_BLK_STATE = S5_BLK_GROUPS * S5_STATE
SUBLANES = 8
S5_POW_ROWS = 2 * SUBLANES
IN_PAD = S5_WIDTH + 4 * GDN_WIDTH + LANES
VMEM_LIMIT = 56 * 1024 * 1024


def _cparams(*sem):
    return pltpu.CompilerParams(dimension_semantics=sem, vmem_limit_bytes=VMEM_LIMIT)


def _ln(x):
    mu = jnp.mean(x, -1, keepdims=True)
    xc = x - mu
    var = jnp.mean(xc * xc, -1, keepdims=True)
    return xc * lax.rsqrt(var + LN_EPS)


def _sigmoid(x):
    return 1.0 / (1.0 + jnp.exp(-x))


def _silu(x):
    return x * _sigmoid(x)


def _split_bf16(x):
    hi = x.astype(BF16)
    return hi, (x - hi.astype(F32)).astype(BF16)


def _ada_kernel(c_ref, w_ref, b_ref, o_ref):
    c_hi, c_lo = _split_bf16(_silu(c_ref[...]))
    w_hi, w_lo = _split_bf16(w_ref[...])
    dot = functools.partial(jnp.dot, preferred_element_type=F32)
    o_ref[...] = dot(c_hi, w_hi) + (dot(c_hi, w_lo) + dot(c_lo, w_hi)) + b_ref[...]


def _ada(c, w_ada, b_ada):
    rows = c.shape[0]
    ncol = w_ada.shape[1]
    blk = D_MODEL
    return pl.pallas_call(
        _ada_kernel,
        grid=(ncol // blk,),
        in_specs=[pl.BlockSpec((rows, D_MODEL), lambda j: (0, 0)),
                  pl.BlockSpec((D_MODEL, blk), lambda j: (0, j)),
                  pl.BlockSpec((1, blk), lambda j: (0, j))],
        out_specs=pl.BlockSpec((rows, blk), lambda j: (0, j)),
        out_shape=jax.ShapeDtypeStruct((rows, ncol), F32),
        compiler_params=_cparams("arbitrary"),
        name="ada",
    )(c, w_ada, b_ada.reshape(1, ncol))


def _s5prep_kernel(are_ref, aim_ref, ldt_ref, btre_ref, btim_ref, ctre_ref, ctim_ref,
                   w_ref, p_ref, qt_ref, at_ref):
    T, half = S5_T, S5_BLK_STATE
    a_re = are_ref[0]
    a_im = aim_ref[0]
    dt = jnp.exp(ldt_ref[0])
    den = a_re * a_re + a_im * a_im

    def lam_pow(tau):
        mag = jnp.exp(a_re * dt * tau)
        return mag * jnp.cos(a_im * dt * tau), mag * jnp.sin(a_im * dt * tau)

    l_re, l_im = lam_pow(1.0)
    n_re, n_im = l_re - 1.0, l_im
    f_re = (n_re * a_re + n_im * a_im) / den
    f_im = (n_im * a_re - n_re * a_im) / den
    same_group = (lax.broadcasted_iota(jnp.int32, (LANES, half), 0) // S5_GROUP
                  == lax.broadcasted_iota(jnp.int32, (LANES, half), 1) // S5_STATE)
    bm_re = jnp.where(same_group, btre_ref[0], 0.0)
    bm_im = jnp.where(same_group, btim_ref[0], 0.0)
    cm_re = jnp.where(same_group, ctre_ref[0], 0.0)
    cm_im = jnp.where(same_group, ctim_ref[0], 0.0)
    nt = lambda a, b: lax.dot_general(a, b, (((1,), (1,)), ((), ())), preferred_element_type=F32)
    cm_re_s, cm_im_s = _split_bf16(cm_re), _split_bf16(cm_im)

    def dot_nt(a, b_split):
        a_hi, a_lo = _split_bf16(a)
        return nt(a_hi, b_split[0]) + (nt(a_hi, b_split[1]) + nt(a_lo, b_split[0]))
    blk = lambda i: slice(i * LANES, (i + 1) * LANES)
    for s in range(T):
        for t in range(s):
            w_ref[0, blk(s), blk(t)] = jnp.zeros((LANES, LANES), BF16)
    m_t = lax.broadcasted_iota(jnp.int32, (S5_POW_ROWS, half), 0).astype(F32) * float(T)
    mag = jnp.exp(a_re * dt * m_t)
    at_ref[0, :, :half] = mag * jnp.cos(a_im * dt * m_t)
    at_ref[0, :, half:] = mag * jnp.sin(a_im * dt * m_t)
    for tau in range(T + 1):
        p_re, p_im = (jnp.ones_like(a_re), jnp.zeros_like(a_re)) if tau == 0 else lam_pow(float(tau))
        if tau < T:
            g_re = p_re * f_re - p_im * f_im
            g_im = p_re * f_im + p_im * f_re
            inj_re = g_re * bm_re - g_im * bm_im
            inj_im = g_re * bm_im + g_im * bm_re
            s = T - 1 - tau
            p_ref[0, blk(s), :half] = inj_re.astype(BF16)
            p_ref[0, blk(s), half:] = inj_im.astype(BF16)
            k_tau = (dot_nt(inj_re, cm_re_s) - dot_nt(inj_im, cm_im_s)).astype(BF16)
            for s in range(T - tau):
                w_ref[0, blk(s), blk(s + tau)] = k_tau
        if tau >= 1:
            qt_ref[0, blk(tau - 1), :half] = (cm_re * p_re - cm_im * p_im).astype(BF16)
            qt_ref[0, blk(tau - 1), half:] = (-(cm_re * p_im + cm_im * p_re)).astype(BF16)


def _s5_prep(a_re, a_im, log_dt, b_re, b_im, c_re, c_im):
    NB, T, half = S5_NBLK, S5_T, S5_BLK_STATE
    tl, st = T * LANES, 2 * half
    lanes = lambda t: t.reshape(NB, 1, half)
    tiled = lambda t: jnp.tile(t.reshape(NB, LANES, S5_STATE), (1, 1, S5_BLK_GROUPS))
    row = pl.BlockSpec((1, 1, half), lambda k: (k, 0, 0))
    mat = pl.BlockSpec((1, LANES, half), lambda k: (k, 0, 0))
    return pl.pallas_call(
        _s5prep_kernel,
        grid=(NB,),
        in_specs=[row, row, row, mat, mat, mat, mat],
        out_specs=[pl.BlockSpec((1, tl, tl), lambda k: (k, 0, 0)),
                   pl.BlockSpec((1, tl, st), lambda k: (k, 0, 0)),
                   pl.BlockSpec((1, tl, st), lambda k: (k, 0, 0)),
                   pl.BlockSpec((1, S5_POW_ROWS, st), lambda k: (k, 0, 0))],
        out_shape=[jax.ShapeDtypeStruct((NB, tl, tl), BF16),
                   jax.ShapeDtypeStruct((NB, tl, st), BF16),
                   jax.ShapeDtypeStruct((NB, tl, st), BF16),
                   jax.ShapeDtypeStruct((NB, S5_POW_ROWS, st), F32)],
        compiler_params=_cparams("arbitrary"),
        name="s5prep",
    )(lanes(a_re), lanes(a_im), lanes(jnp.repeat(log_dt, S5_STATE)),
      tiled(jnp.swapaxes(b_re, 1, 2)), tiled(jnp.swapaxes(b_im, 1, 2)), tiled(c_re), tiled(c_im))


def _s5_kernel(u_ref, h0_ref, w_ref, p_ref, q_ref, at_ref, d_ref, y_ref, hT_ref,
               lhs_ref, x_ref, hp_ref, *, bt, n):
    T, half = S5_T, S5_BLK_STATE
    for b in range(bt):
        for t in range(T):
            lhs_ref[b * n:(b + 1) * n, t * LANES:(t + 1) * LANES] = (
                u_ref[b, pl.ds(t, n, stride=T), :].astype(BF16))
    lhs = lhs_ref[...]
    x_ref[...] = jnp.dot(lhs, p_ref[0], preferred_element_type=F32)
    cplx = lambda m: (at_ref[0, m:m + 1, :half], at_ref[0, m:m + 1, half:])
    rowi = lax.broadcasted_iota(jnp.int32, (SUBLANES, half), 0)
    shifts = []
    d = 1
    while d < SUBLANES:
        p_re, p_im = cplx(d)
        shifts.append((d, jnp.where(rowi >= d, p_re, 0.0), jnp.where(rowi >= d, p_im, 0.0)))
        d *= 2
    row_re = at_ref[0, 0:SUBLANES, :half]
    row_im = at_ref[0, 0:SUBLANES, half:]
    full_re, full_im = cplx(SUBLANES)
    y_in = jnp.concatenate(
        [jnp.dot(lhs[:, :c + MXU_TILE], w_ref[0, :c + MXU_TILE, c:c + MXU_TILE],
                 preferred_element_type=F32) for c in range(0, T * LANES, MXU_TILE)], axis=1)
    for b in range(bt):
        def tile(i, c):
            c_re, c_im = c
            rows = slice(b * n + i * SUBLANES, b * n + (i + 1) * SUBLANES)
            y_re = x_ref[rows, :half]
            y_im = x_ref[rows, half:]
            for d, m_re, m_im in shifts:
                s_re = pltpu.roll(y_re, d, axis=0)
                s_im = pltpu.roll(y_im, d, axis=0)
                y_re, y_im = y_re + (m_re * s_re - m_im * s_im), y_im + (m_re * s_im + m_im * s_re)
            e_re = jnp.where(rowi == 0, 0.0, pltpu.roll(y_re, 1, axis=0))
            e_im = jnp.where(rowi == 0, 0.0, pltpu.roll(y_im, 1, axis=0))
            hp_ref[rows, :half] = e_re + (row_re * c_re - row_im * c_im)
            hp_ref[rows, half:] = e_im + (row_re * c_im + row_im * c_re)
            last = SUBLANES - 1
            return (y_re[last:] + (full_re * c_re - full_im * c_im),
                    y_im[last:] + (full_re * c_im + full_im * c_re))
        c = (h0_ref[b, 0, :, :half], h0_ref[b, 0, :, half:])
        for i in range(n // SUBLANES):
            c = tile(i, c)
        hT_ref[b, 0, :, :half] = c[0]
        hT_ref[b, 0, :, half:] = c[1]
    y = (y_in
         + lax.dot_general(hp_ref[...].astype(BF16), q_ref[0], (((1,), (1,)), ((), ())),
                           preferred_element_type=F32))
    d = d_ref[...]
    for b in range(bt):
        for t in range(T):
            y_ref[b, pl.ds(t, n, stride=T), :] = (
                y[b * n:(b + 1) * n, t * LANES:(t + 1) * LANES]
                + d * u_ref[b, pl.ds(t, n, stride=T), :])


def _s5(u, h0, w, pmat, qmat, a_t, d_skip, bt):
    B, L, _ = u.shape
    n = L // S5_T
    tl, st = S5_T * LANES, 2 * S5_BLK_STATE
    assert n % SUBLANES == 0
    return pl.pallas_call(
        functools.partial(_s5_kernel, bt=bt, n=n),
        grid=(S5_NBLK, B // bt),
        in_specs=[pl.BlockSpec((bt, L, LANES), lambda k, b: (b, 0, k)),
                  pl.BlockSpec((bt, 1, 1, st), lambda k, b: (b, k, 0, 0)),
                  pl.BlockSpec((1, tl, tl), lambda k, b: (k, 0, 0)),
                  pl.BlockSpec((1, tl, st), lambda k, b: (k, 0, 0)),
                  pl.BlockSpec((1, tl, st), lambda k, b: (k, 0, 0)),
                  pl.BlockSpec((1, S5_POW_ROWS, st), lambda k, b: (k, 0, 0)),
                  pl.BlockSpec((1, LANES), lambda k, b: (0, k))],
        out_specs=[pl.BlockSpec((bt, L, LANES), lambda k, b: (b, 0, k)),
                   pl.BlockSpec((bt, 1, 1, st), lambda k, b: (b, k, 0, 0))],
        out_shape=[jax.ShapeDtypeStruct((B, L, S5_WIDTH), F32),
                   jax.ShapeDtypeStruct((B, S5_NBLK, 1, st), F32)],
        scratch_shapes=[pltpu.VMEM((bt * n, tl), BF16),
                        pltpu.VMEM((bt * n, st), F32),
                        pltpu.VMEM((bt * n, st), F32)],
        compiler_params=_cparams("arbitrary", "arbitrary"),
        name="s5",
    )(u, h0, w, pmat, qmat, a_t, d_skip.reshape(1, S5_WIDTH))


def _s5_state_in(h_re, h_im):
    B = h_re.shape[0]
    r = h_re.reshape(B, S5_NBLK, 1, S5_BLK_STATE)
    i = h_im.reshape(B, S5_NBLK, 1, S5_BLK_STATE)
    return jnp.concatenate([r, i], -1)


def _s5_state_out(h):
    B = h.shape[0]
    h = h.reshape(B, S5_NBLK, 2, S5_BLK_GROUPS, S5_STATE)
    return (h[:, :, 0].reshape(B, S5_GROUPS, S5_STATE), h[:, :, 1].reshape(B, S5_GROUPS, S5_STATE))


CONV_TAIL = SUBLANES


def _inproj_kernel(x_ref, sh_ref, sc_ref, w_ref, cb_ref, cw_ref,
                   u_ref, conv_ref, z_ref, ba_ref, cbo_ref, *xbuf_refs):
    bt, lt, d = x_ref.shape
    keep = CONV_WIDTH - 1
    o_q, o_z, o_b = S5_WIDTH, S5_WIDTH + CONV_CH, S5_WIDTH + CONV_CH + GDN_WIDTH
    nparts = CONV_CH // GDN_WIDTH
    parts = [slice(p * GDN_WIDTH, (p + 1) * GDN_WIDTH) for p in range(nparts)]
    nun = len(xbuf_refs) // nparts
    ub = bt // nun
    units = [slice(i * ub, (i + 1) * ub) for i in range(nun)]
    xbufs = [xbuf_refs[i * nparts:(i + 1) * nparts] for i in range(nun)]

    @pl.when(pl.program_id(1) == 0)
    def _():
        for us, bufs in zip(units, xbufs):
            for cs, xbuf_ref in zip(parts, bufs):
                xbuf_ref[:, 0:CONV_TAIL, :] = jnp.zeros((ub, CONV_TAIL, GDN_WIDTH), F32)
                xbuf_ref[:, CONV_TAIL - keep:CONV_TAIL, :] = cb_ref[us, :, cs]

    hs = []
    for us in units:
        h = _ln(x_ref[us]) * (1.0 + sc_ref[us, 0]) + sh_ref[us, 0]
        hs.append(h.reshape(ub * lt, d).astype(BF16))
    for p, cs in enumerate(parts):
        for h, bufs in zip(hs, xbufs):
            bufs[p][:, CONV_TAIL:CONV_TAIL + lt, :] = (
                _bdot(h, w_ref[:, o_q + cs.start:o_q + cs.stop]).reshape(ub, lt, GDN_WIDTH))
    for us, h in zip(units, hs):
        u_ref[us] = _bdot(h, w_ref[:, :o_q]).reshape(ub, lt, S5_WIDTH)
    for us, h in zip(units, hs):
        z_ref[us] = _bdot(h, w_ref[:, o_z:o_b]).reshape(ub, lt, GDN_WIDTH)
        ba_ref[us] = _bdot(h, w_ref[:, o_b:]).reshape(ub, lt, LANES)
    rows = lt + CONV_TAIL
    for us, bufs in zip(units, xbufs):
        for cs, xbuf_ref in zip(parts, bufs):
            xfull = xbuf_ref[...]
            conv = xfull[:, CONV_TAIL:, :] * cw_ref[keep:keep + 1, cs]
            for j in range(keep):
                shifted = pltpu.roll(xfull, rows - (CONV_TAIL - keep + j), axis=1)
                conv = conv + shifted[:, :lt, :] * cw_ref[j:j + 1, cs]
            conv_ref[us, :, cs] = conv
            cbo_ref[us, :, cs] = xbuf_ref[:, lt + CONV_TAIL - keep:lt + CONV_TAIL, :]
            xbuf_ref[:, 0:CONV_TAIL, :] = xbuf_ref[:, lt:lt + CONV_TAIL, :]


def _inproj(x, mod, w_in_p, conv_buf, conv_w, bt, lt):
    B, L, _ = x.shape
    nun = 2 if bt % 2 == 0 else 1
    xs = lambda w: pl.BlockSpec((bt, lt, w), lambda b, i: (b, i, 0))
    ms = lambda idx: pl.BlockSpec((bt, 1, 1, D_MODEL), lambda b, i: (b, idx, 0, 0))
    cache = pl.BlockSpec((bt, CONV_WIDTH - 1, CONV_CH), lambda b, i: (b, 0, 0))
    return pl.pallas_call(
        _inproj_kernel,
        grid=(B // bt, L // lt),
        in_specs=[xs(D_MODEL), ms(0), ms(1),
                  pl.BlockSpec((D_MODEL, IN_PAD), lambda b, i: (0, 0)), cache,
                  pl.BlockSpec((CONV_WIDTH, CONV_CH), lambda b, i: (0, 0))],
        out_specs=[xs(S5_WIDTH), xs(CONV_CH), xs(GDN_WIDTH), xs(LANES), cache],
        out_shape=[jax.ShapeDtypeStruct((B, L, w), F32) for w in (S5_WIDTH, CONV_CH, GDN_WIDTH, LANES)]
                  + [jax.ShapeDtypeStruct((B, CONV_WIDTH - 1, CONV_CH), F32)],
        scratch_shapes=[pltpu.VMEM((bt // nun, lt + CONV_TAIL, GDN_WIDTH), F32)
                        for _ in range(nun * (CONV_CH // GDN_WIDTH))],
        compiler_params=_cparams("arbitrary", "arbitrary"),
        name="inproj",
    )(x, mod, mod, w_in_p, conv_buf, conv_w)


def _mm(a, b):
    return jnp.dot(a.astype(BF16), b.astype(BF16), preferred_element_type=F32)


GDN_INV_BASE = 2
GDN_INV_JOIN = 4


def _bdot(a, b):
    return jnp.dot(a, b, preferred_element_type=F32)


def _unit_lower_inverse(a_list, ri, ci, size):
    eye_f = (ri == ci).astype(F32)
    blk = lambda s: (ri // s) == (ci // s)
    prev = min(GDN_INV_BASE, size)
    base_mask = blk(prev)
    d_list = [(eye_f - jnp.where(base_mask, a, 0.0)).astype(BF16) for a in a_list]
    while prev < size:
        cur = min(prev * GDN_INV_JOIN, size)
        off_mask = blk(cur) & jnp.logical_not(blk(prev))
        ms = [_bdot(d, jnp.where(off_mask, a, 0.0).astype(BF16)) for a, d in zip(a_list, d_list)]
        rs = [eye_f - m for m in ms]
        mps = [m.astype(BF16) for m in ms]
        span = 2
        while span < cur // prev:
            mps = [_bdot(mp, mp).astype(BF16) for mp in mps]
            rs = [r + _bdot(r.astype(BF16), mp) for r, mp in zip(rs, mps)]
            span *= 2
        d_list = [_bdot(r.astype(BF16), d).astype(BF16) for r, d in zip(rs, d_list)]
        prev = cur
    return d_list


def _gdn_kernel(conv_ref, z_ref, ba_ref, s0_ref, alog_ref, dtb_ref, ng_ref,
                y_ref, sT_ref, s_ref, *, chunk, bt):
    C, Dh, H = chunk, GDN_HEAD_DIM, GDN_HEADS

    @pl.when(pl.program_id(1) == 0)
    def _():
        s_ref[...] = s0_ref[...]

    act_all = _silu(conv_ref[...])
    ba = ba_ref[...]
    beta_all = _sigmoid(ba)
    xa = ba + dtb_ref[...]
    softplus = jnp.maximum(xa, 0.0) + jnp.log1p(jnp.exp(-jnp.abs(xa)))
    g = -jnp.exp(alog_ref[...]) * softplus

    ri = lax.broadcasted_iota(jnp.int32, (C, C), 0)
    ci = lax.broadcasted_iota(jnp.int32, (C, C), 1)
    causal = ri >= ci
    strict = ri > ci
    eye = ri == ci
    ng = ng_ref[...]
    tril = causal.astype(BF16)
    decay_all = []
    for b in range(bt):
        g_hi = g[b].astype(BF16)
        g_rest = g[b] - g_hi.astype(F32)
        g_mid = g_rest.astype(BF16)
        g_lo = (g_rest - g_mid.astype(F32)).astype(BF16)
        d3 = _bdot(tril, jnp.concatenate([g_hi, g_mid, g_lo], axis=1))
        decay_all.append(d3[:, :LANES] + (d3[:, LANES:2 * LANES] + d3[:, 2 * LANES:]))
    units = [(b, h) for b in range(bt) for h in range(H)]

    e_list, kbq_list, kT_list, rhs_list, qd_list, kdT_list, gl_list = [], [], [], [], [], [], []
    for b, h in units:
        decay, beta, act = decay_all[b], beta_all[b], act_all[b]
        dcol = decay[:, H + h:H + h + 1]
        dmat = jnp.broadcast_to(dcol, (C, C))
        drow = jnp.sum(jnp.where(eye, dmat, 0.0), axis=0, keepdims=True)
        e = jnp.exp(dmat - drow)
        bcol = beta[:, h:h + 1]
        q = act[:, h * Dh:(h + 1) * Dh]
        k = act[:, GDN_WIDTH + h * Dh:GDN_WIDTH + (h + 1) * Dh]
        v = act[:, 2 * GDN_WIDTH + h * Dh:2 * GDN_WIDTH + (h + 1) * Dh]
        q = q * (lax.rsqrt(jnp.sum(q * q, -1, keepdims=True) + RMS_EPS) * (Dh ** -0.5))
        k = k * lax.rsqrt(jnp.sum(k * k, -1, keepdims=True) + RMS_EPS)
        kb = k * bcol
        kT = k.T
        kT_b = kT.astype(BF16)
        edec = jnp.exp(dcol)
        dlast = decay[C - 1:C, H + h:H + h + 1]
        e_list.append(e)
        kbq_list.append(jnp.concatenate([kb, q], axis=0).astype(BF16))
        kT_list.append(kT_b)
        rhs_list.append(jnp.concatenate([v * bcol, kb * edec], axis=1).astype(BF16))
        qd_list.append((q * edec).astype(BF16))
        kdT_list.append((kT * jnp.exp(dlast - drow)).astype(BF16))
        gl_list.append(jnp.exp(dlast))
    sc_list = [_bdot(kbq, kT_b) for kbq, kT_b in zip(kbq_list, kT_list)]
    a_list = [jnp.where(strict, sc[:C] * e, 0.0) for sc, e in zip(sc_list, e_list)]
    qk_list = [jnp.where(causal, sc[C:] * e, 0.0).astype(BF16) for sc, e in zip(sc_list, e_list)]

    tinv_list = _unit_lower_inverse(a_list, ri, ci, C)

    uw_list = [_bdot(t, rhs) for t, rhs in zip(tinv_list, rhs_list)]
    s_list = [s_ref[b, h] for b, h in units]
    ws_list = [_bdot(jnp.concatenate([uw[:, Dh:].astype(BF16), qd], axis=0), s.astype(BF16))
               for uw, qd, s in zip(uw_list, qd_list, s_list)]
    vn_list = [(uw[:, :Dh] - ws[:C]).astype(BF16) for uw, ws in zip(uw_list, ws_list)]
    ov_list = [_bdot(jnp.concatenate([qk, kdT], axis=0), vn)
               for qk, kdT, vn in zip(qk_list, kdT_list, vn_list)]
    for i, (b, h) in enumerate(units):
        s_ref[b, h] = s_list[i] * gl_list[i] + ov_list[i][C:]
    for i, (b, h) in enumerate(units):
        o = ws_list[i][C:] + ov_list[i][:C]
        o = o * lax.rsqrt(jnp.mean(o * o, -1, keepdims=True) + RMS_EPS) * ng
        y_ref[b, :, h * Dh:(h + 1) * Dh] = o * _silu(z_ref[b, :, h * Dh:(h + 1) * Dh])

    sT_ref[...] = s_ref[...]


def _gdn(conv, z, ba, s0, a_log, dt_bias, norm_g, chunk, bt):
    B, L, _ = conv.shape
    H, Dh = GDN_HEADS, GDN_HEAD_DIM
    lt = chunk
    place = lambda t: jnp.zeros((1, LANES), F32).at[0, H:2 * H].set(t)
    xs = lambda w: pl.BlockSpec((bt, lt, w), lambda b, i: (b, i, 0))
    full = lambda *shape: pl.BlockSpec(shape, lambda b, i: (0,) * len(shape))
    state = pl.BlockSpec((bt, H, Dh, Dh), lambda b, i: (b, 0, 0, 0))
    return pl.pallas_call(
        functools.partial(_gdn_kernel, chunk=chunk, bt=bt),
        grid=(B // bt, L // lt),
        in_specs=[xs(CONV_CH), xs(GDN_WIDTH), xs(LANES), state,
                  full(1, LANES), full(1, LANES), full(1, Dh)],
        out_specs=[xs(GDN_WIDTH), state],
        out_shape=[jax.ShapeDtypeStruct((B, L, GDN_WIDTH), F32),
                   jax.ShapeDtypeStruct((B, H, Dh, Dh), F32)],
        scratch_shapes=[pltpu.VMEM((bt, H, Dh, Dh), F32)],
        compiler_params=_cparams("arbitrary", "arbitrary"),
        name="gdn",
    )(conv, z, ba, s0, place(a_log), place(dt_bias), norm_g.reshape(1, Dh))


FF_BLK = 256


def _gelu_tanh(x):
    return 0.5 * x * (1.0 + jnp.tanh(0.7978845608028654 * (x + 0.044715 * x * x * x)))


def _outffn_kernel(x_ref, ya_ref, yb_ref, g1_ref, sh2_ref, sc2_ref, g2_ref,
                   wglu_ref, wout_ref, l1g_ref, l1b_ref, wup_ref, wdn_ref, l2g_ref, l2b_ref,
                   o_ref):
    bt, lt, d = x_ref.shape
    if bt % 2 == 0:
        subs = [(slice(i * bt // 2, (i + 1) * bt // 2), slice(0, lt)) for i in range(2)]
    else:
        subs = [(slice(0, bt), slice(i * lt // 2, (i + 1) * lt // 2)) for i in range(2)]
    rd = lambda ref, sub: ref[sub[0], sub[1], :]
    flat = lambda t: t.reshape(t.shape[0] * t.shape[1], t.shape[-1])
    unflat = lambda t, like: t.reshape(like.shape[0], like.shape[1], t.shape[-1])
    mod = lambda ref, sub: ref[sub[0], 0]
    nsub = range(len(subs))

    zz = [_mm(_gelu_tanh(flat(rd(ya_ref, s))), wglu_ref[...]) for s in subs]
    y_a = [z[:, :S5_WIDTH] * _sigmoid(z[:, S5_WIDTH:]) for z in zz]
    mix_b = [_mm(flat(rd(yb_ref, s)), wout_ref[S5_WIDTH:, :]) for s in subs]
    mix = [mb + _mm(ya, wout_ref[:S5_WIDTH, :]) for mb, ya in zip(mix_b, y_a)]
    xs = [rd(x_ref, s) for s in subs]
    x1 = [_ln(ALPHA * x + (1.0 + mod(g1_ref, s)) * unflat(mx, x)) * l1g_ref[...] + l1b_ref[...]
          for x, mx, s in zip(xs, mix, subs)]
    h = [flat(_ln(x) * (1.0 + mod(sc2_ref, s)) + mod(sh2_ref, s)).astype(BF16) for x, s in zip(x1, subs)]

    def gate_up(i, j):
        gate = _bdot(h[i], wup_ref[:, j * FF_BLK:(j + 1) * FF_BLK])
        up = _bdot(h[i], wup_ref[:, D_FF + j * FF_BLK:D_FF + (j + 1) * FF_BLK])
        return gate, up

    nblk = D_FF // FF_BLK
    acc = [None for _ in nsub]
    gu = [gate_up(i, 0) for i in nsub]
    for j in range(nblk):
        gu_next = [gate_up(i, j + 1) for i in nsub] if j + 1 < nblk else None
        for i in nsub:
            gate, up = gu[i]
            part = _mm(_silu(gate) * up, wdn_ref[j * FF_BLK:(j + 1) * FF_BLK, :])
            acc[i] = part if acc[i] is None else acc[i] + part
        gu = gu_next
    for i, s in enumerate(subs):
        o_ref[s[0], s[1], :] = (_ln(ALPHA * x1[i] + (1.0 + mod(g2_ref, s)) * unflat(acc[i], x1[i]))
                                * l2g_ref[...] + l2b_ref[...])


def _outffn(x, ya, yb, mod, w_glu, w_out, ln1_g, ln1_b, w_up, w_dn, ln2_g, ln2_b, bt, lt):
    B, L, _ = x.shape
    xs = lambda w: pl.BlockSpec((bt, lt, w), lambda b, i: (b, i, 0))
    ms = lambda idx: pl.BlockSpec((bt, 1, 1, D_MODEL), lambda b, i: (b, idx, 0, 0))
    const = lambda a: pl.BlockSpec(a.shape, lambda b, i: (0,) * a.ndim, pipeline_mode=pl.Buffered(1))
    vec = lambda t: t.reshape(1, D_MODEL)
    weights = (w_glu, w_out, vec(ln1_g), vec(ln1_b), w_up, w_dn, vec(ln2_g), vec(ln2_b))
    return pl.pallas_call(
        _outffn_kernel,
        grid=(B // bt, L // lt),
        in_specs=[xs(D_MODEL), xs(S5_WIDTH), xs(GDN_WIDTH), ms(2), ms(3), ms(4), ms(5)]
                 + [const(a) for a in weights],
        out_specs=xs(D_MODEL),
        out_shape=jax.ShapeDtypeStruct((B, L, D_MODEL), F32),
        compiler_params=_cparams("arbitrary", "arbitrary"),
        name="outffn",
    )(x, ya, yb, mod, mod, mod, mod, *weights)


TOKEN_TILE = 2 * MXU_TILE
GDN_CHUNK = LANES
GDN_CHAINS_SEQ = 4


def _plan(B, L):
    if L >= TOKEN_TILE:
        bt, lt = 1, TOKEN_TILE
    else:
        bt, lt = min(B, TOKEN_TILE // L), L
    assert B % bt == 0 and L % lt == 0
    gdn_chunk = min(L, GDN_CHUNK)
    gdn_bt = min(B, GDN_CHAINS_SEQ)
    assert L % gdn_chunk == 0 and B % gdn_bt == 0
    return bt, lt, gdn_chunk, gdn_bt


def _layer(x, mod, h_re, h_im, s_gdn, conv_buf, wts):
    (w_in_p, s5_mats, s5_d, w_glu, conv_w, a_log, dt_bias, norm_g, w_out,
     ln1_g, ln1_b, w_up, w_dn, ln2_g, ln2_b) = wts
    bt, lt, gdn_chunk, gdn_bt = _plan(x.shape[0], x.shape[1])
    in_lt = min(2 * lt, x.shape[1]) if bt == 1 else lt
    u, conv, z, ba, cbo = _inproj(x, mod, w_in_p, conv_buf, conv_w, bt, in_lt)
    w, pmat, qmat, a_t = s5_mats
    ys5, hT = _s5(u, _s5_state_in(h_re, h_im), w, pmat, qmat, a_t, s5_d, bt)
    yb, sT = _gdn(conv, z, ba, s_gdn, a_log, dt_bias, norm_g, gdn_chunk, gdn_bt)
    y = _outffn(x, ys5, yb, mod, w_glu, w_out, ln1_g, ln1_b, w_up, w_dn, ln2_g, ln2_b, bt, lt)
    o_re, o_im = _s5_state_out(hT)
    return y, o_re, o_im, sT, cbo


def kernel(x_prompt, x_sample, state_s5_re, state_s5_im, state_gdn, cache_gdn_conv, c_prompt, c_sample, w_ada, b_ada, w_in, s5_a_re, s5_a_im, s5_log_dt, s5_b_re, s5_b_im, s5_c_re, s5_c_im, s5_d, w_s5_glu, gdn_conv_w, gdn_a_log, gdn_dt_bias, gdn_norm_g, w_out, ln1_g, ln1_b, w_ffn_up, w_ffn_down, ln2_g, ln2_b):
    bp, bs = x_prompt.shape[0], x_sample.shape[0]
    assert w_ada.shape[0] == DEPTH == 1
    l = 0
    c_all = jnp.concatenate([c_prompt, c_sample], 0)
    c_all = jnp.pad(c_all, ((0, -(bp + bs) % (2 * SUBLANES)), (0, 0)))
    mod = _ada(c_all, w_ada[l], b_ada[l])[:bp + bs]
    mod = mod.reshape(bp + bs, 6, 1, D_MODEL)
    w_in_p = jnp.pad(w_in[l], ((0, 0), (0, IN_PAD - w_in.shape[-1]))).astype(BF16)
    s5_mats = _s5_prep(s5_a_re[l], s5_a_im[l], s5_log_dt[l], s5_b_re[l], s5_b_im[l],
                       s5_c_re[l], s5_c_im[l])
    wts = (w_in_p, s5_mats, s5_d[l], w_s5_glu[l].astype(BF16), gdn_conv_w[l], gdn_a_log[l],
           gdn_dt_bias[l], gdn_norm_g[l], w_out[l].astype(BF16), ln1_g[l], ln1_b[l],
           w_ffn_up[l].astype(BF16), w_ffn_down[l].astype(BF16), ln2_g[l], ln2_b[l])
    zeros = lambda *s: jnp.zeros(s, F32)
    yp, p_re, p_im, p_gdn, p_conv = _layer(
        x_prompt, mod[:bp],
        zeros(bp, S5_GROUPS, S5_STATE), zeros(bp, S5_GROUPS, S5_STATE),
        zeros(bp, GDN_HEADS, GDN_HEAD_DIM, GDN_HEAD_DIM), zeros(bp, CONV_WIDTH - 1, CONV_CH),
        wts)
    ys, s_re, s_im, s_gdn, s_conv = _layer(
        x_sample, mod[bp:], state_s5_re[l], state_s5_im[l], state_gdn[l], cache_gdn_conv[l],
        wts)
    st = lambda t: t[None]
    return (yp, ys, st(p_re), st(p_im), st(p_gdn), st(p_conv),
            st(s_re), st(s_im), st(s_gdn), st(s_conv))
```

```python
import functools

import jax
import jax.numpy as jnp
from jax import lax
from jax.experimental import pallas as pl
from jax.experimental.pallas import tpu as pltpu

F32 = jnp.float32
BF16 = jnp.bfloat16

D_MODEL = 1024
S5_WIDTH = 512
S5_GROUP = 16
S5_GROUPS = 32
S5_STATE = 64
GDN_WIDTH = 512
GDN_HEAD_DIM = 128
GDN_HEADS = 4
CONV_WIDTH = 4
CONV_CH = 3 * GDN_WIDTH
D_FF = 2816
LN_EPS = 1e-5
RMS_EPS = 1e-6
DEPTH = 1
ALPHA = (2 * DEPTH) ** 0.25

LANES = 128
MXU_TILE = 256
S5_T = 8
S5_BLK_GROUPS = LANES // S5_GROUP
S5_NBLK = S5_GROUPS // S5_BLK_GROUPS
S5_BLK_STATE = S5_BLK_GROUPS * S5_STATE
SUBLANES = 8
S5_POW_ROWS = 2 * SUBLANES
IN_PAD = S5_WIDTH + 4 * GDN_WIDTH + LANES
VMEM_LIMIT = 56 * 1024 * 1024


def _cparams(*sem):
    return pltpu.CompilerParams(dimension_semantics=sem, vmem_limit_bytes=VMEM_LIMIT)


STREAM_BUFFERS = 3


def _stream_in(block, nsteps):
    index = lambda b, i: (b, i, 0)
    if nsteps > STREAM_BUFFERS:
        return pl.BlockSpec(block, index, pipeline_mode=pl.Buffered(STREAM_BUFFERS))
    return pl.BlockSpec(block, index)


def _ln(x):
    mu = jnp.mean(x, -1, keepdims=True)
    xc = x - mu
    var = jnp.mean(xc * xc, -1, keepdims=True)
    return xc * lax.rsqrt(var + LN_EPS)


def _sigmoid(x):
    return 1.0 / (1.0 + jnp.exp(-x))


def _silu(x):
    return x * _sigmoid(x)


def _split_bf16(x):
    hi = x.astype(BF16)
    return hi, (x - hi.astype(F32)).astype(BF16)


def _ada_kernel(c_ref, w_ref, b_ref, o_ref):
    c_hi, c_lo = _split_bf16(_silu(c_ref[...]))
    w_hi, w_lo = _split_bf16(w_ref[...])
    dot = functools.partial(jnp.dot, preferred_element_type=F32)
    o_ref[...] = dot(c_hi, w_hi) + (dot(c_hi, w_lo) + dot(c_lo, w_hi)) + b_ref[...]


def _ada(c, w_ada, b_ada):
    rows = c.shape[0]
    ncol = w_ada.shape[1]
    blk = D_MODEL
    return pl.pallas_call(
        _ada_kernel,
        grid=(ncol // blk,),
        in_specs=[pl.BlockSpec((rows, D_MODEL), lambda j: (0, 0)),
                  pl.BlockSpec((D_MODEL, blk), lambda j: (0, j)),
                  pl.BlockSpec((1, blk), lambda j: (0, j))],
        out_specs=pl.BlockSpec((rows, blk), lambda j: (0, j)),
        out_shape=jax.ShapeDtypeStruct((rows, ncol), F32),
        compiler_params=_cparams("arbitrary"),
        name="ada",
    )(c, w_ada, b_ada.reshape(1, ncol))


def _s5prep_kernel(are_ref, aim_ref, ldt_ref, btre_ref, btim_ref, ctre_ref, ctim_ref,
                   w_ref, p_ref, qt_ref, at_ref):
    T, half = S5_T, S5_BLK_STATE
    a_re = are_ref[0]
    a_im = aim_ref[0]
    dt = jnp.exp(ldt_ref[0])
    den = a_re * a_re + a_im * a_im

    def lam_pow(tau):
        mag = jnp.exp(a_re * dt * tau)
        return mag * jnp.cos(a_im * dt * tau), mag * jnp.sin(a_im * dt * tau)

    l_re, l_im = lam_pow(1.0)
    n_re, n_im = l_re - 1.0, l_im
    f_re = (n_re * a_re + n_im * a_im) / den
    f_im = (n_im * a_re - n_re * a_im) / den
    same_group = (lax.broadcasted_iota(jnp.int32, (LANES, half), 0) // S5_GROUP
                  == lax.broadcasted_iota(jnp.int32, (LANES, half), 1) // S5_STATE)
    bm_re = jnp.where(same_group, btre_ref[0], 0.0)
    bm_im = jnp.where(same_group, btim_ref[0], 0.0)
    cm_re = jnp.where(same_group, ctre_ref[0], 0.0)
    cm_im = jnp.where(same_group, ctim_ref[0], 0.0)
    nt = lambda a, b: lax.dot_general(a, b, (((1,), (1,)), ((), ())), preferred_element_type=F32)
    cm_re_s, cm_im_s = _split_bf16(cm_re), _split_bf16(cm_im)

    def dot_nt(a, b_split):
        a_hi, a_lo = _split_bf16(a)
        return nt(a_hi, b_split[0]) + (nt(a_hi, b_split[1]) + nt(a_lo, b_split[0]))
    blk = lambda i: slice(i * LANES, (i + 1) * LANES)
    for s in range(T):
        for t in range(s):
            w_ref[0, blk(s), blk(t)] = jnp.zeros((LANES, LANES), BF16)
    m_t = lax.broadcasted_iota(jnp.int32, (S5_POW_ROWS, half), 0).astype(F32) * float(T)
    mag = jnp.exp(a_re * dt * m_t)
    at_ref[0, :, :half] = mag * jnp.cos(a_im * dt * m_t)
    at_ref[0, :, half:] = mag * jnp.sin(a_im * dt * m_t)
    for tau in range(T + 1):
        p_re, p_im = (jnp.ones_like(a_re), jnp.zeros_like(a_re)) if tau == 0 else lam_pow(float(tau))
        if tau < T:
            g_re = p_re * f_re - p_im * f_im
            g_im = p_re * f_im + p_im * f_re
            inj_re = g_re * bm_re - g_im * bm_im
            inj_im = g_re * bm_im + g_im * bm_re
            s = T - 1 - tau
            p_ref[0, blk(s), :half] = inj_re.astype(BF16)
            p_ref[0, blk(s), half:] = inj_im.astype(BF16)
            k_tau = (dot_nt(inj_re, cm_re_s) - dot_nt(inj_im, cm_im_s)).astype(BF16)
            for s in range(T - tau):
                w_ref[0, blk(s), blk(s + tau)] = k_tau
        if tau >= 1:
            qt_ref[0, blk(tau - 1), :half] = (cm_re * p_re - cm_im * p_im).astype(BF16)
            qt_ref[0, blk(tau - 1), half:] = (-(cm_re * p_im + cm_im * p_re)).astype(BF16)


def _s5_prep(a_re, a_im, log_dt, b_re, b_im, c_re, c_im):
    NB, T, half = S5_NBLK, S5_T, S5_BLK_STATE
    tl, st = T * LANES, 2 * half
    lanes = lambda t: t.reshape(NB, 1, half)
    tiled = lambda t: jnp.tile(t.reshape(NB, LANES, S5_STATE), (1, 1, S5_BLK_GROUPS))
    row = pl.BlockSpec((1, 1, half), lambda k: (k, 0, 0))
    mat = pl.BlockSpec((1, LANES, half), lambda k: (k, 0, 0))
    return pl.pallas_call(
        _s5prep_kernel,
        grid=(NB,),
        in_specs=[row, row, row, mat, mat, mat, mat],
        out_specs=[pl.BlockSpec((1, tl, tl), lambda k: (k, 0, 0)),
                   pl.BlockSpec((1, tl, st), lambda k: (k, 0, 0)),
                   pl.BlockSpec((1, tl, st), lambda k: (k, 0, 0)),
                   pl.BlockSpec((1, S5_POW_ROWS, st), lambda k: (k, 0, 0))],
        out_shape=[jax.ShapeDtypeStruct((NB, tl, tl), BF16),
                   jax.ShapeDtypeStruct((NB, tl, st), BF16),
                   jax.ShapeDtypeStruct((NB, tl, st), BF16),
                   jax.ShapeDtypeStruct((NB, S5_POW_ROWS, st), F32)],
        compiler_params=_cparams("arbitrary"),
        name="s5prep",
    )(lanes(a_re), lanes(a_im), lanes(jnp.repeat(log_dt, S5_STATE)),
      tiled(jnp.swapaxes(b_re, 1, 2)), tiled(jnp.swapaxes(b_im, 1, 2)), tiled(c_re), tiled(c_im))


def _s5_kernel(u_ref, h0_ref, w_ref, p_ref, q_ref, at_ref, d_ref, y_ref, hT_ref,
               lhs_ref, x_ref, hp_ref, *, bt, n):
    T, half = S5_T, S5_BLK_STATE
    for b in range(bt):
        for t in range(T):
            lhs_ref[b * n:(b + 1) * n, t * LANES:(t + 1) * LANES] = (
                u_ref[b, pl.ds(t, n, stride=T), :].astype(BF16))
    lhs = lhs_ref[...]
    x_ref[...] = jnp.dot(lhs, p_ref[0], preferred_element_type=F32)
    cplx = lambda m: (at_ref[0, m:m + 1, :half], at_ref[0, m:m + 1, half:])
    rowi = lax.broadcasted_iota(jnp.int32, (SUBLANES, half), 0)
    shifts = []
    d = 1
    while d < SUBLANES:
        p_re, p_im = cplx(d)
        shifts.append((d, jnp.where(rowi >= d, p_re, 0.0), jnp.where(rowi >= d, p_im, 0.0)))
        d *= 2
    row_re = at_ref[0, 0:SUBLANES, :half]
    row_im = at_ref[0, 0:SUBLANES, half:]
    full_re, full_im = cplx(SUBLANES)
    y_in = jnp.concatenate(
        [jnp.dot(lhs[:, :c + MXU_TILE], w_ref[0, :c + MXU_TILE, c:c + MXU_TILE],
                 preferred_element_type=F32) for c in range(0, T * LANES, MXU_TILE)], axis=1)
    for b in range(bt):
        def tile(i, c):
            c_re, c_im = c
            rows = slice(b * n + i * SUBLANES, b * n + (i + 1) * SUBLANES)
            y_re = x_ref[rows, :half]
            y_im = x_ref[rows, half:]
            for d, m_re, m_im in shifts:
                s_re = pltpu.roll(y_re, d, axis=0)
                s_im = pltpu.roll(y_im, d, axis=0)
                y_re, y_im = y_re + (m_re * s_re - m_im * s_im), y_im + (m_re * s_im + m_im * s_re)
            e_re = jnp.where(rowi == 0, 0.0, pltpu.roll(y_re, 1, axis=0))
            e_im = jnp.where(rowi == 0, 0.0, pltpu.roll(y_im, 1, axis=0))
            hp_ref[rows, :half] = e_re + (row_re * c_re - row_im * c_im)
            hp_ref[rows, half:] = e_im + (row_re * c_im + row_im * c_re)
            last = SUBLANES - 1
            return (y_re[last:] + (full_re * c_re - full_im * c_im),
                    y_im[last:] + (full_re * c_im + full_im * c_re))
        c = (h0_ref[b, 0, :, :half], h0_ref[b, 0, :, half:])
        for i in range(n // SUBLANES):
            c = tile(i, c)
        hT_ref[b, 0, :, :half] = c[0]
        hT_ref[b, 0, :, half:] = c[1]
    y = (y_in
         + lax.dot_general(hp_ref[...].astype(BF16), q_ref[0], (((1,), (1,)), ((), ())),
                           preferred_element_type=F32))
    d = d_ref[...]
    for b in range(bt):
        for t in range(T):
            y_ref[b, pl.ds(t, n, stride=T), :] = (
                y[b * n:(b + 1) * n, t * LANES:(t + 1) * LANES]
                + d * u_ref[b, pl.ds(t, n, stride=T), :])


def _s5(u, h0, w, pmat, qmat, a_t, d_skip, bt):
    B, L, _ = u.shape
    n = L // S5_T
    tl, st = S5_T * LANES, 2 * S5_BLK_STATE
    assert n % SUBLANES == 0
    return pl.pallas_call(
        functools.partial(_s5_kernel, bt=bt, n=n),
        grid=(S5_NBLK, B // bt),
        in_specs=[pl.BlockSpec((bt, L, LANES), lambda k, b: (b, 0, k)),
                  pl.BlockSpec((bt, 1, 1, st), lambda k, b: (b, k, 0, 0)),
                  pl.BlockSpec((1, tl, tl), lambda k, b: (k, 0, 0)),
                  pl.BlockSpec((1, tl, st), lambda k, b: (k, 0, 0)),
                  pl.BlockSpec((1, tl, st), lambda k, b: (k, 0, 0)),
                  pl.BlockSpec((1, S5_POW_ROWS, st), lambda k, b: (k, 0, 0)),
                  pl.BlockSpec((1, LANES), lambda k, b: (0, k))],
        out_specs=[pl.BlockSpec((bt, L, LANES), lambda k, b: (b, 0, k)),
                   pl.BlockSpec((bt, 1, 1, st), lambda k, b: (b, k, 0, 0))],
        out_shape=[jax.ShapeDtypeStruct((B, L, S5_WIDTH), F32),
                   jax.ShapeDtypeStruct((B, S5_NBLK, 1, st), F32)],
        scratch_shapes=[pltpu.VMEM((bt * n, tl), BF16),
                        pltpu.VMEM((bt * n, st), F32),
                        pltpu.VMEM((bt * n, st), F32)],
        compiler_params=_cparams("arbitrary", "arbitrary"),
        name="s5",
    )(u, h0, w, pmat, qmat, a_t, d_skip.reshape(1, S5_WIDTH))


def _s5_state_in(h_re, h_im):
    B = h_re.shape[0]
    r = h_re.reshape(B, S5_NBLK, 1, S5_BLK_STATE)
    i = h_im.reshape(B, S5_NBLK, 1, S5_BLK_STATE)
    return jnp.concatenate([r, i], -1)


def _s5_state_out(h):
    B = h.shape[0]
    h = h.reshape(B, S5_NBLK, 2, S5_BLK_GROUPS, S5_STATE)
    return (h[:, :, 0].reshape(B, S5_GROUPS, S5_STATE), h[:, :, 1].reshape(B, S5_GROUPS, S5_STATE))


CONV_TAIL = SUBLANES


def _inproj_kernel(x_ref, sh_ref, sc_ref, w_ref, cb_ref, cw_ref,
                   u_ref, conv_ref, z_ref, ba_ref, cbo_ref, *xbuf_refs):
    bt, lt, d = x_ref.shape
    keep = CONV_WIDTH - 1
    o_q, o_z, o_b = S5_WIDTH, S5_WIDTH + CONV_CH, S5_WIDTH + CONV_CH + GDN_WIDTH
    nparts = CONV_CH // GDN_WIDTH
    parts = [slice(p * GDN_WIDTH, (p + 1) * GDN_WIDTH) for p in range(nparts)]
    nun = len(xbuf_refs) // nparts
    ub = bt // nun
    units = [slice(i * ub, (i + 1) * ub) for i in range(nun)]
    xbufs = [xbuf_refs[i * nparts:(i + 1) * nparts] for i in range(nun)]

    @pl.when(pl.program_id(1) == 0)
    def _():
        for us, bufs in zip(units, xbufs):
            for cs, xbuf_ref in zip(parts, bufs):
                xbuf_ref[:, 0:CONV_TAIL, :] = jnp.zeros((ub, CONV_TAIL, GDN_WIDTH), F32)
                xbuf_ref[:, CONV_TAIL - keep:CONV_TAIL, :] = cb_ref[us, :, cs]

    hs = []
    for us in units:
        h = _ln(x_ref[us]) * (1.0 + sc_ref[us, 0]) + sh_ref[us, 0]
        hs.append(h.reshape(ub * lt, d).astype(BF16))
    for p, cs in enumerate(parts):
        for h, bufs in zip(hs, xbufs):
            bufs[p][:, CONV_TAIL:CONV_TAIL + lt, :] = (
                _bdot(h, w_ref[:, o_q + cs.start:o_q + cs.stop]).reshape(ub, lt, GDN_WIDTH))
    for us, h in zip(units, hs):
        u_ref[us] = _bdot(h, w_ref[:, :o_q]).reshape(ub, lt, S5_WIDTH)
    for us, h in zip(units, hs):
        z_ref[us] = _bdot(h, w_ref[:, o_z:o_b]).reshape(ub, lt, GDN_WIDTH)
        ba_ref[us] = _bdot(h, w_ref[:, o_b:]).reshape(ub, lt, LANES)
    rows = lt + CONV_TAIL
    for us, bufs in zip(units, xbufs):
        for cs, xbuf_ref in zip(parts, bufs):
            xfull = xbuf_ref[...]
            conv = xfull[:, CONV_TAIL:, :] * cw_ref[keep:keep + 1, cs]
            for j in range(keep):
                shifted = pltpu.roll(xfull, rows - (CONV_TAIL - keep + j), axis=1)
                conv = conv + shifted[:, :lt, :] * cw_ref[j:j + 1, cs]
            conv_ref[us, :, cs] = conv
            cbo_ref[us, :, cs] = xbuf_ref[:, lt + CONV_TAIL - keep:lt + CONV_TAIL, :]
            xbuf_ref[:, 0:CONV_TAIL, :] = xbuf_ref[:, lt:lt + CONV_TAIL, :]


def _inproj(x, mod, w_in_p, conv_buf, conv_w, bt, lt):
    B, L, _ = x.shape
    nun = 2 if bt % 2 == 0 else 1
    xs = lambda w: pl.BlockSpec((bt, lt, w), lambda b, i: (b, i, 0))
    ms = lambda idx: pl.BlockSpec((bt, 1, 1, D_MODEL), lambda b, i: (b, idx, 0, 0))
    cache = pl.BlockSpec((bt, CONV_WIDTH - 1, CONV_CH), lambda b, i: (b, 0, 0))
    return pl.pallas_call(
        _inproj_kernel,
        grid=(B // bt, L // lt),
        in_specs=[xs(D_MODEL), ms(0), ms(1),
                  pl.BlockSpec((D_MODEL, IN_PAD), lambda b, i: (0, 0)), cache,
                  pl.BlockSpec((CONV_WIDTH, CONV_CH), lambda b, i: (0, 0))],
        out_specs=[xs(S5_WIDTH), xs(CONV_CH), xs(GDN_WIDTH), xs(LANES), cache],
        out_shape=[jax.ShapeDtypeStruct((B, L, w), F32) for w in (S5_WIDTH, CONV_CH, GDN_WIDTH, LANES)]
                  + [jax.ShapeDtypeStruct((B, CONV_WIDTH - 1, CONV_CH), F32)],
        scratch_shapes=[pltpu.VMEM((bt // nun, lt + CONV_TAIL, GDN_WIDTH), F32)
                        for _ in range(nun * (CONV_CH // GDN_WIDTH))],
        compiler_params=_cparams("arbitrary", "arbitrary"),
        name="inproj",
    )(x, mod, mod, w_in_p, conv_buf, conv_w)


def _mm(a, b):
    return jnp.dot(a.astype(BF16), b.astype(BF16), preferred_element_type=F32)


GDN_INV_BASE = 2
GDN_INV_JOIN = 4


def _bdot(a, b):
    return jnp.dot(a, b, preferred_element_type=F32)


def _unit_lower_inverse(a_list, ri, ci, size):
    eye_f = (ri == ci).astype(F32)
    blk = lambda s: (ri // s) == (ci // s)
    prev = min(GDN_INV_BASE, size)
    base_mask = blk(prev)
    d_list = [(eye_f - jnp.where(base_mask, a, 0.0)).astype(BF16) for a in a_list]
    while prev < size:
        cur = min(prev * GDN_INV_JOIN, size)
        off_mask = blk(cur) & jnp.logical_not(blk(prev))
        ms = [_bdot(d, jnp.where(off_mask, a, 0.0).astype(BF16)) for a, d in zip(a_list, d_list)]
        rs = [eye_f - m for m in ms]
        mps = [m.astype(BF16) for m in ms]
        span = 2
        while span < cur // prev:
            mps = [_bdot(mp, mp).astype(BF16) for mp in mps]
            rs = [r + _bdot(r.astype(BF16), mp) for r, mp in zip(rs, mps)]
            span *= 2
        d_list = [_bdot(r.astype(BF16), d).astype(BF16) for r, d in zip(rs, d_list)]
        prev = cur
    return d_list


def _gdn_kernel(conv_ref, z_ref, ba_ref, s0_ref, alog_ref, dtb_ref, ng_ref,
                y_ref, sT_ref, s_ref, *, chunk, bt):
    C, Dh, H = chunk, GDN_HEAD_DIM, GDN_HEADS

    @pl.when(pl.program_id(1) == 0)
    def _():
        s_ref[...] = s0_ref[...]

    act_all = _silu(conv_ref[...])
    ba = ba_ref[...]
    beta_all = _sigmoid(ba)
    xa = ba + dtb_ref[...]
    softplus = jnp.maximum(xa, 0.0) + jnp.log1p(jnp.exp(-jnp.abs(xa)))
    g = -jnp.exp(alog_ref[...]) * softplus

    ri = lax.broadcasted_iota(jnp.int32, (C, C), 0)
    ci = lax.broadcasted_iota(jnp.int32, (C, C), 1)
    causal = ri >= ci
    strict = ri > ci
    eye = ri == ci
    ng = ng_ref[...]
    tril = causal.astype(BF16)
    decay_all = []
    for b in range(bt):
        g_hi = g[b].astype(BF16)
        g_rest = g[b] - g_hi.astype(F32)
        g_mid = g_rest.astype(BF16)
        g_lo = (g_rest - g_mid.astype(F32)).astype(BF16)
        d3 = _bdot(tril, jnp.concatenate([g_hi, g_mid, g_lo], axis=1))
        decay_all.append(d3[:, :LANES] + (d3[:, LANES:2 * LANES] + d3[:, 2 * LANES:]))
    units = [(b, h) for b in range(bt) for h in range(H)]

    e_list, kbq_list, kT_list, rhs_list, qd_list, kdT_list, gl_list = [], [], [], [], [], [], []
    for b, h in units:
        decay, beta, act = decay_all[b], beta_all[b], act_all[b]
        dcol = decay[:, H + h:H + h + 1]
        dmat = jnp.broadcast_to(dcol, (C, C))
        drow = jnp.sum(jnp.where(eye, dmat, 0.0), axis=0, keepdims=True)
        e = jnp.exp(dmat - drow)
        bcol = beta[:, h:h + 1]
        q = act[:, h * Dh:(h + 1) * Dh]
        k = act[:, GDN_WIDTH + h * Dh:GDN_WIDTH + (h + 1) * Dh]
        v = act[:, 2 * GDN_WIDTH + h * Dh:2 * GDN_WIDTH + (h + 1) * Dh]
        q = q * (lax.rsqrt(jnp.sum(q * q, -1, keepdims=True) + RMS_EPS) * (Dh ** -0.5))
        k = k * lax.rsqrt(jnp.sum(k * k, -1, keepdims=True) + RMS_EPS)
        kb = k * bcol
        kT = k.T
        kT_b = kT.astype(BF16)
        edec = jnp.exp(dcol)
        dlast = decay[C - 1:C, H + h:H + h + 1]
        e_list.append(e)
        kbq_list.append(jnp.concatenate([kb, q], axis=0).astype(BF16))
        kT_list.append(kT_b)
        rhs_list.append(jnp.concatenate([v * bcol, kb * edec], axis=1).astype(BF16))
        qd_list.append((q * edec).astype(BF16))
        kdT_list.append((kT * jnp.exp(dlast - drow)).astype(BF16))
        gl_list.append(jnp.exp(dlast))
    sc_list = [_bdot(kbq, kT_b) for kbq, kT_b in zip(kbq_list, kT_list)]
    a_list = [jnp.where(strict, sc[:C] * e, 0.0) for sc, e in zip(sc_list, e_list)]
    qk_list = [jnp.where(causal, sc[C:] * e, 0.0).astype(BF16) for sc, e in zip(sc_list, e_list)]

    tinv_list = _unit_lower_inverse(a_list, ri, ci, C)

    uw_list = [_bdot(t, rhs) for t, rhs in zip(tinv_list, rhs_list)]
    s_list = [s_ref[b, h] for b, h in units]
    ws_list = [_bdot(jnp.concatenate([uw[:, Dh:].astype(BF16), qd], axis=0), s.astype(BF16))
               for uw, qd, s in zip(uw_list, qd_list, s_list)]
    vn_list = [(uw[:, :Dh] - ws[:C]).astype(BF16) for uw, ws in zip(uw_list, ws_list)]
    ov_list = [_bdot(jnp.concatenate([qk, kdT], axis=0), vn)
               for qk, kdT, vn in zip(qk_list, kdT_list, vn_list)]
    for i, (b, h) in enumerate(units):
        s_ref[b, h] = s_list[i] * gl_list[i] + ov_list[i][C:]
    for i, (b, h) in enumerate(units):
        o = ws_list[i][C:] + ov_list[i][:C]
        o = o * lax.rsqrt(jnp.mean(o * o, -1, keepdims=True) + RMS_EPS) * ng
        y_ref[b, :, h * Dh:(h + 1) * Dh] = o * _silu(z_ref[b, :, h * Dh:(h + 1) * Dh])

    sT_ref[...] = s_ref[...]


def _gdn(conv, z, ba, s0, a_log, dt_bias, norm_g, chunk, bt):
    B, L, _ = conv.shape
    H, Dh = GDN_HEADS, GDN_HEAD_DIM
    lt = chunk
    place = lambda t: jnp.zeros((1, LANES), F32).at[0, H:2 * H].set(t)
    xs = lambda w: pl.BlockSpec((bt, lt, w), lambda b, i: (b, i, 0))
    full = lambda *shape: pl.BlockSpec(shape, lambda b, i: (0,) * len(shape))
    state = pl.BlockSpec((bt, H, Dh, Dh), lambda b, i: (b, 0, 0, 0))
    return pl.pallas_call(
        functools.partial(_gdn_kernel, chunk=chunk, bt=bt),
        grid=(B // bt, L // lt),
        in_specs=[xs(CONV_CH), xs(GDN_WIDTH), xs(LANES), state,
                  full(1, LANES), full(1, LANES), full(1, Dh)],
        out_specs=[xs(GDN_WIDTH), state],
        out_shape=[jax.ShapeDtypeStruct((B, L, GDN_WIDTH), F32),
                   jax.ShapeDtypeStruct((B, H, Dh, Dh), F32)],
        scratch_shapes=[pltpu.VMEM((bt, H, Dh, Dh), F32)],
        compiler_params=_cparams("arbitrary", "arbitrary"),
        name="gdn",
    )(conv, z, ba, s0, place(a_log), place(dt_bias), norm_g.reshape(1, Dh))


FF_BLK = 256


def _gelu_tanh(x):
    return 0.5 * x * (1.0 + jnp.tanh(0.7978845608028654 * (x + 0.044715 * x * x * x)))


def _outffn_kernel(x_ref, ya_ref, yb_ref, g1_ref, sh2_ref, sc2_ref, g2_ref,
                   wglu_ref, wout_ref, l1g_ref, l1b_ref, wup_ref, wdn_ref, l2g_ref, l2b_ref,
                   o_ref):
    bt, lt, d = x_ref.shape
    if bt % 2 == 0:
        subs = [(slice(i * bt // 2, (i + 1) * bt // 2), slice(0, lt)) for i in range(2)]
    else:
        subs = [(slice(0, bt), slice(i * lt // 2, (i + 1) * lt // 2)) for i in range(2)]
    rd = lambda ref, sub: ref[sub[0], sub[1], :]
    flat = lambda t: t.reshape(t.shape[0] * t.shape[1], t.shape[-1])
    unflat = lambda t, like: t.reshape(like.shape[0], like.shape[1], t.shape[-1])
    mod = lambda ref, sub: ref[sub[0], 0]
    nsub = range(len(subs))

    zz = [_mm(_gelu_tanh(flat(rd(ya_ref, s))), wglu_ref[...]) for s in subs]
    y_a = [z[:, :S5_WIDTH] * _sigmoid(z[:, S5_WIDTH:]) for z in zz]
    mix_b = [_mm(flat(rd(yb_ref, s)), wout_ref[S5_WIDTH:, :]) for s in subs]
    mix = [mb + _mm(ya, wout_ref[:S5_WIDTH, :]) for mb, ya in zip(mix_b, y_a)]
    xs = [rd(x_ref, s) for s in subs]
    x1 = [_ln(ALPHA * x + (1.0 + mod(g1_ref, s)) * unflat(mx, x)) * l1g_ref[...] + l1b_ref[...]
          for x, mx, s in zip(xs, mix, subs)]
    h = [flat(_ln(x) * (1.0 + mod(sc2_ref, s)) + mod(sh2_ref, s)).astype(BF16) for x, s in zip(x1, subs)]

    def gate_up(i, j):
        gate = _bdot(h[i], wup_ref[:, j * FF_BLK:(j + 1) * FF_BLK])
        up = _bdot(h[i], wup_ref[:, D_FF + j * FF_BLK:D_FF + (j + 1) * FF_BLK])
        return gate, up

    nblk = D_FF // FF_BLK
    acc = [None for _ in nsub]
    gu = [gate_up(i, 0) for i in nsub]
    for j in range(nblk):
        gu_next = [gate_up(i, j + 1) for i in nsub] if j + 1 < nblk else None
        for i in nsub:
            gate, up = gu[i]
            part = _mm(_silu(gate) * up, wdn_ref[j * FF_BLK:(j + 1) * FF_BLK, :])
            acc[i] = part if acc[i] is None else acc[i] + part
        gu = gu_next
    for i, s in enumerate(subs):
        o_ref[s[0], s[1], :] = (_ln(ALPHA * x1[i] + (1.0 + mod(g2_ref, s)) * unflat(acc[i], x1[i]))
                                * l2g_ref[...] + l2b_ref[...])


def _outffn(x, ya, yb, mod, w_glu, w_out, ln1_g, ln1_b, w_up, w_dn, ln2_g, ln2_b, bt, lt):
    B, L, _ = x.shape
    xs = lambda w: pl.BlockSpec((bt, lt, w), lambda b, i: (b, i, 0))
    ms = lambda idx: pl.BlockSpec((bt, 1, 1, D_MODEL), lambda b, i: (b, idx, 0, 0))
    const = lambda a: pl.BlockSpec(a.shape, lambda b, i: (0,) * a.ndim, pipeline_mode=pl.Buffered(1))
    vec = lambda t: t.reshape(1, D_MODEL)
    weights = (w_glu, w_out, vec(ln1_g), vec(ln1_b), w_up, w_dn, vec(ln2_g), vec(ln2_b))
    nsteps = (B // bt) * (L // lt)
    if nsteps <= STREAM_BUFFERS:
        return pl.pallas_call(
            _outffn_kernel,
            grid=(B // bt, L // lt),
            in_specs=[xs(D_MODEL), xs(S5_WIDTH), xs(GDN_WIDTH), ms(2), ms(3), ms(4), ms(5)]
                     + [const(a) for a in weights],
            out_specs=xs(D_MODEL),
            out_shape=jax.ShapeDtypeStruct((B, L, D_MODEL), F32),
            compiler_params=_cparams("arbitrary", "arbitrary"),
            name="outffn",
        )(x, ya, yb, mod, mod, mod, mod, *weights)

    def outer(x_hbm, ya_hbm, yb_hbm, mod_hbm, *rest):
        w_refs, o_hbm = rest[:len(weights)], rest[len(weights)]

        def body(x_ref, ya_ref, yb_ref, g1_ref, sh2_ref, sc2_ref, g2_ref, o_ref):
            _outffn_kernel(x_ref, ya_ref, yb_ref, g1_ref, sh2_ref, sc2_ref, g2_ref, *w_refs, o_ref)

        pltpu.emit_pipeline(
            body, grid=(B // bt, L // lt),
            in_specs=[_stream_in((bt, lt, w), nsteps) for w in (D_MODEL, S5_WIDTH, GDN_WIDTH)]
                     + [ms(2), ms(3), ms(4), ms(5)],
            out_specs=[xs(D_MODEL)],
        )(x_hbm, ya_hbm, yb_hbm, mod_hbm, mod_hbm, mod_hbm, mod_hbm, o_hbm)

    hbm = pl.BlockSpec(memory_space=pl.ANY)
    vmem = pl.BlockSpec(memory_space=pltpu.VMEM)
    return pl.pallas_call(
        outer,
        in_specs=[hbm, hbm, hbm, hbm] + [vmem] * len(weights),
        out_specs=hbm,
        out_shape=jax.ShapeDtypeStruct((B, L, D_MODEL), F32),
        compiler_params=pltpu.CompilerParams(vmem_limit_bytes=VMEM_LIMIT),
        name="outffn",
    )(x, ya, yb, mod, *weights)


TOKEN_TILE = 2 * MXU_TILE
GDN_CHUNK = LANES
GDN_CHAINS_SEQ = 4


def _plan(B, L):
    if L >= TOKEN_TILE:
        bt, lt = 1, TOKEN_TILE
    else:
        bt, lt = min(B, TOKEN_TILE // L), L
    assert B % bt == 0 and L % lt == 0
    gdn_chunk = min(L, GDN_CHUNK)
    gdn_bt = min(B, GDN_CHAINS_SEQ)
    assert L % gdn_chunk == 0 and B % gdn_bt == 0
    return bt, lt, gdn_chunk, gdn_bt


def _layer(x, mod, h_re, h_im, s_gdn, conv_buf, wts):
    (w_in_p, s5_mats, s5_d, w_glu, conv_w, a_log, dt_bias, norm_g, w_out,
     ln1_g, ln1_b, w_up, w_dn, ln2_g, ln2_b) = wts
    bt, lt, gdn_chunk, gdn_bt = _plan(x.shape[0], x.shape[1])
    u, conv, z, ba, cbo = _inproj(x, mod, w_in_p, conv_buf, conv_w, bt, lt)
    w, pmat, qmat, a_t = s5_mats
    ys5, hT = _s5(u, _s5_state_in(h_re, h_im), w, pmat, qmat, a_t, s5_d, bt)
    yb, sT = _gdn(conv, z, ba, s_gdn, a_log, dt_bias, norm_g, gdn_chunk, gdn_bt)
    y = _outffn(x, ys5, yb, mod, w_glu, w_out, ln1_g, ln1_b, w_up, w_dn, ln2_g, ln2_b, bt, lt)
    o_re, o_im = _s5_state_out(hT)
    return y, o_re, o_im, sT, cbo


def kernel(x_prompt, x_sample, state_s5_re, state_s5_im, state_gdn, cache_gdn_conv, c_prompt, c_sample, w_ada, b_ada, w_in, s5_a_re, s5_a_im, s5_log_dt, s5_b_re, s5_b_im, s5_c_re, s5_c_im, s5_d, w_s5_glu, gdn_conv_w, gdn_a_log, gdn_dt_bias, gdn_norm_g, w_out, ln1_g, ln1_b, w_ffn_up, w_ffn_down, ln2_g, ln2_b):
    bp, bs = x_prompt.shape[0], x_sample.shape[0]
    assert w_ada.shape[0] == DEPTH == 1
    l = 0
    c_all = jnp.concatenate([c_prompt, c_sample], 0)
    c_all = jnp.pad(c_all, ((0, -(bp + bs) % (2 * SUBLANES)), (0, 0)))
    mod = _ada(c_all, w_ada[l], b_ada[l])[:bp + bs]
    mod = mod.reshape(bp + bs, 6, 1, D_MODEL)
    w_in_p = jnp.pad(w_in[l], ((0, 0), (0, IN_PAD - w_in.shape[-1]))).astype(BF16)
    s5_mats = _s5_prep(s5_a_re[l], s5_a_im[l], s5_log_dt[l], s5_b_re[l], s5_b_im[l],
                       s5_c_re[l], s5_c_im[l])
    wts = (w_in_p, s5_mats, s5_d[l], w_s5_glu[l].astype(BF16), gdn_conv_w[l], gdn_a_log[l],
           gdn_dt_bias[l], gdn_norm_g[l], w_out[l].astype(BF16), ln1_g[l], ln1_b[l],
           w_ffn_up[l].astype(BF16), w_ffn_down[l].astype(BF16), ln2_g[l], ln2_b[l])
    zeros = lambda *s: jnp.zeros(s, F32)
    yp, p_re, p_im, p_gdn, p_conv = _layer(
        x_prompt, mod[:bp],
        zeros(bp, S5_GROUPS, S5_STATE), zeros(bp, S5_GROUPS, S5_STATE),
        zeros(bp, GDN_HEADS, GDN_HEAD_DIM, GDN_HEAD_DIM), zeros(bp, CONV_WIDTH - 1, CONV_CH),
        wts)
    ys, s_re, s_im, s_gdn, s_conv = _layer(
        x_sample, mod[bp:], state_s5_re[l], state_s5_im[l], state_gdn[l], cache_gdn_conv[l],
        wts)
    st = lambda t: t[None]
    return (yp, ys, st(p_re), st(p_im), st(p_gdn), st(p_conv),
            st(s_re), st(s_im), st(s_gdn), st(s_conv))
```
